```python
import jax
import jax.numpy as jnp
from jax import lax
import numpy as np

D_MODEL = 1024
BATCH = 8
SEQ = 2048
DEPTH = 2
DEC_BATCH = 128
DEC_SEQ = 8
PAST_LEN = 8192
PAGE_SIZE = 128

N_META = 16
N_HEADS = 16
N_KV_HEADS = 2
HEAD_DIM = D_MODEL // N_HEADS
GROUP = N_HEADS // N_KV_HEADS
ATT_W = N_HEADS * HEAD_DIM
KV_W = N_KV_HEADS * HEAD_DIM
WINDOW = 128
BLOCK = 128
ATT_SCALE = HEAD_DIM ** -0.5
REC_W = 5 * D_MODEL // 4
REC_BLOCKS = 16
REC_BW = REC_W // REC_BLOCKS
CONV_W = 4
LRU_C = 8.0
D_FF = 3 * D_MODEL
N_EXPERTS = 8
TOP_K = 2
EXPERT_FF = 3 * D_MODEL // 2
N_DENSE = (DEPTH + 1) // 2
N_MOE = DEPTH // 2
IN_SIZES = (ATT_W, KV_W, KV_W, REC_W, REC_W, D_MODEL, D_MODEL)
IN_W = sum(IN_SIZES)
RMS_EPS = 1e-6

kernel_name = 'hybrid_swa_sink_rglru_moe_step'


def rmsnorm(x, g):
    xf = x.astype(jnp.float32)
    y = xf * lax.rsqrt(jnp.mean(xf * xf, axis=-1, keepdims=True) + RMS_EPS)
    return (y * g.astype(jnp.float32)).astype(x.dtype)


def swiglu(h, w_gate, w_up, w_down):
    return (jax.nn.silu(h @ w_gate) * (h @ w_up)) @ w_down


def moe_ffn(h, w_router, b_router, w_gate, w_up, w_down):
    logits = (h @ w_router).astype(jnp.float32) + b_router.astype(jnp.float32)
    top_v, top_i = lax.top_k(logits, TOP_K)
    top_w = jax.nn.softmax(top_v, axis=-1)
    combine = jnp.sum(jax.nn.one_hot(top_i, N_EXPERTS, dtype=jnp.float32) * top_w[..., None], axis=-2)
    out = jnp.zeros_like(h)
    for e in range(N_EXPERTS):
        out = out + combine[..., e:e + 1].astype(h.dtype) * swiglu(h, w_gate[e], w_up[e], w_down[e])
    return out


def sink_softmax(s, mask, sinks):
    s = jnp.where(mask, s, -jnp.inf)
    sk = sinks[:, :, None, None]
    m = jnp.maximum(jnp.max(s, axis=-1, keepdims=True), sk)
    p = jnp.exp(s - m)
    return p / (jnp.sum(p, axis=-1, keepdims=True) + jnp.exp(sk - m))


def banded_window_attention(q, k, v, sinks):
    b, l = q.shape[0], q.shape[1]
    pad = (-l) % BLOCK
    lp = l + pad
    nb = lp // BLOCK

    def to_blocks(t, *tail):
        return jnp.pad(t, ((0, 0), (pad, 0), (0, 0), (0, 0))).reshape(b, nb, BLOCK, *tail)

    qb = to_blocks(q, N_KV_HEADS, GROUP, HEAD_DIM)
    kb = to_blocks(k, N_KV_HEADS, HEAD_DIM)
    vb = to_blocks(v, N_KV_HEADS, HEAD_DIM)

    def with_prev(t):
        prev = jnp.concatenate([jnp.zeros_like(t[:, :1]), t[:, :-1]], axis=1)
        return jnp.concatenate([prev, t], axis=2)

    kk = with_prev(kb)
    vv = with_prev(vb)
    qpos = jnp.arange(lp).reshape(nb, BLOCK) - pad
    kpos = qpos[:, :1] - BLOCK + jnp.arange(2 * BLOCK)[None, :]
    d = qpos[:, :, None] - kpos[:, None, :]
    mask = (kpos[:, None, :] >= 0) & (d >= 0) & (d <= WINDOW)
    s = jnp.einsum('bnqkgd,bnskd->bnkgqs', qb, kk, preferred_element_type=jnp.float32) * ATT_SCALE
    p = sink_softmax(s, mask[:, None, None], sinks.reshape(N_KV_HEADS, GROUP).astype(jnp.float32))
    o = jnp.einsum('bnkgqs,bnskd->bnqkgd', p.astype(vv.dtype), vv)
    o = o.reshape(b, lp, ATT_W)[:, pad:]
    return o, k[:, -WINDOW:], v[:, -WINDOW:]


def cached_window_attention(q, k, v, cache_k, cache_v, sinks):
    b, s_new = q.shape[0], q.shape[1]
    kk = jnp.concatenate([cache_k.astype(k.dtype), k], axis=1)
    vv = jnp.concatenate([cache_v.astype(v.dtype), v], axis=1)
    qpos = PAST_LEN + jnp.arange(s_new)
    kpos = PAST_LEN - WINDOW + jnp.arange(WINDOW + s_new)
    d = qpos[:, None] - kpos[None, :]
    mask = (d >= 0) & (d <= WINDOW)
    qg = q.reshape(b, s_new, N_KV_HEADS, GROUP, HEAD_DIM)
    s = jnp.einsum('bqkgd,bskd->bkgqs', qg, kk, preferred_element_type=jnp.float32) * ATT_SCALE
    p = sink_softmax(s, mask, sinks.reshape(N_KV_HEADS, GROUP).astype(jnp.float32))
    o = jnp.einsum('bkgqs,bskd->bqkgd', p.astype(vv.dtype), vv).reshape(b, s_new, ATT_W)
    return o, kk[:, -WINDOW:], vv[:, -WINDOW:]


def rglru_branch(xr, conv_prev, h_prev, pos, w_conv, b_conv, w_a, b_a, w_i, b_i, lam):
    b, l = xr.shape[0], xr.shape[1]
    xp = jnp.concatenate([conv_prev.astype(xr.dtype), xr], axis=1)
    xc = b_conv
    for j in range(CONV_W):
        xc = xc + xp[:, j:j + l] * w_conv[j]
    xb = xc.reshape(b, l, REC_BLOCKS, REC_BW)
    gate_a = jnp.einsum('blnc,ncd->blnd', xb, w_a).reshape(b, l, REC_W) + b_a
    gate_x = jnp.einsum('blnc,ncd->blnd', xb, w_i).reshape(b, l, REC_W) + b_i
    r = jax.nn.sigmoid(gate_a.astype(jnp.float32))
    i = jax.nn.sigmoid(gate_x.astype(jnp.float32))
    log_a = -LRU_C * r * jax.nn.softplus(-lam.astype(jnp.float32))
    a = jnp.exp(log_a)
    mult = jnp.where((pos == 0)[None, :, None], 1.0, jnp.sqrt(-jnp.expm1(2.0 * log_a)))
    bt = mult * i * xc.astype(jnp.float32)
    bt = bt.at[:, 0].add(a[:, 0] * h_prev.astype(jnp.float32))

    def combine(left, right):
        a_l, b_l = left
        a_r, b_r = right
        return a_l * a_r, a_r * b_l + b_r

    _, h = lax.associative_scan(combine, (a, bt), axis=1)
    h = h.astype(xr.dtype)
    return h, h[:, -1], xp[:, -(CONV_W - 1):]


def mixer_block(x, pos, attend, conv_prev, h_prev, p, l):
    b, n = x.shape[0], x.shape[1]
    hn = rmsnorm(x, p['ln_mix'][l])
    z = hn @ p['w_in'][l]
    offs = np.cumsum(IN_SIZES)[:-1].tolist()
    q, k, v, xr, yr, ga, gr = jnp.split(z, offs, axis=-1)
    q = q.reshape(b, n, N_HEADS, HEAD_DIM)
    k = k.reshape(b, n, N_KV_HEADS, HEAD_DIM)
    v = v.reshape(b, n, N_KV_HEADS, HEAD_DIM)
    att, k_state, v_state = attend(l, q, k, v, p['attn_sinks'][l])
    rec, h_last, conv_state = rglru_branch(
        xr, conv_prev, h_prev, pos, p['w_conv'][l], p['b_conv'][l], p['w_rg_a'][l], p['b_rg_a'][l],
        p['w_rg_i'][l], p['b_rg_i'][l], p['rg_lambda'][l])
    rec = rec * jax.nn.gelu(yr)
    merged = jax.nn.sigmoid(ga) * (att @ p['w_att_proj'][l]) + jax.nn.sigmoid(gr) * (rec @ p['w_rec_proj'][l])
    return x + merged @ p['w_out'][l], k_state, v_state, conv_state, h_last


def run_trunk(x, pos, attend, conv_prev, h_prev, p):
    ks, vs, cs, hs = [], [], [], []
    for l in range(DEPTH):
        x, k_s, v_s, c_s, h_s = mixer_block(x, pos, attend, conv_prev[l], h_prev[l], p, l)
        hf = rmsnorm(x, p['ln_ffn'][l])
        j = l // 2
        if l % 2 == 0:
            x = x + swiglu(hf, p['w_ffn_gate'][j], p['w_ffn_up'][j], p['w_ffn_down'][j])
        else:
            x = x + moe_ffn(hf, p['w_router'][j], p['b_router'][j], p['w_exp_gate'][j],
                            p['w_exp_up'][j], p['w_exp_down'][j])
        ks.append(k_s)
        vs.append(v_s)
        cs.append(c_s)
        hs.append(h_s)
    return rmsnorm(x, p['ln_final']), jnp.stack(ks), jnp.stack(vs), jnp.stack(cs), jnp.stack(hs)


def setup_inputs(seed: int = 0) -> dict:
    key = jax.random.key(seed)
    ks = jax.random.split(key, 40)
    f32 = jnp.float32

    def nrm(k, shape, scale):
        return jax.random.normal(k, shape, f32) * scale

    a0 = jax.random.uniform(ks[30], (DEPTH, REC_W), f32, minval=0.9, maxval=0.999)
    s0 = a0 ** (1.0 / LRU_C)
    return {
        'x_prompt': nrm(ks[0], (BATCH, SEQ, D_MODEL), 1.0),
        'x_sample': nrm(ks[1], (DEC_BATCH, DEC_SEQ, D_MODEL), 1.0),
        'cache_k': nrm(ks[2], (DEPTH, DEC_BATCH, WINDOW, N_KV_HEADS, HEAD_DIM), 1.0),
        'cache_v': nrm(ks[3], (DEPTH, DEC_BATCH, WINDOW, N_KV_HEADS, HEAD_DIM), 1.0),
        'state_conv': nrm(ks[4], (DEPTH, DEC_BATCH, CONV_W - 1, REC_W), 1.0),
        'state_rglru': nrm(ks[5], (DEPTH, DEC_BATCH, REC_W), 0.5),
        'meta_tokens': nrm(ks[6], (N_META, D_MODEL), 1.0),
        'ln_mix': 1.0 + nrm(ks[7], (DEPTH, D_MODEL), 0.02),
        'w_in': nrm(ks[8], (DEPTH, D_MODEL, IN_W), D_MODEL ** -0.5),
        'attn_sinks': nrm(ks[9], (DEPTH, N_HEADS), 0.5),
        'w_conv': nrm(ks[10], (DEPTH, CONV_W, REC_W), CONV_W ** -0.5),
        'b_conv': nrm(ks[11], (DEPTH, REC_W), 0.01),
        'w_rg_a': nrm(ks[12], (DEPTH, REC_BLOCKS, REC_BW, REC_BW), REC_BW ** -0.5),
        'b_rg_a': nrm(ks[13], (DEPTH, REC_W), 0.01),
        'w_rg_i': nrm(ks[14], (DEPTH, REC_BLOCKS, REC_BW, REC_BW), REC_BW ** -0.5),
        'b_rg_i': nrm(ks[15], (DEPTH, REC_W), 0.01),
        'rg_lambda': jnp.log(s0) - jnp.log1p(-s0),
        'w_att_proj': nrm(ks[16], (DEPTH, ATT_W, D_MODEL), ATT_W ** -0.5),
        'w_rec_proj': nrm(ks[17], (DEPTH, REC_W, D_MODEL), REC_W ** -0.5),
        'w_out': nrm(ks[18], (DEPTH, D_MODEL, D_MODEL), D_MODEL ** -0.5),
        'ln_ffn': 1.0 + nrm(ks[19], (DEPTH, D_MODEL), 0.02),
        'w_ffn_gate': nrm(ks[20], (N_DENSE, D_MODEL, D_FF), D_MODEL ** -0.5),
        'w_ffn_up': nrm(ks[21], (N_DENSE, D_MODEL, D_FF), D_MODEL ** -0.5),
        'w_ffn_down': nrm(ks[22], (N_DENSE, D_FF, D_MODEL), D_FF ** -0.5),
        'w_router': nrm(ks[23], (N_MOE, D_MODEL, N_EXPERTS), D_MODEL ** -0.5),
        'b_router': nrm(ks[24], (N_MOE, N_EXPERTS), 0.01),
        'w_exp_gate': nrm(ks[25], (N_MOE, N_EXPERTS, D_MODEL, EXPERT_FF), D_MODEL ** -0.5),
        'w_exp_up': nrm(ks[26], (N_MOE, N_EXPERTS, D_MODEL, EXPERT_FF), D_MODEL ** -0.5),
        'w_exp_down': nrm(ks[27], (N_MOE, N_EXPERTS, EXPERT_FF, D_MODEL), EXPERT_FF ** -0.5),
        'ln_final': 1.0 + nrm(ks[28], (D_MODEL,), 0.02),
    }


def reference(x_prompt, x_sample, cache_k, cache_v, state_conv, state_rglru, meta_tokens, ln_mix, w_in,
              attn_sinks, w_conv, b_conv, w_rg_a, b_rg_a, w_rg_i, b_rg_i, rg_lambda, w_att_proj, w_rec_proj,
              w_out, ln_ffn, w_ffn_gate, w_ffn_up, w_ffn_down, w_router, b_router, w_exp_gate, w_exp_up,
              w_exp_down, ln_final):
    p = dict(ln_mix=ln_mix, w_in=w_in, attn_sinks=attn_sinks, w_conv=w_conv, b_conv=b_conv, w_rg_a=w_rg_a,
             b_rg_a=b_rg_a, w_rg_i=w_rg_i, b_rg_i=b_rg_i, rg_lambda=rg_lambda, w_att_proj=w_att_proj,
             w_rec_proj=w_rec_proj, w_out=w_out, ln_ffn=ln_ffn, w_ffn_gate=w_ffn_gate, w_ffn_up=w_ffn_up,
             w_ffn_down=w_ffn_down, w_router=w_router, b_router=b_router, w_exp_gate=w_exp_gate,
             w_exp_up=w_exp_up, w_exp_down=w_exp_down, ln_final=ln_final)

    bp = x_prompt.shape[0]
    meta = jnp.broadcast_to(meta_tokens.astype(x_prompt.dtype)[None], (bp, N_META, D_MODEL))
    xp = jnp.concatenate([meta, x_prompt], axis=1)
    pos_p = jnp.arange(xp.shape[1])
    conv0 = jnp.zeros((DEPTH, bp, CONV_W - 1, REC_W), x_prompt.dtype)
    h0 = jnp.zeros((DEPTH, bp, REC_W), x_prompt.dtype)

    def attend_prompt(l, q, k, v, sinks):
        return banded_window_attention(q, k, v, sinks)

    y_p, new_k_prompt, new_v_prompt, new_conv_prompt, new_h_prompt = run_trunk(
        xp, pos_p, attend_prompt, conv0, h0, p)
    y_prompt = y_p[:, N_META:]

    pos_s = PAST_LEN + jnp.arange(x_sample.shape[1])

    def attend_sample(l, q, k, v, sinks):
        return cached_window_attention(q, k, v, cache_k[l], cache_v[l], sinks)

    y_sample, new_k_sample, new_v_sample, new_conv_sample, new_h_sample = run_trunk(
        x_sample, pos_s, attend_sample, state_conv, state_rglru, p)

    return (y_prompt, y_sample, new_k_prompt, new_v_prompt, new_conv_prompt, new_h_prompt,
            new_k_sample, new_v_sample, new_conv_sample, new_h_sample)
```

```python
import functools

import jax
import jax.numpy as jnp
from jax import lax
from jax.experimental import pallas as pl
from jax.experimental.pallas import tpu as pltpu

F32 = jnp.float32
BF16 = jnp.bfloat16

D_MODEL = 1024
N_META = 16
N_HEADS = 16
N_KV_HEADS = 2
HEAD_DIM = D_MODEL // N_HEADS
GROUP = N_HEADS // N_KV_HEADS
ATT_W = N_HEADS * HEAD_DIM
KV_W = N_KV_HEADS * HEAD_DIM
WINDOW = 128
BLOCK = 128
ATT_SCALE = HEAD_DIM ** -0.5
REC_W = 5 * D_MODEL // 4
REC_BLOCKS = 16
REC_BW = REC_W // REC_BLOCKS
REC_HALF = REC_W // 2
CONV_W = 4
LRU_C = 8.0
D_FF = 3 * D_MODEL
N_EXPERTS = 8
EXPERT_FF = 3 * D_MODEL // 2
PAST_LEN = 8192
RMS_EPS = 1e-6
IN_OFFS = (0, ATT_W, ATT_W + 2 * KV_W, ATT_W + 2 * KV_W + REC_W, ATT_W + 2 * KV_W + 2 * REC_W,
           ATT_W + 2 * KV_W + 2 * REC_W + 2 * D_MODEL)
LANES = 128
VMEM_LIMIT = 52 * 1024 * 1024

TM = 512
ROUTER_PAD_BIAS = -1e30


def _cparams(*sem):
    return pltpu.CompilerParams(dimension_semantics=sem, vmem_limit_bytes=VMEM_LIMIT)


def _resident(shape):
    nd = len(shape)
    return pl.BlockSpec(shape, lambda *_: (0,) * nd, pipeline_mode=pl.Buffered(1))


def _rms(x, g):
    return x * lax.rsqrt(jnp.mean(x * x, axis=-1, keepdims=True) + RMS_EPS) * g


def _in_proj_kernel(x_ref, ln_ref, w_ref, q_ref, kv_ref, xr_ref, yr_ref, g_ref):
    hn = _rms(x_ref[...], ln_ref[...]).astype(BF16)

    def proj(a, b):
        return jnp.dot(hn, w_ref[:, a:b], preferred_element_type=F32)

    q_ref[...] = (proj(IN_OFFS[0], IN_OFFS[1]) * ATT_SCALE).astype(BF16)
    kv_ref[...] = proj(IN_OFFS[1], IN_OFFS[2])
    xr_ref[...] = proj(IN_OFFS[2], IN_OFFS[3])
    yr_ref[...] = proj(IN_OFFS[3], IN_OFFS[4])
    g_ref[...] = proj(IN_OFFS[4], IN_OFFS[5])


def _in_proj(x, ln, w_in):
    t = x.shape[0]
    row = lambda w: pl.BlockSpec((TM, w), lambda i: (i, 0))
    return pl.pallas_call(
        _in_proj_kernel,
        grid=(t // TM,),
        in_specs=[row(D_MODEL), _resident((1, D_MODEL)), _resident(w_in.shape)],
        out_specs=[row(ATT_W), row(2 * KV_W), row(REC_W), row(REC_W), row(2 * D_MODEL)],
        out_shape=[jax.ShapeDtypeStruct((t, ATT_W), BF16), jax.ShapeDtypeStruct((t, 2 * KV_W), F32),
                   jax.ShapeDtypeStruct((t, REC_W), F32), jax.ShapeDtypeStruct((t, REC_W), F32),
                   jax.ShapeDtypeStruct((t, 2 * D_MODEL), F32)],
        compiler_params=_cparams("parallel"),
        name="in_proj",
    )(x, ln, w_in)


def _sink_softmax(s, mask, sk):
    s = jnp.where(mask, s, -jnp.inf)
    m = jnp.maximum(jnp.max(s, axis=-1, keepdims=True), sk)
    p = jnp.exp(s - m)
    return p / (jnp.sum(p, axis=-1, keepdims=True) + jnp.exp(sk - m))


def _attn_prompt_kernel(pad, sink_ref, q_ref, kvp_ref, kvc_ref, o_ref):
    n = pl.program_id(1)
    q = q_ref[...]
    kv = jnp.concatenate([kvp_ref[...], kvc_ref[...]], axis=0).astype(BF16)
    i = lax.broadcasted_iota(jnp.int32, (BLOCK, 2 * BLOCK), 0)
    j = lax.broadcasted_iota(jnp.int32, (BLOCK, 2 * BLOCK), 1)
    mask = (j >= i) & (j <= i + WINDOW) & (j >= pad + BLOCK - n * BLOCK)
    outs = []
    for h in range(N_HEADS):
        kvh = h // GROUP
        qh = q[:, h * HEAD_DIM:(h + 1) * HEAD_DIM]
        k = kv[:, kvh * HEAD_DIM:(kvh + 1) * HEAD_DIM]
        v = kv[:, KV_W + kvh * HEAD_DIM:KV_W + (kvh + 1) * HEAD_DIM]
        s = lax.dot_general(qh, k, (((1,), (1,)), ((), ())), preferred_element_type=F32)
        p = _sink_softmax(s, mask, sink_ref[h])
        outs.append(jnp.dot(p.astype(BF16), v, preferred_element_type=F32))
    o_ref[...] = jnp.concatenate(outs, axis=1).astype(BF16)


def _attn_prompt(q, kv, sinks, nb, lp, pad):
    t = q.shape[0]
    q2 = q.reshape(t // nb, nb * ATT_W)
    kv2 = kv.reshape(t // nb, nb * 2 * KV_W)
    out = pl.pallas_call(
        functools.partial(_attn_prompt_kernel, pad),
        grid=(nb, lp // BLOCK),
        in_specs=[pl.BlockSpec(memory_space=pltpu.SMEM),
                  pl.BlockSpec((BLOCK, ATT_W), lambda b, n: (n, b)),
                  pl.BlockSpec((BLOCK, 2 * KV_W), lambda b, n: (jnp.maximum(n - 1, 0), b)),
                  pl.BlockSpec((BLOCK, 2 * KV_W), lambda b, n: (n, b))],
        out_specs=pl.BlockSpec((BLOCK, ATT_W), lambda b, n: (n, b)),
        out_shape=jax.ShapeDtypeStruct((lp, nb * ATT_W), BF16),
        compiler_params=_cparams("parallel", "parallel"),
        name="attn_prompt",
    )(sinks, q2, kv2, kv2)
    return out.reshape(lp * nb, ATT_W)


SAMPLE_BB = 8


def _attn_sample_kernel(ds, sink_ref, q_ref, kvn_ref, ck_ref, cv_ref, o_ref, nk_ref, nv_ref):
    rows = GROUP * ds
    r = lax.broadcasted_iota(jnp.int32, (rows, WINDOW + ds), 0)
    j = lax.broadcasted_iota(jnp.int32, (rows, WINDOW + ds), 1)
    tq = r & (ds - 1)
    mask = (j >= tq) & (j <= tq + WINDOW)
    for bl in range(SAMPLE_BB):
        qb = q_ref[:, bl * ATT_W:(bl + 1) * ATT_W].astype(F32)
        kvb = kvn_ref[:, bl * 2 * KV_W:(bl + 1) * 2 * KV_W]
        ck = ck_ref[bl]
        cv = cv_ref[bl]
        nk_ref[bl] = jnp.concatenate([ck[ds:], kvb[:, :KV_W]], axis=0)
        nv_ref[bl] = jnp.concatenate([cv[ds:], kvb[:, KV_W:]], axis=0)
        pieces = []
        for kvh in range(N_KV_HEADS):
            lo, hi = kvh * HEAD_DIM, (kvh + 1) * HEAD_DIM
            k = jnp.concatenate([ck[:, lo:hi], kvb[:, lo:hi]], axis=0).astype(BF16)
            v = jnp.concatenate([cv[:, lo:hi], kvb[:, KV_W + lo:KV_W + hi]], axis=0).astype(BF16)
            qg = jnp.concatenate(
                [qb[:, (kvh * GROUP + g) * HEAD_DIM:(kvh * GROUP + g + 1) * HEAD_DIM] for g in range(GROUP)],
                axis=0).astype(BF16)
            s = lax.dot_general(qg, k, (((1,), (1,)), ((), ())), preferred_element_type=F32)
            p = _sink_softmax(s, mask, sink_ref[kvh][:, 0:1])
            o = jnp.dot(p.astype(BF16), v, preferred_element_type=F32)
            pieces += [o[g * ds:(g + 1) * ds] for g in range(GROUP)]
        o_ref[:, bl * ATT_W:(bl + 1) * ATT_W] = jnp.concatenate(pieces, axis=1).astype(BF16)


def _attn_sample(q, kv, cache_k, cache_v, sinks, nb, ds):
    q2 = q.reshape(ds, nb * ATT_W)
    kv2 = kv.reshape(ds, nb * 2 * KV_W)
    sink_rows = jnp.broadcast_to(
        jnp.repeat(sinks.reshape(N_KV_HEADS, GROUP), ds, axis=1)[:, :, None], (N_KV_HEADS, GROUP * ds, LANES))
    cache_spec = pl.BlockSpec((SAMPLE_BB, WINDOW, KV_W), lambda i: (i, 0, 0))
    att, nk, nv = pl.pallas_call(
        functools.partial(_attn_sample_kernel, ds),
        grid=(nb // SAMPLE_BB,),
        in_specs=[_resident(sink_rows.shape),
                  pl.BlockSpec((ds, SAMPLE_BB * ATT_W), lambda i: (0, i)),
                  pl.BlockSpec((ds, SAMPLE_BB * 2 * KV_W), lambda i: (0, i)),
                  cache_spec, cache_spec],
        out_specs=[pl.BlockSpec((ds, SAMPLE_BB * ATT_W), lambda i: (0, i)), cache_spec, cache_spec],
        out_shape=[jax.ShapeDtypeStruct((ds, nb * ATT_W), BF16),
                   jax.ShapeDtypeStruct(cache_k.shape, F32), jax.ShapeDtypeStruct(cache_v.shape, F32)],
        compiler_params=_cparams("parallel"),
        name="attn_sample",
    )(sink_rows, q2, kv2, cache_k, cache_v)
    return att.reshape(ds * nb, ATT_W), nk, nv


RG_ROWS = 256
RG_TCN = 64


def _rglru_kernel(nb, tcn, pos0, xr_ref, yr_ref, cprev_ref, hprev_ref, wc_ref, bc_ref, wg_ref, ba_ref, bi_ref,
                  lam_ref, rec_ref, hlast_ref, xbuf, a_scr, b_scr, hc):
    step = pl.program_id(0)
    rows = tcn * nb
    tail = (CONV_W - 1) * nb

    @pl.when(step == 0)
    def _():
        xbuf[0:tail, :] = cprev_ref[...]
        hc[...] = hprev_ref[...]

    @pl.when(step > 0)
    def _():
        xbuf[0:tail, :] = xbuf[rows:rows + tail, :]

    def row_pos(r0, n):
        t_row = lax.shift_right_logical(r0 + lax.broadcasted_iota(jnp.int32, (n, 1), 0), nb.bit_length() - 1)
        return pos0 + step * tcn + t_row

    sub = min(RG_ROWS, rows)
    for r0 in range(0, rows, sub):
        x = xr_ref[r0:r0 + sub, :]
        if pos0 < 0:
            x = jnp.where(row_pos(r0, sub) >= 0, x, 0.0)
        xbuf[tail + r0:tail + r0 + sub, :] = x

    lam = lam_ref[...]
    neg_c_softplus = -LRU_C * (jnp.maximum(-lam, 0.0) + jnp.log1p(jnp.exp(-jnp.abs(lam))))

    for r0 in range(0, rows, sub):
        xc = bc_ref[...]
        for jj in range(CONV_W):
            xc = xc + xbuf[r0 + jj * nb:r0 + jj * nb + sub, :] * wc_ref[jj:jj + 1, :]
        xcb = xc.astype(BF16)
        pos = row_pos(r0, sub)
        for c in range(2):
            lo, hi = c * REC_HALF, (c + 1) * REC_HALF
            gates = jnp.dot(xcb[:, lo:hi], wg_ref[c], preferred_element_type=F32)
            r = jax.nn.sigmoid(gates[:, :REC_HALF] + ba_ref[:, lo:hi])
            ig = jax.nn.sigmoid(gates[:, REC_HALF:] + bi_ref[:, lo:hi])
            a = jnp.exp(r * neg_c_softplus[:, lo:hi])
            mult = jnp.sqrt(1.0 - a * a)
            if pos0 <= 0:
                mult = jnp.where(pos == 0, 1.0, mult)
            bt = mult * ig * xc[:, lo:hi]
            if pos0 < 0:
                bt = jnp.where(pos >= 0, bt, 0.0)
            a_scr[r0:r0 + sub, lo:hi] = a
            b_scr[r0:r0 + sub, lo:hi] = bt

    if nb * REC_W <= 16 * 1024:
        def body(t, h):
            rr = pl.multiple_of(t * nb, nb)
            h = a_scr[pl.ds(rr, nb), :] * h + b_scr[pl.ds(rr, nb), :]
            b_scr[pl.ds(rr, nb), :] = h
            return h
        hc[...] = lax.fori_loop(0, tcn, body, hc[...])
    else:
        for t in range(tcn):
            h = a_scr[t * nb:(t + 1) * nb, :] * hc[...] + b_scr[t * nb:(t + 1) * nb, :]
            b_scr[t * nb:(t + 1) * nb, :] = h
            hc[...] = h

    hlast_ref[...] = hc[...]
    for r0 in range(0, rows, sub):
        rec_ref[r0:r0 + sub, :] = (b_scr[r0:r0 + sub, :] * jax.nn.gelu(yr_ref[r0:r0 + sub, :])).astype(BF16)


def _rglru(xr, yr, row_block0, n_chunks, nb, tcn, pos0, conv_prev, h_prev, wts):
    rows = tcn * nb
    tail = (CONV_W - 1) * nb
    blk = pl.BlockSpec((rows, REC_W), lambda i: (row_block0 + i, 0))
    rec, h_last = pl.pallas_call(
        functools.partial(_rglru_kernel, nb, tcn, pos0),
        grid=(n_chunks,),
        in_specs=[blk, blk, _resident(conv_prev.shape), _resident(h_prev.shape)] + [_resident(w.shape) for w in wts],
        out_specs=[pl.BlockSpec((rows, REC_W), lambda i: (i, 0)), pl.BlockSpec((nb, REC_W), lambda i: (0, 0))],
        out_shape=[jax.ShapeDtypeStruct((n_chunks * rows, REC_W), BF16), jax.ShapeDtypeStruct((nb, REC_W), F32)],
        scratch_shapes=[pltpu.VMEM((rows + tail, REC_W), F32), pltpu.VMEM((rows, REC_W), F32),
                        pltpu.VMEM((rows, REC_W), F32), pltpu.VMEM((nb, REC_W), F32)],
        compiler_params=_cparams("arbitrary"),
        name="rglru",
    )(xr, yr, conv_prev, h_prev, *wts)
    return rec, h_last


def _merge_kernel(n_p, attp_ref, atts_ref, recp_ref, recs_ref, g_ref, x_ref, wa_ref, wr_ref, wo_ref, o_ref):
    is_p = pl.program_id(0) < n_p
    att = jnp.where(is_p, attp_ref[...], atts_ref[...])
    rec = jnp.where(is_p, recp_ref[...], recs_ref[...])
    ya = jnp.dot(att, wa_ref[...], preferred_element_type=F32)
    yr = jnp.dot(rec, wr_ref[...], preferred_element_type=F32)
    merged = jax.nn.sigmoid(g_ref[:, :D_MODEL]) * ya + jax.nn.sigmoid(g_ref[:, D_MODEL:]) * yr
    o_ref[...] = x_ref[...] + jnp.dot(merged.astype(BF16), wo_ref[...], preferred_element_type=F32)


def _merge(att_p, att_s, rec_p, rec_s, g, x, wa, wr, wo):
    t = x.shape[0]
    n_p = att_p.shape[0] // TM
    n_s = att_s.shape[0] // TM
    p_spec = lambda w: pl.BlockSpec((TM, w), lambda i: (jnp.minimum(i, n_p - 1), 0))
    s_spec = lambda w: pl.BlockSpec((TM, w), lambda i: (jnp.clip(i - n_p, 0, n_s - 1), 0))
    row = lambda w: pl.BlockSpec((TM, w), lambda i: (i, 0))
    return pl.pallas_call(
        functools.partial(_merge_kernel, n_p),
        grid=(t // TM,),
        in_specs=[p_spec(ATT_W), s_spec(ATT_W), p_spec(REC_W), s_spec(REC_W), row(2 * D_MODEL), row(D_MODEL),
                  _resident(wa.shape), _resident(wr.shape), _resident(wo.shape)],
        out_specs=row(D_MODEL),
        out_shape=jax.ShapeDtypeStruct((t, D_MODEL), F32),
        compiler_params=_cparams("parallel"),
        name="merge_out",
    )(att_p, att_s, rec_p, rec_s, g, x, wa, wr, wo)


FF_CHUNK = 512


def _swiglu_mid(hn, wg_ref, wu_ref, mid_ref, width):
    for c in range(0, width, FF_CHUNK):
        gate = jnp.dot(hn, wg_ref[:, c:c + FF_CHUNK], preferred_element_type=F32)
        up = jnp.dot(hn, wu_ref[:, c:c + FF_CHUNK], preferred_element_type=F32)
        mid_ref[:, c:c + FF_CHUNK] = (jax.nn.silu(gate) * up).astype(BF16)


def _ffn_kernel(x_ref, ln_ref, wg_ref, wu_ref, wd_ref, o_ref, mid_ref):
    x = x_ref[...]
    hn = _rms(x, ln_ref[...]).astype(BF16)
    _swiglu_mid(hn, wg_ref, wu_ref, mid_ref, D_FF)
    o_ref[...] = x + jnp.dot(mid_ref[...], wd_ref[...], preferred_element_type=F32)


def _ffn(x, ln, wg, wu, wd):
    t = x.shape[0]
    row = pl.BlockSpec((TM, D_MODEL), lambda i: (i, 0))
    return pl.pallas_call(
        _ffn_kernel,
        grid=(t // TM,),
        in_specs=[row, _resident((1, D_MODEL)), _resident(wg.shape), _resident(wu.shape), _resident(wd.shape)],
        out_specs=row,
        out_shape=jax.ShapeDtypeStruct((t, D_MODEL), F32),
        scratch_shapes=[pltpu.VMEM((TM, D_FF), BF16)],
        compiler_params=_cparams("parallel"),
        name="ffn_dense",
    )(x, ln, wg, wu, wd)


def _moe_kernel(x_ref, ln_ref, wr_ref, br_ref, wg_ref, wu_ref, wd_ref, o_ref, hn_scr, comb_scr, acc_scr, mid_ref):
    e = pl.program_id(1)
    lane = lax.broadcasted_iota(jnp.int32, (TM, LANES), 1).astype(F32)

    @pl.when(e == 0)
    def _():
        hn = _rms(x_ref[...], ln_ref[...]).astype(BF16)
        hn_scr[...] = hn
        logits = jnp.dot(hn, wr_ref[...], preferred_element_type=F32) + br_ref[...]
        m1 = jnp.max(logits, axis=-1, keepdims=True)
        i1 = jnp.min(jnp.where(logits == m1, lane, float(LANES)), axis=-1, keepdims=True)
        rest = jnp.where(lane == i1, -jnp.inf, logits)
        m2 = jnp.max(rest, axis=-1, keepdims=True)
        i2 = jnp.min(jnp.where(rest == m2, lane, float(LANES)), axis=-1, keepdims=True)
        e2 = jnp.exp(m2 - m1)
        den = 1.0 + e2
        comb_scr[...] = jnp.where(lane == i1, 1.0 / den, 0.0) + jnp.where(lane == i2, e2 / den, 0.0)
        acc_scr[...] = jnp.zeros_like(acc_scr)

    _swiglu_mid(hn_scr[...], wg_ref, wu_ref, mid_ref, EXPERT_FF)
    y = jnp.dot(mid_ref[...], wd_ref[...], preferred_element_type=F32)
    w_e = jnp.sum(jnp.where(lane == e.astype(F32), comb_scr[...], 0.0), axis=-1, keepdims=True)
    acc_scr[...] += w_e * y

    @pl.when(e == N_EXPERTS - 1)
    def _():
        o_ref[...] = x_ref[...] + acc_scr[...]


def _moe(x, ln, w_router, b_router, wg, wu, wd):
    t = x.shape[0]
    row = pl.BlockSpec((TM, D_MODEL), lambda i, e: (i, 0))
    ex = lambda a, b: pl.BlockSpec((None, a, b), lambda i, e: (e, 0, 0))
    return pl.pallas_call(
        _moe_kernel,
        grid=(t // TM, N_EXPERTS),
        in_specs=[row, _resident((1, D_MODEL)), _resident(w_router.shape), _resident(b_router.shape),
                  ex(D_MODEL, EXPERT_FF), ex(D_MODEL, EXPERT_FF), ex(EXPERT_FF, D_MODEL)],
        out_specs=row,
        out_shape=jax.ShapeDtypeStruct((t, D_MODEL), F32),
        scratch_shapes=[pltpu.VMEM((TM, D_MODEL), BF16), pltpu.VMEM((TM, LANES), F32),
                        pltpu.VMEM((TM, D_MODEL), F32), pltpu.VMEM((TM, EXPERT_FF), BF16)],
        compiler_params=_cparams("parallel", "arbitrary"),
        name="moe_dense",
    )(x, ln, w_router, b_router, wg, wu, wd)


def _final_norm_kernel(x_ref, ln_ref, o_ref):
    o_ref[...] = _rms(x_ref[...], ln_ref[...])


def _final_norm(x, ln):
    t = x.shape[0]
    row = pl.BlockSpec((TM, D_MODEL), lambda i: (i, 0))
    return pl.pallas_call(
        _final_norm_kernel,
        grid=(t // TM,),
        in_specs=[row, _resident((1, D_MODEL))],
        out_specs=row,
        out_shape=jax.ShapeDtypeStruct((t, D_MODEL), F32),
        compiler_params=_cparams("parallel"),
        name="final_norm",
    )(x, ln)


def _gate_weights(w_a, w_i):
    half_blocks = REC_BLOCKS // 2
    eye = jnp.eye(half_blocks, dtype=F32)

    def dense(w):
        return (eye[:, None, :, None] * w[:, :, None, :]).reshape(REC_HALF, REC_HALF)

    halves = [jnp.concatenate([dense(w_a[c * half_blocks:(c + 1) * half_blocks]),
                               dense(w_i[c * half_blocks:(c + 1) * half_blocks])], axis=1) for c in range(2)]
    return jnp.stack(halves).astype(BF16)


def kernel(x_prompt, x_sample, cache_k, cache_v, state_conv, state_rglru, meta_tokens, ln_mix, w_in, attn_sinks,
           w_conv, b_conv, w_rg_a, b_rg_a, w_rg_i, b_rg_i, rg_lambda, w_att_proj, w_rec_proj, w_out, ln_ffn,
           w_ffn_gate, w_ffn_up, w_ffn_down, w_router, b_router, w_exp_gate, w_exp_up, w_exp_down, ln_final):
    bp, seq, _ = x_prompt.shape
    db, ds, _ = x_sample.shape
    depth = w_in.shape[0]
    pad = (-(N_META + seq)) % BLOCK
    lp = pad + N_META + seq
    tp, ts = lp * bp, ds * db
    assert tp % TM == 0 and ts % TM == 0 and db % SAMPLE_BB == 0 and ds >= CONV_W - 1
    assert seq >= WINDOW and bp & (bp - 1) == 0 and db & (db - 1) == 0 and ds & (ds - 1) == 0
    assert lp % RG_TCN == 0 and tp % ts == 0

    meta = jnp.broadcast_to(meta_tokens[:, None, :], (N_META, bp, D_MODEL))
    xp = jnp.concatenate([jnp.zeros((pad, bp, D_MODEL), F32), meta, jnp.transpose(x_prompt, (1, 0, 2))], axis=0)
    x = jnp.concatenate([xp.reshape(tp, D_MODEL), jnp.transpose(x_sample, (1, 0, 2)).reshape(ts, D_MODEL)], axis=0)

    zeros_conv = jnp.zeros(((CONV_W - 1) * bp, REC_W), F32)
    zeros_h = jnp.zeros((bp, REC_W), F32)
    outs = {k: [] for k in ("kp", "vp", "cp", "hp", "ks", "vs", "cs", "hs")}

    for l in range(depth):
        q, kv, xr, yr, g = _in_proj(x, ln_mix[l].reshape(1, D_MODEL), w_in[l].astype(BF16))

        att_p = _attn_prompt(q, kv, attn_sinks[l], bp, lp, pad)
        att_s, nk, nv = _attn_sample(q[tp:], kv[tp:], cache_k[l].reshape(db, WINDOW, KV_W),
                                     cache_v[l].reshape(db, WINDOW, KV_W), attn_sinks[l], db, ds)

        rg_w = (w_conv[l], b_conv[l].reshape(1, REC_W), _gate_weights(w_rg_a[l], w_rg_i[l]),
                b_rg_a[l].reshape(1, REC_W), b_rg_i[l].reshape(1, REC_W), rg_lambda[l].reshape(1, REC_W))
        rec_p, h_p = _rglru(xr, yr, 0, lp // RG_TCN, bp, RG_TCN, -pad, zeros_conv, zeros_h, rg_w)
        conv_s = jnp.transpose(state_conv[l], (1, 0, 2)).reshape((CONV_W - 1) * db, REC_W)
        rec_s, h_s = _rglru(xr, yr, tp // ts, 1, db, ds, PAST_LEN, conv_s, state_rglru[l], rg_w)

        x = _merge(att_p, att_s, rec_p, rec_s, g, x, w_att_proj[l].astype(BF16), w_rec_proj[l].astype(BF16),
                   w_out[l].astype(BF16))

        j = l // 2
        ln_f = ln_ffn[l].reshape(1, D_MODEL)
        if l % 2 == 0:
            x = _ffn(x, ln_f, w_ffn_gate[j].astype(BF16), w_ffn_up[j].astype(BF16), w_ffn_down[j].astype(BF16))
        else:
            w_r = jnp.zeros((D_MODEL, LANES), F32).at[:, :N_EXPERTS].set(w_router[j]).astype(BF16)
            b_r = jnp.full((1, LANES), ROUTER_PAD_BIAS, F32).at[0, :N_EXPERTS].set(b_router[j])
            x = _moe(x, ln_f, w_r, b_r, w_exp_gate[j].astype(BF16), w_exp_up[j].astype(BF16),
                     w_exp_down[j].astype(BF16))

        kv_p = kv[:tp].reshape(lp, bp, 2 * KV_W)[lp - WINDOW:]
        outs["kp"].append(jnp.transpose(kv_p[..., :KV_W], (1, 0, 2)).reshape(bp, WINDOW, N_KV_HEADS, HEAD_DIM))
        outs["vp"].append(jnp.transpose(kv_p[..., KV_W:], (1, 0, 2)).reshape(bp, WINDOW, N_KV_HEADS, HEAD_DIM))
        outs["cp"].append(jnp.transpose(xr[:tp].reshape(lp, bp, REC_W)[lp - (CONV_W - 1):], (1, 0, 2)))
        outs["hp"].append(h_p)
        outs["ks"].append(nk.reshape(db, WINDOW, N_KV_HEADS, HEAD_DIM))
        outs["vs"].append(nv.reshape(db, WINDOW, N_KV_HEADS, HEAD_DIM))
        outs["cs"].append(jnp.transpose(xr[tp:].reshape(ds, db, REC_W)[ds - (CONV_W - 1):], (1, 0, 2)))
        outs["hs"].append(h_s)

    y = _final_norm(x, ln_final.reshape(1, D_MODEL))
    y_prompt = jnp.transpose(y[:tp].reshape(lp, bp, D_MODEL)[pad + N_META:], (1, 0, 2))
    y_sample = jnp.transpose(y[tp:].reshape(ds, db, D_MODEL), (1, 0, 2))
    st = {k: jnp.stack(v) for k, v in outs.items()}
    return (y_prompt, y_sample, st["kp"], st["vp"], st["cp"], st["hp"], st["ks"], st["vs"], st["cs"], st["hs"])
```

```python
import functools

import jax
import jax.numpy as jnp
from jax import lax
from jax.experimental import pallas as pl
from jax.experimental.pallas import tpu as pltpu

F32 = jnp.float32
BF16 = jnp.bfloat16

D_MODEL = 1024
N_META = 16
N_HEADS = 16
N_KV_HEADS = 2
HEAD_DIM = D_MODEL // N_HEADS
GROUP = N_HEADS // N_KV_HEADS
ATT_W = N_HEADS * HEAD_DIM
KV_W = N_KV_HEADS * HEAD_DIM
WINDOW = 128
BLOCK = 128
ATT_SCALE = HEAD_DIM ** -0.5
REC_W = 5 * D_MODEL // 4
REC_BLOCKS = 16
REC_BW = REC_W // REC_BLOCKS
REC_HALF = REC_W // 2
CONV_W = 4
LRU_C = 8.0
D_FF = 3 * D_MODEL
N_EXPERTS = 8
EXPERT_FF = 3 * D_MODEL // 2
PAST_LEN = 8192
RMS_EPS = 1e-6
IN_OFFS = (0, ATT_W, ATT_W + 2 * KV_W, ATT_W + 2 * KV_W + REC_W, ATT_W + 2 * KV_W + 2 * REC_W,
           ATT_W + 2 * KV_W + 2 * REC_W + 2 * D_MODEL)
LANES = 128
VMEM_LIMIT = 52 * 1024 * 1024

TM = 512
ROUTER_PAD_BIAS = -1e30


def _cparams(*sem):
    return pltpu.CompilerParams(dimension_semantics=sem, vmem_limit_bytes=VMEM_LIMIT)


def _resident(shape):
    nd = len(shape)
    return pl.BlockSpec(shape, lambda *_: (0,) * nd, pipeline_mode=pl.Buffered(1))


def _rms(x, g):
    return x * lax.rsqrt(jnp.mean(x * x, axis=-1, keepdims=True) + RMS_EPS) * g


def _in_proj_kernel(x_ref, ln_ref, w_ref, q_ref, kv_ref, xr_ref, yr_ref, g_ref):
    hn = _rms(x_ref[...], ln_ref[...]).astype(BF16)

    def proj(a, b):
        return jnp.dot(hn, w_ref[:, a:b], preferred_element_type=F32)

    q_ref[...] = (proj(IN_OFFS[0], IN_OFFS[1]) * ATT_SCALE).astype(BF16)
    kv_ref[...] = proj(IN_OFFS[1], IN_OFFS[2])
    xr_ref[...] = proj(IN_OFFS[2], IN_OFFS[3])
    yr_ref[...] = proj(IN_OFFS[3], IN_OFFS[4])
    g_ref[...] = proj(IN_OFFS[4], IN_OFFS[5])


def _in_proj(x, ln, w_in):
    t = x.shape[0]
    row = lambda w: pl.BlockSpec((TM, w), lambda i: (i, 0))
    return pl.pallas_call(
        _in_proj_kernel,
        grid=(t // TM,),
        in_specs=[row(D_MODEL), _resident((1, D_MODEL)), _resident(w_in.shape)],
        out_specs=[row(ATT_W), row(2 * KV_W), row(REC_W), row(REC_W), row(2 * D_MODEL)],
        out_shape=[jax.ShapeDtypeStruct((t, ATT_W), BF16), jax.ShapeDtypeStruct((t, 2 * KV_W), F32),
                   jax.ShapeDtypeStruct((t, REC_W), F32), jax.ShapeDtypeStruct((t, REC_W), F32),
                   jax.ShapeDtypeStruct((t, 2 * D_MODEL), F32)],
        compiler_params=_cparams("parallel"),
        name="in_proj",
    )(x, ln, w_in)


def _sink_softmax(s, mask, sk):
    s = jnp.where(mask, s, -jnp.inf)
    m = jnp.maximum(jnp.max(s, axis=-1, keepdims=True), sk)
    p = jnp.exp(s - m)
    return p / (jnp.sum(p, axis=-1, keepdims=True) + jnp.exp(sk - m))


def _attn_prompt_kernel(pad, sink_ref, q_ref, kvp_ref, kvc_ref, o_ref):
    n = pl.program_id(1)
    q = q_ref[...]
    kv = jnp.concatenate([kvp_ref[...], kvc_ref[...]], axis=0).astype(BF16)
    i = lax.broadcasted_iota(jnp.int32, (BLOCK, 2 * BLOCK), 0)
    j = lax.broadcasted_iota(jnp.int32, (BLOCK, 2 * BLOCK), 1)
    mask = (j >= i) & (j <= i + WINDOW) & (j >= pad + BLOCK - n * BLOCK)
    outs = []
    for h in range(N_HEADS):
        kvh = h // GROUP
        qh = q[:, h * HEAD_DIM:(h + 1) * HEAD_DIM]
        k = kv[:, kvh * HEAD_DIM:(kvh + 1) * HEAD_DIM]
        v = kv[:, KV_W + kvh * HEAD_DIM:KV_W + (kvh + 1) * HEAD_DIM]
        s = lax.dot_general(qh, k, (((1,), (1,)), ((), ())), preferred_element_type=F32)
        p = _sink_softmax(s, mask, sink_ref[h])
        outs.append(jnp.dot(p.astype(BF16), v, preferred_element_type=F32))
    o_ref[...] = jnp.concatenate(outs, axis=1).astype(BF16)


def _attn_prompt(q, kv, sinks, nb, lp, pad):
    t = q.shape[0]
    q2 = q.reshape(t // nb, nb * ATT_W)
    kv2 = kv.reshape(t // nb, nb * 2 * KV_W)
    out = pl.pallas_call(
        functools.partial(_attn_prompt_kernel, pad),
        grid=(nb, lp // BLOCK),
        in_specs=[pl.BlockSpec(memory_space=pltpu.SMEM),
                  pl.BlockSpec((BLOCK, ATT_W), lambda b, n: (n, b)),
                  pl.BlockSpec((BLOCK, 2 * KV_W), lambda b, n: (jnp.maximum(n - 1, 0), b)),
                  pl.BlockSpec((BLOCK, 2 * KV_W), lambda b, n: (n, b))],
        out_specs=pl.BlockSpec((BLOCK, ATT_W), lambda b, n: (n, b)),
        out_shape=jax.ShapeDtypeStruct((lp, nb * ATT_W), BF16),
        compiler_params=_cparams("parallel", "parallel"),
        name="attn_prompt",
    )(sinks, q2, kv2, kv2)
    return out.reshape(lp * nb, ATT_W)


SAMPLE_BB = 8


def _attn_sample_kernel(ds, sink_ref, q_ref, kvn_ref, ck_ref, cv_ref, o_ref, nk_ref, nv_ref):
    rows = GROUP * ds
    r = lax.broadcasted_iota(jnp.int32, (rows, WINDOW + ds), 0)
    j = lax.broadcasted_iota(jnp.int32, (rows, WINDOW + ds), 1)
    tq = r & (ds - 1)
    mask = (j >= tq) & (j <= tq + WINDOW)
    for bl in range(SAMPLE_BB):
        qb = q_ref[:, bl * ATT_W:(bl + 1) * ATT_W].astype(F32)
        kvb = kvn_ref[:, bl * 2 * KV_W:(bl + 1) * 2 * KV_W]
        ck = ck_ref[bl]
        cv = cv_ref[bl]
        nk_ref[bl] = jnp.concatenate([ck[ds:], kvb[:, :KV_W]], axis=0)
        nv_ref[bl] = jnp.concatenate([cv[ds:], kvb[:, KV_W:]], axis=0)
        pieces = []
        for kvh in range(N_KV_HEADS):
            lo, hi = kvh * HEAD_DIM, (kvh + 1) * HEAD_DIM
            k = jnp.concatenate([ck[:, lo:hi], kvb[:, lo:hi]], axis=0).astype(BF16)
            v = jnp.concatenate([cv[:, lo:hi], kvb[:, KV_W + lo:KV_W + hi]], axis=0).astype(BF16)
            qg = jnp.concatenate(
                [qb[:, (kvh * GROUP + g) * HEAD_DIM:(kvh * GROUP + g + 1) * HEAD_DIM] for g in range(GROUP)],
                axis=0).astype(BF16)
            s = lax.dot_general(qg, k, (((1,), (1,)), ((), ())), preferred_element_type=F32)
            p = _sink_softmax(s, mask, sink_ref[kvh][:, 0:1])
            o = jnp.dot(p.astype(BF16), v, preferred_element_type=F32)
            pieces += [o[g * ds:(g + 1) * ds] for g in range(GROUP)]
        o_ref[:, bl * ATT_W:(bl + 1) * ATT_W] = jnp.concatenate(pieces, axis=1).astype(BF16)


def _attn_sample(q, kv, cache_k, cache_v, sinks, nb, ds):
    q2 = q.reshape(ds, nb * ATT_W)
    kv2 = kv.reshape(ds, nb * 2 * KV_W)
    sink_rows = jnp.broadcast_to(
        jnp.repeat(sinks.reshape(N_KV_HEADS, GROUP), ds, axis=1)[:, :, None], (N_KV_HEADS, GROUP * ds, LANES))
    cache_spec = pl.BlockSpec((SAMPLE_BB, WINDOW, KV_W), lambda i: (i, 0, 0))
    att, nk, nv = pl.pallas_call(
        functools.partial(_attn_sample_kernel, ds),
        grid=(nb // SAMPLE_BB,),
        in_specs=[_resident(sink_rows.shape),
                  pl.BlockSpec((ds, SAMPLE_BB * ATT_W), lambda i: (0, i)),
                  pl.BlockSpec((ds, SAMPLE_BB * 2 * KV_W), lambda i: (0, i)),
                  cache_spec, cache_spec],
        out_specs=[pl.BlockSpec((ds, SAMPLE_BB * ATT_W), lambda i: (0, i)), cache_spec, cache_spec],
        out_shape=[jax.ShapeDtypeStruct((ds, nb * ATT_W), BF16),
                   jax.ShapeDtypeStruct(cache_k.shape, F32), jax.ShapeDtypeStruct(cache_v.shape, F32)],
        compiler_params=_cparams("parallel"),
        name="attn_sample",
    )(sink_rows, q2, kv2, cache_k, cache_v)
    return att.reshape(ds * nb, ATT_W), nk, nv


RG_ROWS = 256
RG_TCN = 64


def _rglru_kernel(nb, tcn, pos0, xr_ref, yr_ref, cprev_ref, hprev_ref, wc_ref, bc_ref, wg_ref, ba_ref, bi_ref,
                  lam_ref, rec_ref, hlast_ref, xbuf, a_scr, b_scr, hc):
    step = pl.program_id(0)
    rows = tcn * nb
    tail = (CONV_W - 1) * nb

    @pl.when(step == 0)
    def _():
        xbuf[0:tail, :] = cprev_ref[...]
        hc[...] = hprev_ref[...]

    @pl.when(step > 0)
    def _():
        xbuf[0:tail, :] = xbuf[rows:rows + tail, :]

    def row_pos(r0, n):
        t_row = lax.shift_right_logical(r0 + lax.broadcasted_iota(jnp.int32, (n, 1), 0), nb.bit_length() - 1)
        return pos0 + step * tcn + t_row

    sub = min(RG_ROWS, rows)
    for r0 in range(0, rows, sub):
        x = xr_ref[r0:r0 + sub, :]
        if pos0 < 0:
            x = jnp.where(row_pos(r0, sub) >= 0, x, 0.0)
        xbuf[tail + r0:tail + r0 + sub, :] = x

    lam = lam_ref[...]
    neg_c_softplus = -LRU_C * (jnp.maximum(-lam, 0.0) + jnp.log1p(jnp.exp(-jnp.abs(lam))))

    for r0 in range(0, rows, sub):
        xc = bc_ref[...]
        for jj in range(CONV_W):
            xc = xc + xbuf[r0 + jj * nb:r0 + jj * nb + sub, :] * wc_ref[jj:jj + 1, :]
        xcb = xc.astype(BF16)
        pos = row_pos(r0, sub)
        for c in range(2):
            lo, hi = c * REC_HALF, (c + 1) * REC_HALF
            gates = jnp.dot(xcb[:, lo:hi], wg_ref[c], preferred_element_type=F32)
            r = jax.nn.sigmoid(gates[:, :REC_HALF] + ba_ref[:, lo:hi])
            ig = jax.nn.sigmoid(gates[:, REC_HALF:] + bi_ref[:, lo:hi])
            a = jnp.exp(r * neg_c_softplus[:, lo:hi])
            mult = jnp.sqrt(1.0 - a * a)
            if pos0 <= 0:
                mult = jnp.where(pos == 0, 1.0, mult)
            bt = mult * ig * xc[:, lo:hi]
            if pos0 < 0:
                bt = jnp.where(pos >= 0, bt, 0.0)
            a_scr[r0:r0 + sub, lo:hi] = a
            b_scr[r0:r0 + sub, lo:hi] = bt

    if nb * REC_W <= 16 * 1024:
        def body(t, h):
            rr = pl.multiple_of(t * nb, nb)
            h = a_scr[pl.ds(rr, nb), :] * h + b_scr[pl.ds(rr, nb), :]
            b_scr[pl.ds(rr, nb), :] = h
            return h
        hc[...] = lax.fori_loop(0, tcn, body, hc[...])
    else:
        for t in range(tcn):
            h = a_scr[t * nb:(t + 1) * nb, :] * hc[...] + b_scr[t * nb:(t + 1) * nb, :]
            b_scr[t * nb:(t + 1) * nb, :] = h
            hc[...] = h

    hlast_ref[...] = hc[...]
    for r0 in range(0, rows, sub):
        rec_ref[r0:r0 + sub, :] = (b_scr[r0:r0 + sub, :] * jax.nn.gelu(yr_ref[r0:r0 + sub, :])).astype(BF16)


def _rglru(xr, yr, row_block0, n_chunks, nb, tcn, pos0, conv_prev, h_prev, wts):
    rows = tcn * nb
    tail = (CONV_W - 1) * nb
    blk = pl.BlockSpec((rows, REC_W), lambda i: (row_block0 + i, 0))
    rec, h_last = pl.pallas_call(
        functools.partial(_rglru_kernel, nb, tcn, pos0),
        grid=(n_chunks,),
        in_specs=[blk, blk, _resident(conv_prev.shape), _resident(h_prev.shape)] + [_resident(w.shape) for w in wts],
        out_specs=[pl.BlockSpec((rows, REC_W), lambda i: (i, 0)), pl.BlockSpec((nb, REC_W), lambda i: (0, 0))],
        out_shape=[jax.ShapeDtypeStruct((n_chunks * rows, REC_W), BF16), jax.ShapeDtypeStruct((nb, REC_W), F32)],
        scratch_shapes=[pltpu.VMEM((rows + tail, REC_W), F32), pltpu.VMEM((rows, REC_W), F32),
                        pltpu.VMEM((rows, REC_W), F32), pltpu.VMEM((nb, REC_W), F32)],
        compiler_params=_cparams("arbitrary"),
        name="rglru",
    )(xr, yr, conv_prev, h_prev, *wts)
    return rec, h_last


def _merge_kernel(n_p, attp_ref, atts_ref, recp_ref, recs_ref, g_ref, x_ref, wa_ref, wr_ref, wo_ref, o_ref):
    is_p = pl.program_id(0) < n_p
    att = jnp.where(is_p, attp_ref[...], atts_ref[...])
    rec = jnp.where(is_p, recp_ref[...], recs_ref[...])
    ya = jnp.dot(att, wa_ref[...], preferred_element_type=F32)
    yr = jnp.dot(rec, wr_ref[...], preferred_element_type=F32)
    merged = jax.nn.sigmoid(g_ref[:, :D_MODEL]) * ya + jax.nn.sigmoid(g_ref[:, D_MODEL:]) * yr
    o_ref[...] = x_ref[...] + jnp.dot(merged.astype(BF16), wo_ref[...], preferred_element_type=F32)


def _merge(att_p, att_s, rec_p, rec_s, g, x, wa, wr, wo):
    t = x.shape[0]
    n_p = att_p.shape[0] // TM
    n_s = att_s.shape[0] // TM
    p_spec = lambda w: pl.BlockSpec((TM, w), lambda i: (jnp.minimum(i, n_p - 1), 0))
    s_spec = lambda w: pl.BlockSpec((TM, w), lambda i: (jnp.clip(i - n_p, 0, n_s - 1), 0))
    row = lambda w: pl.BlockSpec((TM, w), lambda i: (i, 0))
    return pl.pallas_call(
        functools.partial(_merge_kernel, n_p),
        grid=(t // TM,),
        in_specs=[p_spec(ATT_W), s_spec(ATT_W), p_spec(REC_W), s_spec(REC_W), row(2 * D_MODEL), row(D_MODEL),
                  _resident(wa.shape), _resident(wr.shape), _resident(wo.shape)],
        out_specs=row(D_MODEL),
        out_shape=jax.ShapeDtypeStruct((t, D_MODEL), F32),
        compiler_params=_cparams("parallel"),
        name="merge_out",
    )(att_p, att_s, rec_p, rec_s, g, x, wa, wr, wo)


FF_CHUNK = 512


def _swiglu_mid(hn, wg_ref, wu_ref, mid_ref, width):
    for c in range(0, width, FF_CHUNK):
        gate = jnp.dot(hn, wg_ref[:, c:c + FF_CHUNK], preferred_element_type=F32)
        up = jnp.dot(hn, wu_ref[:, c:c + FF_CHUNK], preferred_element_type=F32)
        mid_ref[:, c:c + FF_CHUNK] = (jax.nn.silu(gate) * up).astype(BF16)


def _ffn_kernel(x_ref, ln_ref, wg_ref, wu_ref, wd_ref, o_ref, mid_ref):
    x = x_ref[...]
    hn = _rms(x, ln_ref[...]).astype(BF16)
    _swiglu_mid(hn, wg_ref, wu_ref, mid_ref, D_FF)
    o_ref[...] = x + jnp.dot(mid_ref[...], wd_ref[...], preferred_element_type=F32)


def _ffn(x, ln, wg, wu, wd):
    t = x.shape[0]
    row = pl.BlockSpec((TM, D_MODEL), lambda i: (i, 0))
    return pl.pallas_call(
        _ffn_kernel,
        grid=(t // TM,),
        in_specs=[row, _resident((1, D_MODEL)), _resident(wg.shape), _resident(wu.shape), _resident(wd.shape)],
        out_specs=row,
        out_shape=jax.ShapeDtypeStruct((t, D_MODEL), F32),
        scratch_shapes=[pltpu.VMEM((TM, D_FF), BF16)],
        compiler_params=_cparams("parallel"),
        name="ffn_dense",
    )(x, ln, wg, wu, wd)


TMG = 512
TOP_K = 2
DISPATCH_ROWS = 1024
INFO_E, INFO_RANK, INFO_W = 0, 2, 4


def _route_kernel(x_ref, ln_ref, wr_ref, br_ref, hn_ref, info_ref, cnt_ref, base_scr):
    @pl.when(pl.program_id(0) == 0)
    def _():
        base_scr[...] = jnp.zeros_like(base_scr)

    hn = _rms(x_ref[...], ln_ref[...])
    hn_ref[...] = hn
    lane = lax.broadcasted_iota(jnp.int32, (TM, LANES), 1).astype(F32)
    logits = jnp.dot(hn.astype(BF16), wr_ref[...], preferred_element_type=F32) + br_ref[...]
    m1 = jnp.max(logits, axis=-1, keepdims=True)
    i1 = jnp.min(jnp.where(logits == m1, lane, float(LANES)), axis=-1, keepdims=True)
    rest = jnp.where(lane == i1, -jnp.inf, logits)
    m2 = jnp.max(rest, axis=-1, keepdims=True)
    i2 = jnp.min(jnp.where(rest == m2, lane, float(LANES)), axis=-1, keepdims=True)
    e2 = jnp.exp(m2 - m1)
    den = 1.0 + e2
    hit1 = lane == i1
    hit2 = lane == i2
    onehot = jnp.where(hit1 | hit2, 1.0, 0.0)
    r = lax.broadcasted_iota(jnp.int32, (TM, TM), 0)
    c = lax.broadcasted_iota(jnp.int32, (TM, TM), 1)
    before = jnp.dot(jnp.where(c < r, 1.0, 0.0).astype(BF16), onehot.astype(BF16), preferred_element_type=F32)
    before = before + base_scr[0:1, :]
    rank1 = jnp.sum(jnp.where(hit1, before, 0.0), axis=-1, keepdims=True)
    rank2 = jnp.sum(jnp.where(hit2, before, 0.0), axis=-1, keepdims=True)
    base_scr[0:1, :] = base_scr[0:1, :] + jnp.sum(onehot, axis=0, keepdims=True)
    cnt_ref[...] = base_scr[...]
    info = jnp.zeros((TM, LANES), F32)
    for k, v in ((INFO_E, i1), (INFO_E + 1, i2), (INFO_RANK, rank1), (INFO_RANK + 1, rank2),
                 (INFO_W, 1.0 / den), (INFO_W + 1, e2 / den)):
        info = jnp.where(lane == float(k), v, info)
    info_ref[...] = info


def _route(x, ln, w_router, b_router):
    t = x.shape[0]
    row = lambda w: pl.BlockSpec((TM, w), lambda i: (i, 0))
    return pl.pallas_call(
        _route_kernel,
        grid=(t // TM,),
        in_specs=[row(D_MODEL), _resident((1, D_MODEL)), _resident(w_router.shape), _resident(b_router.shape)],
        out_specs=[row(D_MODEL), row(LANES), pl.BlockSpec((8, LANES), lambda i: (0, 0))],
        out_shape=[jax.ShapeDtypeStruct((t, D_MODEL), F32), jax.ShapeDtypeStruct((t, LANES), F32),
                   jax.ShapeDtypeStruct((8, LANES), F32)],
        scratch_shapes=[pltpu.VMEM((8, LANES), F32)],
        compiler_params=_cparams("arbitrary"),
        name="moe_route",
    )(x, ln, w_router, b_router)


def _dispatch_kernel(pos_hbm, hn_hbm, xs_in, xs_out, pos_smem, idx_sem, row_sem):
    del xs_in
    i = pl.program_id(0)
    idx_copy = pltpu.make_async_copy(pos_hbm.at[i], pos_smem, idx_sem)
    idx_copy.start()
    idx_copy.wait()

    def issue(tk, carry):
        src = hn_hbm.at[pl.ds(i * DISPATCH_ROWS + tk, 1)]
        for k in range(TOP_K):
            pltpu.make_async_copy(src, xs_out.at[pl.ds(pos_smem[TOP_K * tk + k], 1)], row_sem).start()
        return carry

    lax.fori_loop(0, DISPATCH_ROWS, issue, 0, unroll=8)
    n = TOP_K * DISPATCH_ROWS
    pltpu.make_async_copy(hn_hbm.at[pl.ds(0, n)], xs_out.at[pl.ds(0, n)], row_sem).wait()


def _dispatch(pos, hn, n_rows):
    t = hn.shape[0]
    steps = t // DISPATCH_ROWS
    xs0 = jnp.zeros((n_rows, D_MODEL), F32)
    return pl.pallas_call(
        _dispatch_kernel,
        grid=(steps,),
        in_specs=[pl.BlockSpec(memory_space=pl.ANY)] * 3,
        out_specs=pl.BlockSpec(memory_space=pl.ANY),
        out_shape=jax.ShapeDtypeStruct((n_rows, D_MODEL), F32),
        scratch_shapes=[pltpu.SMEM((TOP_K * DISPATCH_ROWS,), jnp.int32), pltpu.SemaphoreType.DMA,
                        pltpu.SemaphoreType.DMA],
        input_output_aliases={2: 0},
        compiler_params=_cparams("arbitrary"),
        name="moe_dispatch",
    )(pos.reshape(steps, TOP_K * DISPATCH_ROWS), hn, xs0)


def _expert_kernel(te_ref, nu_ref, xs_ref, wg_ref, wu_ref, wd_ref, y_ref, mid_ref):
    del te_ref

    @pl.when(pl.program_id(0) < nu_ref[0])
    def _():
        _swiglu_mid(xs_ref[...].astype(BF16), wg_ref, wu_ref, mid_ref, EXPERT_FF)
        y_ref[...] = jnp.dot(mid_ref[...], wd_ref[...], preferred_element_type=F32)

    @pl.when(pl.program_id(0) >= nu_ref[0])
    def _():
        y_ref[...] = jnp.zeros_like(y_ref)


def _experts(tile_expert, n_used, xs, wg, wu, wd):
    n_rows = xs.shape[0]
    row = pl.BlockSpec((TMG, D_MODEL), lambda i, te, nu: (i, 0))
    ex = lambda a, b: pl.BlockSpec((None, a, b), lambda i, te, nu: (te[i], 0, 0))
    return pl.pallas_call(
        _expert_kernel,
        grid_spec=pltpu.PrefetchScalarGridSpec(
            num_scalar_prefetch=2,
            grid=(n_rows // TMG,),
            in_specs=[row, ex(D_MODEL, EXPERT_FF), ex(D_MODEL, EXPERT_FF), ex(EXPERT_FF, D_MODEL)],
            out_specs=row,
            scratch_shapes=[pltpu.VMEM((TMG, EXPERT_FF), BF16)]),
        out_shape=jax.ShapeDtypeStruct((n_rows, D_MODEL), F32),
        compiler_params=_cparams("arbitrary"),
        name="moe_experts",
    )(tile_expert, n_used, xs, wg, wu, wd)


def _combine_kernel(pos_hbm, x_ref, info_ref, y_hbm, o_ref, pos_smem, buf, idx_sem, row_sem):
    i = pl.program_id(0)
    n = pl.num_programs(0)

    def gather(tile, slot):
        idx_copy = pltpu.make_async_copy(pos_hbm.at[tile], pos_smem.at[slot], idx_sem)
        idx_copy.start()
        idx_copy.wait()

        def issue(tk, carry):
            for k in range(TOP_K):
                pltpu.make_async_copy(y_hbm.at[pl.ds(pos_smem[slot, TOP_K * tk + k], 1)],
                                      buf.at[slot, k, pl.ds(tk, 1)], row_sem.at[slot]).start()
            return carry

        lax.fori_loop(0, TM, issue, 0, unroll=8)

    @pl.when(i == 0)
    def _():
        gather(0, 0)

    @pl.when(i + 1 < n)
    def _():
        gather(i + 1, (i + 1) % 2)

    slot = i % 2
    for k in range(TOP_K):
        pltpu.make_async_copy(y_hbm.at[pl.ds(0, TM)], buf.at[slot, k], row_sem.at[slot]).wait()
    lane = lax.broadcasted_iota(jnp.int32, (TM, LANES), 1)
    info = info_ref[...]
    out = x_ref[...]
    for k in range(TOP_K):
        w_k = jnp.sum(jnp.where(lane == INFO_W + k, info, 0.0), axis=-1, keepdims=True)
        out = out + w_k * buf[slot, k]
    o_ref[...] = out


def _combine(pos, x, info, y):
    t = x.shape[0]
    steps = t // TM
    row = lambda w: pl.BlockSpec((TM, w), lambda i: (i, 0))
    return pl.pallas_call(
        _combine_kernel,
        grid=(steps,),
        in_specs=[pl.BlockSpec(memory_space=pl.ANY), row(D_MODEL), row(LANES), pl.BlockSpec(memory_space=pl.ANY)],
        out_specs=row(D_MODEL),
        out_shape=jax.ShapeDtypeStruct((t, D_MODEL), F32),
        scratch_shapes=[pltpu.SMEM((2, TOP_K * TM), jnp.int32), pltpu.VMEM((2, TOP_K, TM, D_MODEL), F32),
                        pltpu.SemaphoreType.DMA, pltpu.SemaphoreType.DMA((2,))],
        compiler_params=_cparams("arbitrary"),
        name="moe_combine",
    )(pos.reshape(steps, TOP_K * TM), x, info, y)


def _moe(x, ln, w_router, b_router, wg, wu, wd):
    t = x.shape[0]
    hn, info, cnt = _route(x, ln, w_router, b_router)
    counts = cnt[0, :N_EXPERTS].astype(jnp.int32)
    tiles = (counts + TMG - 1) // TMG
    tile_end = jnp.cumsum(tiles)
    offsets = (tile_end - tiles) * TMG
    n_tiles = TOP_K * t // TMG + N_EXPERTS
    tile_expert = jnp.minimum(jnp.sum(jnp.arange(n_tiles)[:, None] >= tile_end[None, :], axis=1),
                              N_EXPERTS - 1).astype(jnp.int32)
    n_used = tile_end[N_EXPERTS - 1:].astype(jnp.int32)
    e = info[:, INFO_E:INFO_E + TOP_K].astype(jnp.int32)
    pos = (jnp.sum(jnp.where(e[..., None] == jnp.arange(N_EXPERTS), offsets, 0), axis=-1)
           + info[:, INFO_RANK:INFO_RANK + TOP_K].astype(jnp.int32)).reshape(-1)
    xs = _dispatch(pos, hn, n_tiles * TMG)
    y = _experts(tile_expert, n_used, xs, wg, wu, wd)
    return _combine(pos, x, info, y)


def _final_norm_kernel(x_ref, ln_ref, o_ref):
    o_ref[...] = _rms(x_ref[...], ln_ref[...])


def _final_norm(x, ln):
    t = x.shape[0]
    row = pl.BlockSpec((TM, D_MODEL), lambda i: (i, 0))
    return pl.pallas_call(
        _final_norm_kernel,
        grid=(t // TM,),
        in_specs=[row, _resident((1, D_MODEL))],
        out_specs=row,
        out_shape=jax.ShapeDtypeStruct((t, D_MODEL), F32),
        compiler_params=_cparams("parallel"),
        name="final_norm",
    )(x, ln)


def _gate_weights(w_a, w_i):
    half_blocks = REC_BLOCKS // 2
    eye = jnp.eye(half_blocks, dtype=F32)

    def dense(w):
        return (eye[:, None, :, None] * w[:, :, None, :]).reshape(REC_HALF, REC_HALF)

    halves = [jnp.concatenate([dense(w_a[c * half_blocks:(c + 1) * half_blocks]),
                               dense(w_i[c * half_blocks:(c + 1) * half_blocks])], axis=1) for c in range(2)]
    return jnp.stack(halves).astype(BF16)


def kernel(x_prompt, x_sample, cache_k, cache_v, state_conv, state_rglru, meta_tokens, ln_mix, w_in, attn_sinks,
           w_conv, b_conv, w_rg_a, b_rg_a, w_rg_i, b_rg_i, rg_lambda, w_att_proj, w_rec_proj, w_out, ln_ffn,
           w_ffn_gate, w_ffn_up, w_ffn_down, w_router, b_router, w_exp_gate, w_exp_up, w_exp_down, ln_final):
    bp, seq, _ = x_prompt.shape
    db, ds, _ = x_sample.shape
    depth = w_in.shape[0]
    pad = (-(N_META + seq)) % BLOCK
    lp = pad + N_META + seq
    tp, ts = lp * bp, ds * db
    assert tp % TM == 0 and ts % TM == 0 and db % SAMPLE_BB == 0 and ds >= CONV_W - 1
    assert seq >= WINDOW and bp & (bp - 1) == 0 and db & (db - 1) == 0 and ds & (ds - 1) == 0
    assert lp % RG_TCN == 0 and tp % ts == 0

    meta = jnp.broadcast_to(meta_tokens[:, None, :], (N_META, bp, D_MODEL))
    xp = jnp.concatenate([jnp.zeros((pad, bp, D_MODEL), F32), meta, jnp.transpose(x_prompt, (1, 0, 2))], axis=0)
    x = jnp.concatenate([xp.reshape(tp, D_MODEL), jnp.transpose(x_sample, (1, 0, 2)).reshape(ts, D_MODEL)], axis=0)

    zeros_conv = jnp.zeros(((CONV_W - 1) * bp, REC_W), F32)
    zeros_h = jnp.zeros((bp, REC_W), F32)
    outs = {k: [] for k in ("kp", "vp", "cp", "hp", "ks", "vs", "cs", "hs")}

    for l in range(depth):
        q, kv, xr, yr, g = _in_proj(x, ln_mix[l].reshape(1, D_MODEL), w_in[l].astype(BF16))

        att_p = _attn_prompt(q, kv, attn_sinks[l], bp, lp, pad)
        att_s, nk, nv = _attn_sample(q[tp:], kv[tp:], cache_k[l].reshape(db, WINDOW, KV_W),
                                     cache_v[l].reshape(db, WINDOW, KV_W), attn_sinks[l], db, ds)

        rg_w = (w_conv[l], b_conv[l].reshape(1, REC_W), _gate_weights(w_rg_a[l], w_rg_i[l]),
                b_rg_a[l].reshape(1, REC_W), b_rg_i[l].reshape(1, REC_W), rg_lambda[l].reshape(1, REC_W))
        rec_p, h_p = _rglru(xr, yr, 0, lp // RG_TCN, bp, RG_TCN, -pad, zeros_conv, zeros_h, rg_w)
        conv_s = jnp.transpose(state_conv[l], (1, 0, 2)).reshape((CONV_W - 1) * db, REC_W)
        rec_s, h_s = _rglru(xr, yr, tp // ts, 1, db, ds, PAST_LEN, conv_s, state_rglru[l], rg_w)

        x = _merge(att_p, att_s, rec_p, rec_s, g, x, w_att_proj[l].astype(BF16), w_rec_proj[l].astype(BF16),
                   w_out[l].astype(BF16))

        j = l // 2
        ln_f = ln_ffn[l].reshape(1, D_MODEL)
        if l % 2 == 0:
            x = _ffn(x, ln_f, w_ffn_gate[j].astype(BF16), w_ffn_up[j].astype(BF16), w_ffn_down[j].astype(BF16))
        else:
            w_r = jnp.zeros((D_MODEL, LANES), F32).at[:, :N_EXPERTS].set(w_router[j]).astype(BF16)
            b_r = jnp.full((1, LANES), ROUTER_PAD_BIAS, F32).at[0, :N_EXPERTS].set(b_router[j])
            x = _moe(x, ln_f, w_r, b_r, w_exp_gate[j].astype(BF16), w_exp_up[j].astype(BF16),
                     w_exp_down[j].astype(BF16))

        kv_p = kv[:tp].reshape(lp, bp, 2 * KV_W)[lp - WINDOW:]
        outs["kp"].append(jnp.transpose(kv_p[..., :KV_W], (1, 0, 2)).reshape(bp, WINDOW, N_KV_HEADS, HEAD_DIM))
        outs["vp"].append(jnp.transpose(kv_p[..., KV_W:], (1, 0, 2)).reshape(bp, WINDOW, N_KV_HEADS, HEAD_DIM))
        outs["cp"].append(jnp.transpose(xr[:tp].reshape(lp, bp, REC_W)[lp - (CONV_W - 1):], (1, 0, 2)))
        outs["hp"].append(h_p)
        outs["ks"].append(nk.reshape(db, WINDOW, N_KV_HEADS, HEAD_DIM))
        outs["vs"].append(nv.reshape(db, WINDOW, N_KV_HEADS, HEAD_DIM))
        outs["cs"].append(jnp.transpose(xr[tp:].reshape(ds, db, REC_W)[ds - (CONV_W - 1):], (1, 0, 2)))
        outs["hs"].append(h_s)

    y = _final_norm(x, ln_final.reshape(1, D_MODEL))
    y_prompt = jnp.transpose(y[:tp].reshape(lp, bp, D_MODEL)[pad + N_META:], (1, 0, 2))
    y_sample = jnp.transpose(y[tp:].reshape(ds, db, D_MODEL), (1, 0, 2))
    st = {k: jnp.stack(v) for k, v in outs.items()}
    return (y_prompt, y_sample, st["kp"], st["vp"], st["cp"], st["hp"], st["ks"], st["vs"], st["cs"], st["hs"])
```

```python
import functools

import jax
import jax.numpy as jnp
from jax import lax
from jax.experimental import pallas as pl
from jax.experimental.pallas import tpu as pltpu

F32 = jnp.float32
BF16 = jnp.bfloat16

D_MODEL = 1024
N_META = 16
N_HEADS = 16
N_KV_HEADS = 2
HEAD_DIM = D_MODEL // N_HEADS
GROUP = N_HEADS // N_KV_HEADS
ATT_W = N_HEADS * HEAD_DIM
KV_W = N_KV_HEADS * HEAD_DIM
WINDOW = 128
BLOCK = 128
ATT_SCALE = HEAD_DIM ** -0.5
REC_W = 5 * D_MODEL // 4
REC_BLOCKS = 16
REC_BW = REC_W // REC_BLOCKS
REC_HALF = REC_W // 2
CONV_W = 4
LRU_C = 8.0
D_FF = 3 * D_MODEL
N_EXPERTS = 8
EXPERT_FF = 3 * D_MODEL // 2
PAST_LEN = 8192
RMS_EPS = 1e-6
IN_OFFS = (0, ATT_W, ATT_W + 2 * KV_W, ATT_W + 2 * KV_W + REC_W, ATT_W + 2 * KV_W + 2 * REC_W,
           ATT_W + 2 * KV_W + 2 * REC_W + 2 * D_MODEL)
LANES = 128
VMEM_LIMIT = 52 * 1024 * 1024

TM = 512
ROUTER_PAD_BIAS = -1e30


def _cparams(*sem):
    return pltpu.CompilerParams(dimension_semantics=sem, vmem_limit_bytes=VMEM_LIMIT)


def _resident(shape):
    nd = len(shape)
    return pl.BlockSpec(shape, lambda *_: (0,) * nd, pipeline_mode=pl.Buffered(1))


def _rms(x, g):
    return x * lax.rsqrt(jnp.mean(x * x, axis=-1, keepdims=True) + RMS_EPS) * g


def _in_proj_kernel(x_ref, ln_ref, w_ref, q_ref, kv_ref, xr_ref, yr_ref, g_ref):
    hn = _rms(x_ref[...], ln_ref[...]).astype(BF16)

    def proj(a, b):
        return jnp.dot(hn, w_ref[:, a:b], preferred_element_type=F32)

    q_ref[...] = (proj(IN_OFFS[0], IN_OFFS[1]) * ATT_SCALE).astype(BF16)
    kv_ref[...] = proj(IN_OFFS[1], IN_OFFS[2])
    xr_ref[...] = proj(IN_OFFS[2], IN_OFFS[3])
    yr_ref[...] = proj(IN_OFFS[3], IN_OFFS[4])
    g_ref[...] = proj(IN_OFFS[4], IN_OFFS[5])


def _in_proj(x, ln, w_in):
    t = x.shape[0]
    row = lambda w: pl.BlockSpec((TM, w), lambda i: (i, 0))
    return pl.pallas_call(
        _in_proj_kernel,
        grid=(t // TM,),
        in_specs=[row(D_MODEL), _resident((1, D_MODEL)), _resident(w_in.shape)],
        out_specs=[row(ATT_W), row(2 * KV_W), row(REC_W), row(REC_W), row(2 * D_MODEL)],
        out_shape=[jax.ShapeDtypeStruct((t, ATT_W), BF16), jax.ShapeDtypeStruct((t, 2 * KV_W), F32),
                   jax.ShapeDtypeStruct((t, REC_W), F32), jax.ShapeDtypeStruct((t, REC_W), F32),
                   jax.ShapeDtypeStruct((t, 2 * D_MODEL), F32)],
        compiler_params=_cparams("parallel"),
        name="in_proj",
    )(x, ln, w_in)


def _sink_softmax(s, mask, sk):
    s = jnp.where(mask, s, -jnp.inf)
    m = jnp.maximum(jnp.max(s, axis=-1, keepdims=True), sk)
    p = jnp.exp(s - m)
    return p / (jnp.sum(p, axis=-1, keepdims=True) + jnp.exp(sk - m))


def _attn_prompt_kernel(pad, sink_ref, q_ref, kvp_ref, kvc_ref, o_ref):
    n = pl.program_id(1)
    q = q_ref[...]
    kv = jnp.concatenate([kvp_ref[...], kvc_ref[...]], axis=0).astype(BF16)
    i = lax.broadcasted_iota(jnp.int32, (BLOCK, 2 * BLOCK), 0)
    j = lax.broadcasted_iota(jnp.int32, (BLOCK, 2 * BLOCK), 1)
    mask = (j >= i) & (j <= i + WINDOW) & (j >= pad + BLOCK - n * BLOCK)
    outs = []
    for h in range(N_HEADS):
        kvh = h // GROUP
        qh = q[:, h * HEAD_DIM:(h + 1) * HEAD_DIM]
        k = kv[:, kvh * HEAD_DIM:(kvh + 1) * HEAD_DIM]
        v = kv[:, KV_W + kvh * HEAD_DIM:KV_W + (kvh + 1) * HEAD_DIM]
        s = lax.dot_general(qh, k, (((1,), (1,)), ((), ())), preferred_element_type=F32)
        p = _sink_softmax(s, mask, sink_ref[h])
        outs.append(jnp.dot(p.astype(BF16), v, preferred_element_type=F32))
    o_ref[...] = jnp.concatenate(outs, axis=1).astype(BF16)


def _attn_prompt(q, kv, sinks, nb, lp, pad):
    t = q.shape[0]
    q2 = q.reshape(t // nb, nb * ATT_W)
    kv2 = kv.reshape(t // nb, nb * 2 * KV_W)
    out = pl.pallas_call(
        functools.partial(_attn_prompt_kernel, pad),
        grid=(nb, lp // BLOCK),
        in_specs=[pl.BlockSpec(memory_space=pltpu.SMEM),
                  pl.BlockSpec((BLOCK, ATT_W), lambda b, n: (n, b)),
                  pl.BlockSpec((BLOCK, 2 * KV_W), lambda b, n: (jnp.maximum(n - 1, 0), b)),
                  pl.BlockSpec((BLOCK, 2 * KV_W), lambda b, n: (n, b))],
        out_specs=pl.BlockSpec((BLOCK, ATT_W), lambda b, n: (n, b)),
        out_shape=jax.ShapeDtypeStruct((lp, nb * ATT_W), BF16),
        compiler_params=_cparams("parallel", "parallel"),
        name="attn_prompt",
    )(sinks, q2, kv2, kv2)
    return out.reshape(lp * nb, ATT_W)


SAMPLE_BB = 8


def _attn_sample_kernel(ds, sink_ref, q_ref, kvn_ref, ck_ref, cv_ref, o_ref, nk_ref, nv_ref):
    rows = GROUP * ds
    r = lax.broadcasted_iota(jnp.int32, (rows, WINDOW + ds), 0)
    j = lax.broadcasted_iota(jnp.int32, (rows, WINDOW + ds), 1)
    tq = r & (ds - 1)
    mask = (j >= tq) & (j <= tq + WINDOW)
    for bl in range(SAMPLE_BB):
        qb = q_ref[:, bl * ATT_W:(bl + 1) * ATT_W].astype(F32)
        kvb = kvn_ref[:, bl * 2 * KV_W:(bl + 1) * 2 * KV_W]
        ck = ck_ref[bl]
        cv = cv_ref[bl]
        nk_ref[bl] = jnp.concatenate([ck[ds:], kvb[:, :KV_W]], axis=0)
        nv_ref[bl] = jnp.concatenate([cv[ds:], kvb[:, KV_W:]], axis=0)
        pieces = []
        for kvh in range(N_KV_HEADS):
            lo, hi = kvh * HEAD_DIM, (kvh + 1) * HEAD_DIM
            k = jnp.concatenate([ck[:, lo:hi], kvb[:, lo:hi]], axis=0).astype(BF16)
            v = jnp.concatenate([cv[:, lo:hi], kvb[:, KV_W + lo:KV_W + hi]], axis=0).astype(BF16)
            qg = jnp.concatenate(
                [qb[:, (kvh * GROUP + g) * HEAD_DIM:(kvh * GROUP + g + 1) * HEAD_DIM] for g in range(GROUP)],
                axis=0).astype(BF16)
            s = lax.dot_general(qg, k, (((1,), (1,)), ((), ())), preferred_element_type=F32)
            p = _sink_softmax(s, mask, sink_ref[kvh][:, 0:1])
            o = jnp.dot(p.astype(BF16), v, preferred_element_type=F32)
            pieces += [o[g * ds:(g + 1) * ds] for g in range(GROUP)]
        o_ref[:, bl * ATT_W:(bl + 1) * ATT_W] = jnp.concatenate(pieces, axis=1).astype(BF16)


def _attn_sample(q, kv, cache_k, cache_v, sinks, nb, ds):
    q2 = q.reshape(ds, nb * ATT_W)
    kv2 = kv.reshape(ds, nb * 2 * KV_W)
    sink_rows = jnp.broadcast_to(
        jnp.repeat(sinks.reshape(N_KV_HEADS, GROUP), ds, axis=1)[:, :, None], (N_KV_HEADS, GROUP * ds, LANES))
    cache_spec = pl.BlockSpec((SAMPLE_BB, WINDOW, KV_W), lambda i: (i, 0, 0))
    att, nk, nv = pl.pallas_call(
        functools.partial(_attn_sample_kernel, ds),
        grid=(nb // SAMPLE_BB,),
        in_specs=[_resident(sink_rows.shape),
                  pl.BlockSpec((ds, SAMPLE_BB * ATT_W), lambda i: (0, i)),
                  pl.BlockSpec((ds, SAMPLE_BB * 2 * KV_W), lambda i: (0, i)),
                  cache_spec, cache_spec],
        out_specs=[pl.BlockSpec((ds, SAMPLE_BB * ATT_W), lambda i: (0, i)), cache_spec, cache_spec],
        out_shape=[jax.ShapeDtypeStruct((ds, nb * ATT_W), BF16),
                   jax.ShapeDtypeStruct(cache_k.shape, F32), jax.ShapeDtypeStruct(cache_v.shape, F32)],
        compiler_params=_cparams("parallel"),
        name="attn_sample",
    )(sink_rows, q2, kv2, cache_k, cache_v)
    return att.reshape(ds * nb, ATT_W), nk, nv


RG_ROWS = 256
RG_TCN = 64


def _rglru_kernel(nb, tcn, pos0, xr_ref, yr_ref, cprev_ref, hprev_ref, wc_ref, bc_ref, wg_ref, ba_ref, bi_ref,
                  lam_ref, rec_ref, hlast_ref, xbuf, a_scr, b_scr, hc):
    step = pl.program_id(0)
    rows = tcn * nb
    tail = (CONV_W - 1) * nb

    @pl.when(step == 0)
    def _():
        xbuf[0:tail, :] = cprev_ref[...]
        hc[...] = hprev_ref[...]

    @pl.when(step > 0)
    def _():
        xbuf[0:tail, :] = xbuf[rows:rows + tail, :]

    def row_pos(r0, n):
        t_row = lax.shift_right_logical(r0 + lax.broadcasted_iota(jnp.int32, (n, 1), 0), nb.bit_length() - 1)
        return pos0 + step * tcn + t_row

    sub = min(RG_ROWS, rows)
    for r0 in range(0, rows, sub):
        x = xr_ref[r0:r0 + sub, :]
        if pos0 < 0:
            x = jnp.where(row_pos(r0, sub) >= 0, x, 0.0)
        xbuf[tail + r0:tail + r0 + sub, :] = x

    lam = lam_ref[...]
    neg_c_softplus = -LRU_C * (jnp.maximum(-lam, 0.0) + jnp.log1p(jnp.exp(-jnp.abs(lam))))

    for r0 in range(0, rows, sub):
        xc = bc_ref[...]
        for jj in range(CONV_W):
            xc = xc + xbuf[r0 + jj * nb:r0 + jj * nb + sub, :] * wc_ref[jj:jj + 1, :]
        xcb = xc.astype(BF16)
        pos = row_pos(r0, sub)
        for c in range(2):
            lo, hi = c * REC_HALF, (c + 1) * REC_HALF
            gates = jnp.dot(xcb[:, lo:hi], wg_ref[c], preferred_element_type=F32)
            r = jax.nn.sigmoid(gates[:, :REC_HALF] + ba_ref[:, lo:hi])
            ig = jax.nn.sigmoid(gates[:, REC_HALF:] + bi_ref[:, lo:hi])
            a = jnp.exp(r * neg_c_softplus[:, lo:hi])
            mult = jnp.sqrt(1.0 - a * a)
            if pos0 <= 0:
                mult = jnp.where(pos == 0, 1.0, mult)
            bt = mult * ig * xc[:, lo:hi]
            if pos0 < 0:
                bt = jnp.where(pos >= 0, bt, 0.0)
            a_scr[r0:r0 + sub, lo:hi] = a
            b_scr[r0:r0 + sub, lo:hi] = bt

    if nb * REC_W <= 16 * 1024:
        def body(t, h):
            rr = pl.multiple_of(t * nb, nb)
            h = a_scr[pl.ds(rr, nb), :] * h + b_scr[pl.ds(rr, nb), :]
            b_scr[pl.ds(rr, nb), :] = h
            return h
        hc[...] = lax.fori_loop(0, tcn, body, hc[...])
    else:
        for t in range(tcn):
            h = a_scr[t * nb:(t + 1) * nb, :] * hc[...] + b_scr[t * nb:(t + 1) * nb, :]
            b_scr[t * nb:(t + 1) * nb, :] = h
            hc[...] = h

    hlast_ref[...] = hc[...]
    for r0 in range(0, rows, sub):
        rec_ref[r0:r0 + sub, :] = (b_scr[r0:r0 + sub, :] * jax.nn.gelu(yr_ref[r0:r0 + sub, :])).astype(BF16)


def _rglru(xr, yr, row_block0, n_chunks, nb, tcn, pos0, conv_prev, h_prev, wts):
    rows = tcn * nb
    tail = (CONV_W - 1) * nb
    blk = pl.BlockSpec((rows, REC_W), lambda i: (row_block0 + i, 0))
    rec, h_last = pl.pallas_call(
        functools.partial(_rglru_kernel, nb, tcn, pos0),
        grid=(n_chunks,),
        in_specs=[blk, blk, _resident(conv_prev.shape), _resident(h_prev.shape)] + [_resident(w.shape) for w in wts],
        out_specs=[pl.BlockSpec((rows, REC_W), lambda i: (i, 0)), pl.BlockSpec((nb, REC_W), lambda i: (0, 0))],
        out_shape=[jax.ShapeDtypeStruct((n_chunks * rows, REC_W), BF16), jax.ShapeDtypeStruct((nb, REC_W), F32)],
        scratch_shapes=[pltpu.VMEM((rows + tail, REC_W), F32), pltpu.VMEM((rows, REC_W), F32),
                        pltpu.VMEM((rows, REC_W), F32), pltpu.VMEM((nb, REC_W), F32)],
        compiler_params=_cparams("arbitrary"),
        name="rglru",
    )(xr, yr, conv_prev, h_prev, *wts)
    return rec, h_last


def _merge_kernel(n_p, attp_ref, atts_ref, recp_ref, recs_ref, g_ref, x_ref, wa_ref, wr_ref, wo_ref, o_ref):
    is_p = pl.program_id(0) < n_p
    att = jnp.where(is_p, attp_ref[...], atts_ref[...])
    rec = jnp.where(is_p, recp_ref[...], recs_ref[...])
    ya = jnp.dot(att, wa_ref[...], preferred_element_type=F32)
    yr = jnp.dot(rec, wr_ref[...], preferred_element_type=F32)
    merged = jax.nn.sigmoid(g_ref[:, :D_MODEL]) * ya + jax.nn.sigmoid(g_ref[:, D_MODEL:]) * yr
    o_ref[...] = x_ref[...] + jnp.dot(merged.astype(BF16), wo_ref[...], preferred_element_type=F32)


def _merge(att_p, att_s, rec_p, rec_s, g, x, wa, wr, wo):
    t = x.shape[0]
    n_p = att_p.shape[0] // TM
    n_s = att_s.shape[0] // TM
    p_spec = lambda w: pl.BlockSpec((TM, w), lambda i: (jnp.minimum(i, n_p - 1), 0))
    s_spec = lambda w: pl.BlockSpec((TM, w), lambda i: (jnp.clip(i - n_p, 0, n_s - 1), 0))
    row = lambda w: pl.BlockSpec((TM, w), lambda i: (i, 0))
    return pl.pallas_call(
        functools.partial(_merge_kernel, n_p),
        grid=(t // TM,),
        in_specs=[p_spec(ATT_W), s_spec(ATT_W), p_spec(REC_W), s_spec(REC_W), row(2 * D_MODEL), row(D_MODEL),
                  _resident(wa.shape), _resident(wr.shape), _resident(wo.shape)],
        out_specs=row(D_MODEL),
        out_shape=jax.ShapeDtypeStruct((t, D_MODEL), F32),
        compiler_params=_cparams("parallel"),
        name="merge_out",
    )(att_p, att_s, rec_p, rec_s, g, x, wa, wr, wo)


FF_CHUNK = 512


def _swiglu_mid(hn, wg_ref, wu_ref, mid_ref, width):
    for c in range(0, width, FF_CHUNK):
        gate = jnp.dot(hn, wg_ref[:, c:c + FF_CHUNK], preferred_element_type=F32)
        up = jnp.dot(hn, wu_ref[:, c:c + FF_CHUNK], preferred_element_type=F32)
        mid_ref[:, c:c + FF_CHUNK] = (jax.nn.silu(gate) * up).astype(BF16)


def _ffn_kernel(x_ref, ln_ref, wg_ref, wu_ref, wd_ref, o_ref, mid_ref):
    x = x_ref[...]
    hn = _rms(x, ln_ref[...]).astype(BF16)
    _swiglu_mid(hn, wg_ref, wu_ref, mid_ref, D_FF)
    o_ref[...] = x + jnp.dot(mid_ref[...], wd_ref[...], preferred_element_type=F32)


def _ffn(x, ln, wg, wu, wd):
    t = x.shape[0]
    row = pl.BlockSpec((TM, D_MODEL), lambda i: (i, 0))
    return pl.pallas_call(
        _ffn_kernel,
        grid=(t // TM,),
        in_specs=[row, _resident((1, D_MODEL)), _resident(wg.shape), _resident(wu.shape), _resident(wd.shape)],
        out_specs=row,
        out_shape=jax.ShapeDtypeStruct((t, D_MODEL), F32),
        scratch_shapes=[pltpu.VMEM((TM, D_FF), BF16)],
        compiler_params=_cparams("parallel"),
        name="ffn_dense",
    )(x, ln, wg, wu, wd)


TMG = 512
TOP_K = 2
DISPATCH_ROWS = 512
INFO_E, INFO_RANK, INFO_W = 0, 2, 4


def _route_kernel(x_ref, ln_ref, wr_ref, br_ref, hn_ref, info_ref, cnt_ref, base_scr):
    @pl.when(pl.program_id(0) == 0)
    def _():
        base_scr[...] = jnp.zeros_like(base_scr)

    hn = _rms(x_ref[...], ln_ref[...])
    hn_ref[...] = hn
    lane = lax.broadcasted_iota(jnp.int32, (TM, LANES), 1).astype(F32)
    logits = jnp.dot(hn.astype(BF16), wr_ref[...], preferred_element_type=F32) + br_ref[...]
    m1 = jnp.max(logits, axis=-1, keepdims=True)
    i1 = jnp.min(jnp.where(logits == m1, lane, float(LANES)), axis=-1, keepdims=True)
    rest = jnp.where(lane == i1, -jnp.inf, logits)
    m2 = jnp.max(rest, axis=-1, keepdims=True)
    i2 = jnp.min(jnp.where(rest == m2, lane, float(LANES)), axis=-1, keepdims=True)
    e2 = jnp.exp(m2 - m1)
    den = 1.0 + e2
    hit1 = lane == i1
    hit2 = lane == i2
    onehot = jnp.where(hit1 | hit2, 1.0, 0.0)
    r = lax.broadcasted_iota(jnp.int32, (TM, TM), 0)
    c = lax.broadcasted_iota(jnp.int32, (TM, TM), 1)
    before = jnp.dot(jnp.where(c < r, 1.0, 0.0).astype(BF16), onehot.astype(BF16), preferred_element_type=F32)
    before = before + base_scr[0:1, :]
    rank1 = jnp.sum(jnp.where(hit1, before, 0.0), axis=-1, keepdims=True)
    rank2 = jnp.sum(jnp.where(hit2, before, 0.0), axis=-1, keepdims=True)
    base_scr[0:1, :] = base_scr[0:1, :] + jnp.sum(onehot, axis=0, keepdims=True)
    cnt_ref[...] = base_scr[...]
    info = jnp.zeros((TM, LANES), F32)
    for k, v in ((INFO_E, i1), (INFO_E + 1, i2), (INFO_RANK, rank1), (INFO_RANK + 1, rank2),
                 (INFO_W, 1.0 / den), (INFO_W + 1, e2 / den)):
        info = jnp.where(lane == float(k), v, info)
    info_ref[...] = info


def _route(x, ln, w_router, b_router):
    t = x.shape[0]
    row = lambda w: pl.BlockSpec((TM, w), lambda i: (i, 0))
    return pl.pallas_call(
        _route_kernel,
        grid=(t // TM,),
        in_specs=[row(D_MODEL), _resident((1, D_MODEL)), _resident(w_router.shape), _resident(b_router.shape)],
        out_specs=[row(D_MODEL), row(LANES), pl.BlockSpec((8, LANES), lambda i: (0, 0))],
        out_shape=[jax.ShapeDtypeStruct((t, D_MODEL), F32), jax.ShapeDtypeStruct((t, LANES), F32),
                   jax.ShapeDtypeStruct((8, LANES), F32)],
        scratch_shapes=[pltpu.VMEM((8, LANES), F32)],
        compiler_params=_cparams("arbitrary"),
        name="moe_route",
    )(x, ln, w_router, b_router)


def _dispatch_kernel(pos_hbm, hn_ref, xs_in, xs_out, pos_smem, idx_sem, row_sem):
    del xs_in
    i = pl.program_id(0)
    idx_copy = pltpu.make_async_copy(pos_hbm.at[i], pos_smem, idx_sem)
    idx_copy.start()
    idx_copy.wait()

    def issue(tk, carry):
        for k in range(TOP_K):
            pltpu.make_async_copy(hn_ref.at[pl.ds(tk, 1)], xs_out.at[pl.ds(pos_smem[TOP_K * tk + k], 1)],
                                  row_sem).start()
        return carry

    lax.fori_loop(0, DISPATCH_ROWS, issue, 0, unroll=8)
    for k in range(TOP_K):
        pltpu.make_async_copy(hn_ref, xs_out.at[pl.ds(0, DISPATCH_ROWS)], row_sem).wait()


def _dispatch(pos, hn, n_rows):
    t = hn.shape[0]
    steps = t // DISPATCH_ROWS
    xs0 = jnp.zeros((n_rows, D_MODEL), F32)
    return pl.pallas_call(
        _dispatch_kernel,
        grid=(steps,),
        in_specs=[pl.BlockSpec(memory_space=pl.ANY), pl.BlockSpec((DISPATCH_ROWS, D_MODEL), lambda i: (i, 0)),
                  pl.BlockSpec(memory_space=pl.ANY)],
        out_specs=pl.BlockSpec(memory_space=pl.ANY),
        out_shape=jax.ShapeDtypeStruct((n_rows, D_MODEL), F32),
        scratch_shapes=[pltpu.SMEM((TOP_K * DISPATCH_ROWS,), jnp.int32), pltpu.SemaphoreType.DMA,
                        pltpu.SemaphoreType.DMA],
        input_output_aliases={2: 0},
        compiler_params=_cparams("arbitrary"),
        name="moe_dispatch",
    )(pos.reshape(steps, TOP_K * DISPATCH_ROWS), hn, xs0)


def _expert_kernel(te_ref, nu_ref, xs_ref, wg_ref, wu_ref, wd_ref, y_ref, mid_ref):
    del te_ref

    @pl.when(pl.program_id(0) < nu_ref[0])
    def _():
        _swiglu_mid(xs_ref[...].astype(BF16), wg_ref, wu_ref, mid_ref, EXPERT_FF)
        y_ref[...] = jnp.dot(mid_ref[...], wd_ref[...], preferred_element_type=F32)

    @pl.when(pl.program_id(0) >= nu_ref[0])
    def _():
        y_ref[...] = jnp.zeros_like(y_ref)


def _experts(tile_expert, n_used, xs, wg, wu, wd):
    n_rows = xs.shape[0]
    row = pl.BlockSpec((TMG, D_MODEL), lambda i, te, nu: (i, 0))
    ex = lambda a, b: pl.BlockSpec((None, a, b), lambda i, te, nu: (te[i], 0, 0))
    return pl.pallas_call(
        _expert_kernel,
        grid_spec=pltpu.PrefetchScalarGridSpec(
            num_scalar_prefetch=2,
            grid=(n_rows // TMG,),
            in_specs=[row, ex(D_MODEL, EXPERT_FF), ex(D_MODEL, EXPERT_FF), ex(EXPERT_FF, D_MODEL)],
            out_specs=row,
            scratch_shapes=[pltpu.VMEM((TMG, EXPERT_FF), BF16)]),
        out_shape=jax.ShapeDtypeStruct((n_rows, D_MODEL), F32),
        compiler_params=_cparams("arbitrary"),
        name="moe_experts",
    )(tile_expert, n_used, xs, wg, wu, wd)


def _combine_kernel(pos_hbm, x_ref, info_ref, y_hbm, o_ref, pos_smem, buf, idx_sem, row_sem):
    i = pl.program_id(0)
    n = pl.num_programs(0)

    def gather(tile, slot):
        idx_copy = pltpu.make_async_copy(pos_hbm.at[tile], pos_smem.at[slot], idx_sem)
        idx_copy.start()
        idx_copy.wait()

        def issue(tk, carry):
            for k in range(TOP_K):
                pltpu.make_async_copy(y_hbm.at[pl.ds(pos_smem[slot, TOP_K * tk + k], 1)],
                                      buf.at[slot, k, pl.ds(tk, 1)], row_sem.at[slot]).start()
            return carry

        lax.fori_loop(0, TM, issue, 0, unroll=8)

    @pl.when(i == 0)
    def _():
        gather(0, 0)

    @pl.when(i + 1 < n)
    def _():
        gather(i + 1, (i + 1) % 2)

    slot = i % 2
    for k in range(TOP_K):
        pltpu.make_async_copy(y_hbm.at[pl.ds(0, TM)], buf.at[slot, k], row_sem.at[slot]).wait()
    lane = lax.broadcasted_iota(jnp.int32, (TM, LANES), 1)
    info = info_ref[...]
    out = x_ref[...]
    for k in range(TOP_K):
        w_k = jnp.sum(jnp.where(lane == INFO_W + k, info, 0.0), axis=-1, keepdims=True)
        out = out + w_k * buf[slot, k]
    o_ref[...] = out


def _combine(pos, x, info, y):
    t = x.shape[0]
    steps = t // TM
    row = lambda w: pl.BlockSpec((TM, w), lambda i: (i, 0))
    return pl.pallas_call(
        _combine_kernel,
        grid=(steps,),
        in_specs=[pl.BlockSpec(memory_space=pl.ANY), row(D_MODEL), row(LANES), pl.BlockSpec(memory_space=pl.ANY)],
        out_specs=row(D_MODEL),
        out_shape=jax.ShapeDtypeStruct((t, D_MODEL), F32),
        scratch_shapes=[pltpu.SMEM((2, TOP_K * TM), jnp.int32), pltpu.VMEM((2, TOP_K, TM, D_MODEL), F32),
                        pltpu.SemaphoreType.DMA, pltpu.SemaphoreType.DMA((2,))],
        compiler_params=_cparams("arbitrary"),
        name="moe_combine",
    )(pos.reshape(steps, TOP_K * TM), x, info, y)


def _moe(x, ln, w_router, b_router, wg, wu, wd):
    t = x.shape[0]
    hn, info, cnt = _route(x, ln, w_router, b_router)
    counts = cnt[0, :N_EXPERTS].astype(jnp.int32)
    tiles = (counts + TMG - 1) // TMG
    tile_end = jnp.cumsum(tiles)
    offsets = (tile_end - tiles) * TMG
    n_tiles = TOP_K * t // TMG + N_EXPERTS
    tile_expert = jnp.minimum(jnp.sum(jnp.arange(n_tiles)[:, None] >= tile_end[None, :], axis=1),
                              N_EXPERTS - 1).astype(jnp.int32)
    n_used = tile_end[N_EXPERTS - 1:].astype(jnp.int32)
    e = info[:, INFO_E:INFO_E + TOP_K].astype(jnp.int32)
    pos = (jnp.sum(jnp.where(e[..., None] == jnp.arange(N_EXPERTS), offsets, 0), axis=-1)
           + info[:, INFO_RANK:INFO_RANK + TOP_K].astype(jnp.int32)).reshape(-1)
    xs = _dispatch(pos, hn, n_tiles * TMG)
    y = _experts(tile_expert, n_used, xs, wg, wu, wd)
    return _combine(pos, x, info, y)


def _final_norm_kernel(x_ref, ln_ref, o_ref):
    o_ref[...] = _rms(x_ref[...], ln_ref[...])


def _final_norm(x, ln):
    t = x.shape[0]
    row = pl.BlockSpec((TM, D_MODEL), lambda i: (i, 0))
    return pl.pallas_call(
        _final_norm_kernel,
        grid=(t // TM,),
        in_specs=[row, _resident((1, D_MODEL))],
        out_specs=row,
        out_shape=jax.ShapeDtypeStruct((t, D_MODEL), F32),
        compiler_params=_cparams("parallel"),
        name="final_norm",
    )(x, ln)


def _gate_weights(w_a, w_i):
    half_blocks = REC_BLOCKS // 2
    eye = jnp.eye(half_blocks, dtype=F32)

    def dense(w):
        return (eye[:, None, :, None] * w[:, :, None, :]).reshape(REC_HALF, REC_HALF)

    halves = [jnp.concatenate([dense(w_a[c * half_blocks:(c + 1) * half_blocks]),
                               dense(w_i[c * half_blocks:(c + 1) * half_blocks])], axis=1) for c in range(2)]
    return jnp.stack(halves).astype(BF16)


def kernel(x_prompt, x_sample, cache_k, cache_v, state_conv, state_rglru, meta_tokens, ln_mix, w_in, attn_sinks,
           w_conv, b_conv, w_rg_a, b_rg_a, w_rg_i, b_rg_i, rg_lambda, w_att_proj, w_rec_proj, w_out, ln_ffn,
           w_ffn_gate, w_ffn_up, w_ffn_down, w_router, b_router, w_exp_gate, w_exp_up, w_exp_down, ln_final):
    bp, seq, _ = x_prompt.shape
    db, ds, _ = x_sample.shape
    depth = w_in.shape[0]
    pad = (-(N_META + seq)) % BLOCK
    lp = pad + N_META + seq
    tp, ts = lp * bp, ds * db
    assert tp % TM == 0 and ts % TM == 0 and db % SAMPLE_BB == 0 and ds >= CONV_W - 1
    assert seq >= WINDOW and bp & (bp - 1) == 0 and db & (db - 1) == 0 and ds & (ds - 1) == 0
    assert lp % RG_TCN == 0 and tp % ts == 0

    meta = jnp.broadcast_to(meta_tokens[:, None, :], (N_META, bp, D_MODEL))
    xp = jnp.concatenate([jnp.zeros((pad, bp, D_MODEL), F32), meta, jnp.transpose(x_prompt, (1, 0, 2))], axis=0)
    x = jnp.concatenate([xp.reshape(tp, D_MODEL), jnp.transpose(x_sample, (1, 0, 2)).reshape(ts, D_MODEL)], axis=0)

    zeros_conv = jnp.zeros(((CONV_W - 1) * bp, REC_W), F32)
    zeros_h = jnp.zeros((bp, REC_W), F32)
    outs = {k: [] for k in ("kp", "vp", "cp", "hp", "ks", "vs", "cs", "hs")}

    for l in range(depth):
        q, kv, xr, yr, g = _in_proj(x, ln_mix[l].reshape(1, D_MODEL), w_in[l].astype(BF16))

        att_p = _attn_prompt(q, kv, attn_sinks[l], bp, lp, pad)
        att_s, nk, nv = _attn_sample(q[tp:], kv[tp:], cache_k[l].reshape(db, WINDOW, KV_W),
                                     cache_v[l].reshape(db, WINDOW, KV_W), attn_sinks[l], db, ds)

        rg_w = (w_conv[l], b_conv[l].reshape(1, REC_W), _gate_weights(w_rg_a[l], w_rg_i[l]),
                b_rg_a[l].reshape(1, REC_W), b_rg_i[l].reshape(1, REC_W), rg_lambda[l].reshape(1, REC_W))
        rec_p, h_p = _rglru(xr, yr, 0, lp // RG_TCN, bp, RG_TCN, -pad, zeros_conv, zeros_h, rg_w)
        conv_s = jnp.transpose(state_conv[l], (1, 0, 2)).reshape((CONV_W - 1) * db, REC_W)
        rec_s, h_s = _rglru(xr, yr, tp // ts, 1, db, ds, PAST_LEN, conv_s, state_rglru[l], rg_w)

        x = _merge(att_p, att_s, rec_p, rec_s, g, x, w_att_proj[l].astype(BF16), w_rec_proj[l].astype(BF16),
                   w_out[l].astype(BF16))

        j = l // 2
        ln_f = ln_ffn[l].reshape(1, D_MODEL)
        if l % 2 == 0:
            x = _ffn(x, ln_f, w_ffn_gate[j].astype(BF16), w_ffn_up[j].astype(BF16), w_ffn_down[j].astype(BF16))
        else:
            w_r = jnp.zeros((D_MODEL, LANES), F32).at[:, :N_EXPERTS].set(w_router[j]).astype(BF16)
            b_r = jnp.full((1, LANES), ROUTER_PAD_BIAS, F32).at[0, :N_EXPERTS].set(b_router[j])
            x = _moe(x, ln_f, w_r, b_r, w_exp_gate[j].astype(BF16), w_exp_up[j].astype(BF16),
                     w_exp_down[j].astype(BF16))

        kv_p = kv[:tp].reshape(lp, bp, 2 * KV_W)[lp - WINDOW:]
        outs["kp"].append(jnp.transpose(kv_p[..., :KV_W], (1, 0, 2)).reshape(bp, WINDOW, N_KV_HEADS, HEAD_DIM))
        outs["vp"].append(jnp.transpose(kv_p[..., KV_W:], (1, 0, 2)).reshape(bp, WINDOW, N_KV_HEADS, HEAD_DIM))
        outs["cp"].append(jnp.transpose(xr[:tp].reshape(lp, bp, REC_W)[lp - (CONV_W - 1):], (1, 0, 2)))
        outs["hp"].append(h_p)
        outs["ks"].append(nk.reshape(db, WINDOW, N_KV_HEADS, HEAD_DIM))
        outs["vs"].append(nv.reshape(db, WINDOW, N_KV_HEADS, HEAD_DIM))
        outs["cs"].append(jnp.transpose(xr[tp:].reshape(ds, db, REC_W)[ds - (CONV_W - 1):], (1, 0, 2)))
        outs["hs"].append(h_s)

    y = _final_norm(x, ln_final.reshape(1, D_MODEL))
    y_prompt = jnp.transpose(y[:tp].reshape(lp, bp, D_MODEL)[pad + N_META:], (1, 0, 2))
    y_sample = jnp.transpose(y[tp:].reshape(ds, db, D_MODEL), (1, 0, 2))
    st = {k: jnp.stack(v) for k, v in outs.items()}
    return (y_prompt, y_sample, st["kp"], st["vp"], st["cp"], st["hp"], st["ks"], st["vs"], st["cs"], st["hs"])
```

```python
import functools

import jax
import jax.numpy as jnp
from jax import lax
from jax.experimental import pallas as pl
from jax.experimental.pallas import tpu as pltpu

F32 = jnp.float32
BF16 = jnp.bfloat16

D_MODEL = 1024
N_META = 16
N_HEADS = 16
N_KV_HEADS = 2
HEAD_DIM = D_MODEL // N_HEADS
GROUP = N_HEADS // N_KV_HEADS
ATT_W = N_HEADS * HEAD_DIM
KV_W = N_KV_HEADS * HEAD_DIM
WINDOW = 128
BLOCK = 128
ATT_SCALE = HEAD_DIM ** -0.5
REC_W = 5 * D_MODEL // 4
REC_BLOCKS = 16
REC_BW = REC_W // REC_BLOCKS
REC_HALF = REC_W // 2
CONV_W = 4
LRU_C = 8.0
D_FF = 3 * D_MODEL
N_EXPERTS = 8
EXPERT_FF = 3 * D_MODEL // 2
PAST_LEN = 8192
RMS_EPS = 1e-6
IN_OFFS = (0, ATT_W, ATT_W + 2 * KV_W, ATT_W + 2 * KV_W + REC_W, ATT_W + 2 * KV_W + 2 * REC_W,
           ATT_W + 2 * KV_W + 2 * REC_W + 2 * D_MODEL)
LANES = 128
VMEM_LIMIT = 52 * 1024 * 1024

TM = 512
ROUTER_PAD_BIAS = -1e30


def _cparams(*sem):
    return pltpu.CompilerParams(dimension_semantics=sem, vmem_limit_bytes=VMEM_LIMIT)


def _resident(shape):
    nd = len(shape)
    return pl.BlockSpec(shape, lambda *_: (0,) * nd, pipeline_mode=pl.Buffered(1))


def _rms(x, g):
    return x * lax.rsqrt(jnp.mean(x * x, axis=-1, keepdims=True) + RMS_EPS) * g


NBV = 8


def _to_col_view(val, scr, out_ref, dtype):
    rows, width = val.shape
    for l in range(width // LANES):
        scr[l] = val[:, l * LANES:(l + 1) * LANES]
    for b in range(NBV):
        for l in range(width // LANES):
            out_ref[:, b * width + l * LANES:b * width + (l + 1) * LANES] = (
                scr[l, pl.ds(b, rows // NBV, stride=NBV), :].astype(dtype))


def _in_proj_kernel(x_ref, ln_ref, w_ref, q_ref, kv_ref, xr_ref, yr_ref, g_ref, q_scr, kv_scr):
    hn = _rms(x_ref[...], ln_ref[...]).astype(BF16)

    def proj(a, b):
        return jnp.dot(hn, w_ref[:, a:b], preferred_element_type=F32)

    _to_col_view(proj(IN_OFFS[0], IN_OFFS[1]) * ATT_SCALE, q_scr, q_ref, BF16)
    _to_col_view(proj(IN_OFFS[1], IN_OFFS[2]), kv_scr, kv_ref, F32)
    xr_ref[...] = proj(IN_OFFS[2], IN_OFFS[3])
    yr_ref[...] = proj(IN_OFFS[3], IN_OFFS[4])
    g_ref[...] = proj(IN_OFFS[4], IN_OFFS[5])


def _in_proj(x, ln, w_in):
    t = x.shape[0]
    row = lambda w: pl.BlockSpec((TM, w), lambda i: (i, 0))
    view = lambda w: pl.BlockSpec((TM // NBV, NBV * w), lambda i: (i, 0))
    return pl.pallas_call(
        _in_proj_kernel,
        grid=(t // TM,),
        in_specs=[row(D_MODEL), _resident((1, D_MODEL)), _resident(w_in.shape)],
        out_specs=[view(ATT_W), view(2 * KV_W), row(REC_W), row(REC_W), row(2 * D_MODEL)],
        out_shape=[jax.ShapeDtypeStruct((t // NBV, NBV * ATT_W), BF16),
                   jax.ShapeDtypeStruct((t // NBV, NBV * 2 * KV_W), F32),
                   jax.ShapeDtypeStruct((t, REC_W), F32), jax.ShapeDtypeStruct((t, REC_W), F32),
                   jax.ShapeDtypeStruct((t, 2 * D_MODEL), F32)],
        scratch_shapes=[pltpu.VMEM((ATT_W // LANES, TM, LANES), F32), pltpu.VMEM((2 * KV_W // LANES, TM, LANES), F32)],
        compiler_params=_cparams("parallel"),
        name="in_proj",
    )(x, ln, w_in)


def _sink_softmax(s, mask, sk):
    s = jnp.where(mask, s, -jnp.inf)
    m = jnp.maximum(jnp.max(s, axis=-1, keepdims=True), sk)
    p = jnp.exp(s - m)
    return p / (jnp.sum(p, axis=-1, keepdims=True) + jnp.exp(sk - m))


def _attn_prompt_kernel(pad, sink_ref, q_ref, kvp_ref, kvc_ref, o_ref):
    n = pl.program_id(1)
    q = q_ref[...]
    kv = jnp.concatenate([kvp_ref[...], kvc_ref[...]], axis=0).astype(BF16)
    i = lax.broadcasted_iota(jnp.int32, (BLOCK, 2 * BLOCK), 0)
    j = lax.broadcasted_iota(jnp.int32, (BLOCK, 2 * BLOCK), 1)
    mask = (j >= i) & (j <= i + WINDOW) & (j >= pad + BLOCK - n * BLOCK)
    outs = []
    for h in range(N_HEADS):
        kvh = h // GROUP
        qh = q[:, h * HEAD_DIM:(h + 1) * HEAD_DIM]
        k = kv[:, kvh * HEAD_DIM:(kvh + 1) * HEAD_DIM]
        v = kv[:, KV_W + kvh * HEAD_DIM:KV_W + (kvh + 1) * HEAD_DIM]
        s = lax.dot_general(qh, k, (((1,), (1,)), ((), ())), preferred_element_type=F32)
        p = _sink_softmax(s, mask, sink_ref[h])
        outs.append(jnp.dot(p.astype(BF16), v, preferred_element_type=F32))
    o_ref[...] = jnp.concatenate(outs, axis=1).astype(BF16)


def _attn_prompt(q2, kv2, sinks, nb, lp, pad):
    return pl.pallas_call(
        functools.partial(_attn_prompt_kernel, pad),
        grid=(nb, lp // BLOCK),
        in_specs=[pl.BlockSpec(memory_space=pltpu.SMEM),
                  pl.BlockSpec((BLOCK, ATT_W), lambda b, n: (n, b)),
                  pl.BlockSpec((BLOCK, 2 * KV_W), lambda b, n: (jnp.maximum(n - 1, 0), b)),
                  pl.BlockSpec((BLOCK, 2 * KV_W), lambda b, n: (n, b))],
        out_specs=pl.BlockSpec((BLOCK, ATT_W), lambda b, n: (n, b)),
        out_shape=jax.ShapeDtypeStruct((lp, nb * ATT_W), BF16),
        compiler_params=_cparams("parallel", "parallel"),
        name="attn_prompt",
    )(sinks, q2, kv2, kv2)


SAMPLE_BB = 8


def _attn_sample_kernel(ds, sink_ref, q_ref, kvn_ref, ck_ref, cv_ref, o_ref, nk_ref, nv_ref):
    rows = GROUP * ds
    r = lax.broadcasted_iota(jnp.int32, (rows, WINDOW + ds), 0)
    j = lax.broadcasted_iota(jnp.int32, (rows, WINDOW + ds), 1)
    tq = r & (ds - 1)
    mask = (j >= tq) & (j <= tq + WINDOW)
    for bl in range(SAMPLE_BB):
        qb = q_ref[:, bl * ATT_W:(bl + 1) * ATT_W].astype(F32)
        kvb = kvn_ref[:, bl * 2 * KV_W:(bl + 1) * 2 * KV_W]
        ck = ck_ref[bl]
        cv = cv_ref[bl]
        nk_ref[bl] = jnp.concatenate([ck[ds:], kvb[:, :KV_W]], axis=0)
        nv_ref[bl] = jnp.concatenate([cv[ds:], kvb[:, KV_W:]], axis=0)
        pieces = []
        for kvh in range(N_KV_HEADS):
            lo, hi = kvh * HEAD_DIM, (kvh + 1) * HEAD_DIM
            k = jnp.concatenate([ck[:, lo:hi], kvb[:, lo:hi]], axis=0).astype(BF16)
            v = jnp.concatenate([cv[:, lo:hi], kvb[:, KV_W + lo:KV_W + hi]], axis=0).astype(BF16)
            qg = jnp.concatenate(
                [qb[:, (kvh * GROUP + g) * HEAD_DIM:(kvh * GROUP + g + 1) * HEAD_DIM] for g in range(GROUP)],
                axis=0).astype(BF16)
            s = lax.dot_general(qg, k, (((1,), (1,)), ((), ())), preferred_element_type=F32)
            p = _sink_softmax(s, mask, sink_ref[kvh][:, 0:1])
            o = jnp.dot(p.astype(BF16), v, preferred_element_type=F32)
            pieces += [o[g * ds:(g + 1) * ds] for g in range(GROUP)]
        o_ref[:, bl * ATT_W:(bl + 1) * ATT_W] = jnp.concatenate(pieces, axis=1).astype(BF16)


def _attn_sample(q, kv, cache_k, cache_v, sinks, nb, ds):
    q2 = q.reshape(ds, nb * ATT_W)
    kv2 = kv.reshape(ds, nb * 2 * KV_W)
    sink_rows = jnp.broadcast_to(
        jnp.repeat(sinks.reshape(N_KV_HEADS, GROUP), ds, axis=1)[:, :, None], (N_KV_HEADS, GROUP * ds, LANES))
    cache_spec = pl.BlockSpec((SAMPLE_BB, WINDOW, KV_W), lambda i: (i, 0, 0))
    att, nk, nv = pl.pallas_call(
        functools.partial(_attn_sample_kernel, ds),
        grid=(nb // SAMPLE_BB,),
        in_specs=[_resident(sink_rows.shape),
                  pl.BlockSpec((ds, SAMPLE_BB * ATT_W), lambda i: (0, i)),
                  pl.BlockSpec((ds, SAMPLE_BB * 2 * KV_W), lambda i: (0, i)),
                  cache_spec, cache_spec],
        out_specs=[pl.BlockSpec((ds, SAMPLE_BB * ATT_W), lambda i: (0, i)), cache_spec, cache_spec],
        out_shape=[jax.ShapeDtypeStruct((ds, nb * ATT_W), BF16),
                   jax.ShapeDtypeStruct(cache_k.shape, F32), jax.ShapeDtypeStruct(cache_v.shape, F32)],
        compiler_params=_cparams("parallel"),
        name="attn_sample",
    )(sink_rows, q2, kv2, cache_k, cache_v)
    return att.reshape(ds * nb, ATT_W), nk, nv


RG_ROWS = 256
RG_TCN = 64


def _rglru_kernel(nb, tcn, pos0, xr_ref, yr_ref, cprev_ref, hprev_ref, wc_ref, bc_ref, wg_ref, ba_ref, bi_ref,
                  lam_ref, rec_ref, hlast_ref, xbuf, a_scr, b_scr, hc):
    step = pl.program_id(0)
    rows = tcn * nb
    tail = (CONV_W - 1) * nb

    @pl.when(step == 0)
    def _():
        xbuf[0:tail, :] = cprev_ref[...]
        hc[...] = hprev_ref[...]

    @pl.when(step > 0)
    def _():
        xbuf[0:tail, :] = xbuf[rows:rows + tail, :]

    def row_pos(r0, n):
        t_row = lax.shift_right_logical(r0 + lax.broadcasted_iota(jnp.int32, (n, 1), 0), nb.bit_length() - 1)
        return pos0 + step * tcn + t_row

    sub = min(RG_ROWS, rows)
    for r0 in range(0, rows, sub):
        x = xr_ref[r0:r0 + sub, :]
        if pos0 < 0:
            x = jnp.where(row_pos(r0, sub) >= 0, x, 0.0)
        xbuf[tail + r0:tail + r0 + sub, :] = x

    lam = lam_ref[...]
    neg_c_softplus = -LRU_C * (jnp.maximum(-lam, 0.0) + jnp.log1p(jnp.exp(-jnp.abs(lam))))

    for r0 in range(0, rows, sub):
        xc = bc_ref[...]
        for jj in range(CONV_W):
            xc = xc + xbuf[r0 + jj * nb:r0 + jj * nb + sub, :] * wc_ref[jj:jj + 1, :]
        xcb = xc.astype(BF16)
        pos = row_pos(r0, sub)
        for c in range(2):
            lo, hi = c * REC_HALF, (c + 1) * REC_HALF
            gates = jnp.dot(xcb[:, lo:hi], wg_ref[c], preferred_element_type=F32)
            r = jax.nn.sigmoid(gates[:, :REC_HALF] + ba_ref[:, lo:hi])
            ig = jax.nn.sigmoid(gates[:, REC_HALF:] + bi_ref[:, lo:hi])
            a = jnp.exp(r * neg_c_softplus[:, lo:hi])
            mult = jnp.sqrt(1.0 - a * a)
            if pos0 <= 0:
                mult = jnp.where(pos == 0, 1.0, mult)
            bt = mult * ig * xc[:, lo:hi]
            if pos0 < 0:
                bt = jnp.where(pos >= 0, bt, 0.0)
            a_scr[r0:r0 + sub, lo:hi] = a
            b_scr[r0:r0 + sub, lo:hi] = bt

    if nb * REC_W <= 16 * 1024:
        def body(t, h):
            rr = pl.multiple_of(t * nb, nb)
            h = a_scr[pl.ds(rr, nb), :] * h + b_scr[pl.ds(rr, nb), :]
            b_scr[pl.ds(rr, nb), :] = h
            return h
        hc[...] = lax.fori_loop(0, tcn, body, hc[...])
    else:
        for t in range(tcn):
            h = a_scr[t * nb:(t + 1) * nb, :] * hc[...] + b_scr[t * nb:(t + 1) * nb, :]
            b_scr[t * nb:(t + 1) * nb, :] = h
            hc[...] = h

    hlast_ref[...] = hc[...]
    for r0 in range(0, rows, sub):
        rec_ref[r0:r0 + sub, :] = (b_scr[r0:r0 + sub, :] * jax.nn.gelu(yr_ref[r0:r0 + sub, :])).astype(BF16)


def _rglru(xr, yr, row_block0, n_chunks, nb, tcn, pos0, conv_prev, h_prev, wts):
    rows = tcn * nb
    tail = (CONV_W - 1) * nb
    blk = pl.BlockSpec((rows, REC_W), lambda i: (row_block0 + i, 0))
    rec, h_last = pl.pallas_call(
        functools.partial(_rglru_kernel, nb, tcn, pos0),
        grid=(n_chunks,),
        in_specs=[blk, blk, _resident(conv_prev.shape), _resident(h_prev.shape)] + [_resident(w.shape) for w in wts],
        out_specs=[pl.BlockSpec((rows, REC_W), lambda i: (i, 0)), pl.BlockSpec((nb, REC_W), lambda i: (0, 0))],
        out_shape=[jax.ShapeDtypeStruct((n_chunks * rows, REC_W), BF16), jax.ShapeDtypeStruct((nb, REC_W), F32)],
        scratch_shapes=[pltpu.VMEM((rows + tail, REC_W), F32), pltpu.VMEM((rows, REC_W), F32),
                        pltpu.VMEM((rows, REC_W), F32), pltpu.VMEM((nb, REC_W), F32)],
        compiler_params=_cparams("arbitrary"),
        name="rglru",
    )(xr, yr, conv_prev, h_prev, *wts)
    return rec, h_last


def _merge_kernel(n_p, attp_ref, atts_ref, recp_ref, recs_ref, g_ref, x_ref, wa_ref, wr_ref, wo_ref, o_ref,
                  att_scr, att_rows):
    is_p = pl.program_id(0) < n_p

    @pl.when(is_p)
    def _():
        for b in range(NBV):
            for l in range(ATT_W // LANES):
                att_scr[l, pl.ds(b, TM // NBV, stride=NBV), :] = (
                    attp_ref[:, b * ATT_W + l * LANES:b * ATT_W + (l + 1) * LANES].astype(F32))
        for l in range(ATT_W // LANES):
            att_rows[:, l * LANES:(l + 1) * LANES] = att_scr[l].astype(BF16)

    @pl.when(jnp.logical_not(is_p))
    def _():
        att_rows[...] = atts_ref[...]

    att = att_rows[...]
    rec = jnp.where(is_p, recp_ref[...], recs_ref[...])
    ya = jnp.dot(att, wa_ref[...], preferred_element_type=F32)
    yr = jnp.dot(rec, wr_ref[...], preferred_element_type=F32)
    merged = jax.nn.sigmoid(g_ref[:, :D_MODEL]) * ya + jax.nn.sigmoid(g_ref[:, D_MODEL:]) * yr
    o_ref[...] = x_ref[...] + jnp.dot(merged.astype(BF16), wo_ref[...], preferred_element_type=F32)


def _merge(att_p, att_s, rec_p, rec_s, g, x, wa, wr, wo):
    t = x.shape[0]
    n_p = rec_p.shape[0] // TM
    n_s = att_s.shape[0] // TM
    p_spec = lambda w: pl.BlockSpec((TM, w), lambda i: (jnp.minimum(i, n_p - 1), 0))
    s_spec = lambda w: pl.BlockSpec((TM, w), lambda i: (jnp.clip(i - n_p, 0, n_s - 1), 0))
    row = lambda w: pl.BlockSpec((TM, w), lambda i: (i, 0))
    view_spec = pl.BlockSpec((TM // NBV, NBV * ATT_W), lambda i: (jnp.minimum(i, n_p - 1), 0))
    return pl.pallas_call(
        functools.partial(_merge_kernel, n_p),
        grid=(t // TM,),
        in_specs=[view_spec, s_spec(ATT_W), p_spec(REC_W), s_spec(REC_W), row(2 * D_MODEL), row(D_MODEL),
                  _resident(wa.shape), _resident(wr.shape), _resident(wo.shape)],
        out_specs=row(D_MODEL),
        out_shape=jax.ShapeDtypeStruct((t, D_MODEL), F32),
        scratch_shapes=[pltpu.VMEM((ATT_W // LANES, TM, LANES), F32), pltpu.VMEM((TM, ATT_W), BF16)],
        compiler_params=_cparams("parallel"),
        name="merge_out",
    )(att_p, att_s, rec_p, rec_s, g, x, wa, wr, wo)


FF_CHUNK = 512


def _swiglu_mid(hn, wg_ref, wu_ref, mid_ref, width):
    for c in range(0, width, FF_CHUNK):
        gate = jnp.dot(hn, wg_ref[:, c:c + FF_CHUNK], preferred_element_type=F32)
        up = jnp.dot(hn, wu_ref[:, c:c + FF_CHUNK], preferred_element_type=F32)
        mid_ref[:, c:c + FF_CHUNK] = (jax.nn.silu(gate) * up).astype(BF16)


def _ffn_kernel(x_ref, ln_ref, wg_ref, wu_ref, wd_ref, o_ref, mid_ref):
    x = x_ref[...]
    hn = _rms(x, ln_ref[...]).astype(BF16)
    _swiglu_mid(hn, wg_ref, wu_ref, mid_ref, D_FF)
    o_ref[...] = x + jnp.dot(mid_ref[...], wd_ref[...], preferred_element_type=F32)


def _ffn(x, ln, wg, wu, wd):
    t = x.shape[0]
    row = pl.BlockSpec((TM, D_MODEL), lambda i: (i, 0))
    return pl.pallas_call(
        _ffn_kernel,
        grid=(t // TM,),
        in_specs=[row, _resident((1, D_MODEL)), _resident(wg.shape), _resident(wu.shape), _resident(wd.shape)],
        out_specs=row,
        out_shape=jax.ShapeDtypeStruct((t, D_MODEL), F32),
        scratch_shapes=[pltpu.VMEM((TM, D_FF), BF16)],
        compiler_params=_cparams("parallel"),
        name="ffn_dense",
    )(x, ln, wg, wu, wd)


TMG = 512
TOP_K = 2
DISPATCH_ROWS = 512
INFO_E, INFO_RANK, INFO_W = 0, 2, 4


def _route_kernel(x_ref, ln_ref, wr_ref, br_ref, hn_ref, info_ref, cnt_ref, base_scr):
    @pl.when(pl.program_id(0) == 0)
    def _():
        base_scr[...] = jnp.zeros_like(base_scr)

    hn = _rms(x_ref[...], ln_ref[...])
    hn_ref[...] = hn
    lane = lax.broadcasted_iota(jnp.int32, (TM, LANES), 1).astype(F32)
    logits = jnp.dot(hn.astype(BF16), wr_ref[...], preferred_element_type=F32) + br_ref[...]
    m1 = jnp.max(logits, axis=-1, keepdims=True)
    i1 = jnp.min(jnp.where(logits == m1, lane, float(LANES)), axis=-1, keepdims=True)
    rest = jnp.where(lane == i1, -jnp.inf, logits)
    m2 = jnp.max(rest, axis=-1, keepdims=True)
    i2 = jnp.min(jnp.where(rest == m2, lane, float(LANES)), axis=-1, keepdims=True)
    e2 = jnp.exp(m2 - m1)
    den = 1.0 + e2
    hit1 = lane == i1
    hit2 = lane == i2
    onehot = jnp.where(hit1 | hit2, 1.0, 0.0)
    r = lax.broadcasted_iota(jnp.int32, (TM, TM), 0)
    c = lax.broadcasted_iota(jnp.int32, (TM, TM), 1)
    before = jnp.dot(jnp.where(c < r, 1.0, 0.0).astype(BF16), onehot.astype(BF16), preferred_element_type=F32)
    before = before + base_scr[0:1, :]
    rank1 = jnp.sum(jnp.where(hit1, before, 0.0), axis=-1, keepdims=True)
    rank2 = jnp.sum(jnp.where(hit2, before, 0.0), axis=-1, keepdims=True)
    base_scr[0:1, :] = base_scr[0:1, :] + jnp.sum(onehot, axis=0, keepdims=True)
    cnt_ref[...] = base_scr[...]
    info = jnp.zeros((TM, LANES), F32)
    for k, v in ((INFO_E, i1), (INFO_E + 1, i2), (INFO_RANK, rank1), (INFO_RANK + 1, rank2),
                 (INFO_W, 1.0 / den), (INFO_W + 1, e2 / den)):
        info = jnp.where(lane == float(k), v, info)
    info_ref[...] = info


def _route(x, ln, w_router, b_router):
    t = x.shape[0]
    row = lambda w: pl.BlockSpec((TM, w), lambda i: (i, 0))
    return pl.pallas_call(
        _route_kernel,
        grid=(t // TM,),
        in_specs=[row(D_MODEL), _resident((1, D_MODEL)), _resident(w_router.shape), _resident(b_router.shape)],
        out_specs=[row(D_MODEL), row(LANES), pl.BlockSpec((8, LANES), lambda i: (0, 0))],
        out_shape=[jax.ShapeDtypeStruct((t, D_MODEL), F32), jax.ShapeDtypeStruct((t, LANES), F32),
                   jax.ShapeDtypeStruct((8, LANES), F32)],
        scratch_shapes=[pltpu.VMEM((8, LANES), F32)],
        compiler_params=_cparams("arbitrary"),
        name="moe_route",
    )(x, ln, w_router, b_router)


def _dispatch_kernel(pos_hbm, hn_ref, xs_in, xs_out, pos_smem, idx_sem, row_sem):
    del xs_in
    i = pl.program_id(0)
    idx_copy = pltpu.make_async_copy(pos_hbm.at[i], pos_smem, idx_sem)
    idx_copy.start()
    idx_copy.wait()

    def issue(tk, carry):
        for k in range(TOP_K):
            pltpu.make_async_copy(hn_ref.at[pl.ds(tk, 1)], xs_out.at[pl.ds(pos_smem[TOP_K * tk + k], 1)],
                                  row_sem).start()
        return carry

    lax.fori_loop(0, DISPATCH_ROWS, issue, 0, unroll=8)
    for k in range(TOP_K):
        pltpu.make_async_copy(hn_ref, xs_out.at[pl.ds(0, DISPATCH_ROWS)], row_sem).wait()


def _dispatch(pos, hn, n_rows):
    t = hn.shape[0]
    steps = t // DISPATCH_ROWS
    xs0 = jnp.zeros((n_rows, D_MODEL), F32)
    return pl.pallas_call(
        _dispatch_kernel,
        grid=(steps,),
        in_specs=[pl.BlockSpec(memory_space=pl.ANY), pl.BlockSpec((DISPATCH_ROWS, D_MODEL), lambda i: (i, 0)),
                  pl.BlockSpec(memory_space=pl.ANY)],
        out_specs=pl.BlockSpec(memory_space=pl.ANY),
        out_shape=jax.ShapeDtypeStruct((n_rows, D_MODEL), F32),
        scratch_shapes=[pltpu.SMEM((TOP_K * DISPATCH_ROWS,), jnp.int32), pltpu.SemaphoreType.DMA,
                        pltpu.SemaphoreType.DMA],
        input_output_aliases={2: 0},
        compiler_params=_cparams("arbitrary"),
        name="moe_dispatch",
    )(pos.reshape(steps, TOP_K * DISPATCH_ROWS), hn, xs0)


def _expert_kernel(te_ref, nu_ref, xs_ref, wg_ref, wu_ref, wd_ref, y_ref, mid_ref):
    del te_ref

    @pl.when(pl.program_id(0) < nu_ref[0])
    def _():
        _swiglu_mid(xs_ref[...].astype(BF16), wg_ref, wu_ref, mid_ref, EXPERT_FF)
        y_ref[...] = jnp.dot(mid_ref[...], wd_ref[...], preferred_element_type=F32)

    @pl.when(pl.program_id(0) >= nu_ref[0])
    def _():
        y_ref[...] = jnp.zeros_like(y_ref)


def _experts(tile_expert, n_used, xs, wg, wu, wd):
    n_rows = xs.shape[0]
    row = pl.BlockSpec((TMG, D_MODEL), lambda i, te, nu: (i, 0))
    ex = lambda a, b: pl.BlockSpec((None, a, b), lambda i, te, nu: (te[i], 0, 0))
    return pl.pallas_call(
        _expert_kernel,
        grid_spec=pltpu.PrefetchScalarGridSpec(
            num_scalar_prefetch=2,
            grid=(n_rows // TMG,),
            in_specs=[row, ex(D_MODEL, EXPERT_FF), ex(D_MODEL, EXPERT_FF), ex(EXPERT_FF, D_MODEL)],
            out_specs=row,
            scratch_shapes=[pltpu.VMEM((TMG, EXPERT_FF), BF16)]),
        out_shape=jax.ShapeDtypeStruct((n_rows, D_MODEL), F32),
        compiler_params=_cparams("arbitrary"),
        name="moe_experts",
    )(tile_expert, n_used, xs, wg, wu, wd)


def _combine_kernel(pos_hbm, x_ref, info_ref, y_hbm, o_ref, pos_smem, buf, idx_sem, row_sem):
    i = pl.program_id(0)
    n = pl.num_programs(0)

    def gather(tile, slot):
        idx_copy = pltpu.make_async_copy(pos_hbm.at[tile], pos_smem.at[slot], idx_sem)
        idx_copy.start()
        idx_copy.wait()

        def issue(tk, carry):
            for k in range(TOP_K):
                pltpu.make_async_copy(y_hbm.at[pl.ds(pos_smem[slot, TOP_K * tk + k], 1)],
                                      buf.at[slot, k, pl.ds(tk, 1)], row_sem.at[slot]).start()
            return carry

        lax.fori_loop(0, TM, issue, 0, unroll=8)

    @pl.when(i == 0)
    def _():
        gather(0, 0)

    @pl.when(i + 1 < n)
    def _():
        gather(i + 1, (i + 1) % 2)

    slot = i % 2
    for k in range(TOP_K):
        pltpu.make_async_copy(y_hbm.at[pl.ds(0, TM)], buf.at[slot, k], row_sem.at[slot]).wait()
    lane = lax.broadcasted_iota(jnp.int32, (TM, LANES), 1)
    info = info_ref[...]
    out = x_ref[...]
    for k in range(TOP_K):
        w_k = jnp.sum(jnp.where(lane == INFO_W + k, info, 0.0), axis=-1, keepdims=True)
        out = out + w_k * buf[slot, k]
    o_ref[...] = out


def _combine(pos, x, info, y):
    t = x.shape[0]
    steps = t // TM
    row = lambda w: pl.BlockSpec((TM, w), lambda i: (i, 0))
    return pl.pallas_call(
        _combine_kernel,
        grid=(steps,),
        in_specs=[pl.BlockSpec(memory_space=pl.ANY), row(D_MODEL), row(LANES), pl.BlockSpec(memory_space=pl.ANY)],
        out_specs=row(D_MODEL),
        out_shape=jax.ShapeDtypeStruct((t, D_MODEL), F32),
        scratch_shapes=[pltpu.SMEM((2, TOP_K * TM), jnp.int32), pltpu.VMEM((2, TOP_K, TM, D_MODEL), F32),
                        pltpu.SemaphoreType.DMA, pltpu.SemaphoreType.DMA((2,))],
        compiler_params=_cparams("arbitrary"),
        name="moe_combine",
    )(pos.reshape(steps, TOP_K * TM), x, info, y)


def _moe(x, ln, w_router, b_router, wg, wu, wd):
    t = x.shape[0]
    hn, info, cnt = _route(x, ln, w_router, b_router)
    counts = cnt[0, :N_EXPERTS].astype(jnp.int32)
    tiles = (counts + TMG - 1) // TMG
    tile_end = jnp.cumsum(tiles)
    offsets = (tile_end - tiles) * TMG
    n_tiles = TOP_K * t // TMG + N_EXPERTS
    tile_expert = jnp.minimum(jnp.sum(jnp.arange(n_tiles)[:, None] >= tile_end[None, :], axis=1),
                              N_EXPERTS - 1).astype(jnp.int32)
    n_used = tile_end[N_EXPERTS - 1:].astype(jnp.int32)
    e = info[:, INFO_E:INFO_E + TOP_K].astype(jnp.int32)
    pos = (jnp.sum(jnp.where(e[..., None] == jnp.arange(N_EXPERTS), offsets, 0), axis=-1)
           + info[:, INFO_RANK:INFO_RANK + TOP_K].astype(jnp.int32)).reshape(-1)
    xs = _dispatch(pos, hn, n_tiles * TMG)
    y = _experts(tile_expert, n_used, xs, wg, wu, wd)
    return _combine(pos, x, info, y)


def _final_norm_kernel(x_ref, ln_ref, o_ref):
    o_ref[...] = _rms(x_ref[...], ln_ref[...])


def _final_norm(x, ln):
    t = x.shape[0]
    row = pl.BlockSpec((TM, D_MODEL), lambda i: (i, 0))
    return pl.pallas_call(
        _final_norm_kernel,
        grid=(t // TM,),
        in_specs=[row, _resident((1, D_MODEL))],
        out_specs=row,
        out_shape=jax.ShapeDtypeStruct((t, D_MODEL), F32),
        compiler_params=_cparams("parallel"),
        name="final_norm",
    )(x, ln)


def _gate_weights(w_a, w_i):
    half_blocks = REC_BLOCKS // 2
    eye = jnp.eye(half_blocks, dtype=F32)

    def dense(w):
        return (eye[:, None, :, None] * w[:, :, None, :]).reshape(REC_HALF, REC_HALF)

    halves = [jnp.concatenate([dense(w_a[c * half_blocks:(c + 1) * half_blocks]),
                               dense(w_i[c * half_blocks:(c + 1) * half_blocks])], axis=1) for c in range(2)]
    return jnp.stack(halves).astype(BF16)


def kernel(x_prompt, x_sample, cache_k, cache_v, state_conv, state_rglru, meta_tokens, ln_mix, w_in, attn_sinks,
           w_conv, b_conv, w_rg_a, b_rg_a, w_rg_i, b_rg_i, rg_lambda, w_att_proj, w_rec_proj, w_out, ln_ffn,
           w_ffn_gate, w_ffn_up, w_ffn_down, w_router, b_router, w_exp_gate, w_exp_up, w_exp_down, ln_final):
    bp, seq, _ = x_prompt.shape
    db, ds, _ = x_sample.shape
    depth = w_in.shape[0]
    pad = (-(N_META + seq)) % BLOCK
    lp = pad + N_META + seq
    tp, ts = lp * bp, ds * db
    assert tp % TM == 0 and ts % TM == 0 and db % SAMPLE_BB == 0 and ds >= CONV_W - 1
    assert seq >= WINDOW and bp & (bp - 1) == 0 and db & (db - 1) == 0 and ds & (ds - 1) == 0
    assert lp % RG_TCN == 0 and tp % ts == 0 and bp == NBV

    meta = jnp.broadcast_to(meta_tokens[:, None, :], (N_META, bp, D_MODEL))
    xp = jnp.concatenate([jnp.zeros((pad, bp, D_MODEL), F32), meta, jnp.transpose(x_prompt, (1, 0, 2))], axis=0)
    x = jnp.concatenate([xp.reshape(tp, D_MODEL), jnp.transpose(x_sample, (1, 0, 2)).reshape(ts, D_MODEL)], axis=0)

    zeros_conv = jnp.zeros(((CONV_W - 1) * bp, REC_W), F32)
    zeros_h = jnp.zeros((bp, REC_W), F32)
    outs = {k: [] for k in ("kp", "vp", "cp", "hp", "ks", "vs", "cs", "hs")}

    for l in range(depth):
        q2, kv2, xr, yr, g = _in_proj(x, ln_mix[l].reshape(1, D_MODEL), w_in[l].astype(BF16))

        att_p = _attn_prompt(q2, kv2, attn_sinks[l], bp, lp, pad)
        att_s, nk, nv = _attn_sample(q2[lp:].reshape(ts, ATT_W), kv2[lp:].reshape(ts, 2 * KV_W),
                                     cache_k[l].reshape(db, WINDOW, KV_W),
                                     cache_v[l].reshape(db, WINDOW, KV_W), attn_sinks[l], db, ds)

        rg_w = (w_conv[l], b_conv[l].reshape(1, REC_W), _gate_weights(w_rg_a[l], w_rg_i[l]),
                b_rg_a[l].reshape(1, REC_W), b_rg_i[l].reshape(1, REC_W), rg_lambda[l].reshape(1, REC_W))
        rec_p, h_p = _rglru(xr, yr, 0, lp // RG_TCN, bp, RG_TCN, -pad, zeros_conv, zeros_h, rg_w)
        conv_s = jnp.transpose(state_conv[l], (1, 0, 2)).reshape((CONV_W - 1) * db, REC_W)
        rec_s, h_s = _rglru(xr, yr, tp // ts, 1, db, ds, PAST_LEN, conv_s, state_rglru[l], rg_w)

        x = _merge(att_p, att_s, rec_p, rec_s, g, x, w_att_proj[l].astype(BF16), w_rec_proj[l].astype(BF16),
                   w_out[l].astype(BF16))

        j = l // 2
        ln_f = ln_ffn[l].reshape(1, D_MODEL)
        if l % 2 == 0:
            x = _ffn(x, ln_f, w_ffn_gate[j].astype(BF16), w_ffn_up[j].astype(BF16), w_ffn_down[j].astype(BF16))
        else:
            w_r = jnp.zeros((D_MODEL, LANES), F32).at[:, :N_EXPERTS].set(w_router[j]).astype(BF16)
            b_r = jnp.full((1, LANES), ROUTER_PAD_BIAS, F32).at[0, :N_EXPERTS].set(b_router[j])
            x = _moe(x, ln_f, w_r, b_r, w_exp_gate[j].astype(BF16), w_exp_up[j].astype(BF16),
                     w_exp_down[j].astype(BF16))

        kv_p = kv2[lp - WINDOW:lp].reshape(WINDOW, bp, 2 * KV_W)
        outs["kp"].append(jnp.transpose(kv_p[..., :KV_W], (1, 0, 2)).reshape(bp, WINDOW, N_KV_HEADS, HEAD_DIM))
        outs["vp"].append(jnp.transpose(kv_p[..., KV_W:], (1, 0, 2)).reshape(bp, WINDOW, N_KV_HEADS, HEAD_DIM))
        outs["cp"].append(jnp.transpose(xr[tp - (CONV_W - 1) * bp:tp].reshape(CONV_W - 1, bp, REC_W), (1, 0, 2)))
        outs["hp"].append(h_p)
        outs["ks"].append(nk.reshape(db, WINDOW, N_KV_HEADS, HEAD_DIM))
        outs["vs"].append(nv.reshape(db, WINDOW, N_KV_HEADS, HEAD_DIM))
        outs["cs"].append(jnp.transpose(xr[tp + ts - (CONV_W - 1) * db:].reshape(CONV_W - 1, db, REC_W), (1, 0, 2)))
        outs["hs"].append(h_s)

    y = _final_norm(x, ln_final.reshape(1, D_MODEL))
    y_prompt = jnp.transpose(y[(pad + N_META) * bp:tp].reshape(seq, bp, D_MODEL), (1, 0, 2))
    y_sample = jnp.transpose(y[tp:].reshape(ds, db, D_MODEL), (1, 0, 2))
    st = {k: jnp.stack(v) for k, v in outs.items()}
    return (y_prompt, y_sample, st["kp"], st["vp"], st["cp"], st["hp"], st["ks"], st["vs"], st["cs"], st["hs"])
```

```python
import functools

import jax
import jax.numpy as jnp
from jax import lax
from jax.experimental import pallas as pl
from jax.experimental.pallas import tpu as pltpu

F32 = jnp.float32
BF16 = jnp.bfloat16

D_MODEL = 1024
N_META = 16
N_HEADS = 16
N_KV_HEADS = 2
HEAD_DIM = D_MODEL // N_HEADS
GROUP = N_HEADS // N_KV_HEADS
ATT_W = N_HEADS * HEAD_DIM
KV_W = N_KV_HEADS * HEAD_DIM
WINDOW = 128
BLOCK = 128
ATT_SCALE = HEAD_DIM ** -0.5
REC_W = 5 * D_MODEL // 4
REC_BLOCKS = 16
REC_BW = REC_W // REC_BLOCKS
REC_HALF = REC_W // 2
CONV_W = 4
LRU_C = 8.0
D_FF = 3 * D_MODEL
N_EXPERTS = 8
EXPERT_FF = 3 * D_MODEL // 2
PAST_LEN = 8192
RMS_EPS = 1e-6
IN_OFFS = (0, ATT_W, ATT_W + 2 * KV_W, ATT_W + 2 * KV_W + REC_W, ATT_W + 2 * KV_W + 2 * REC_W,
           ATT_W + 2 * KV_W + 2 * REC_W + 2 * D_MODEL)
LANES = 128
VMEM_LIMIT = 52 * 1024 * 1024

TM = 512
ROUTER_PAD_BIAS = -1e30


def _cparams(*sem):
    return pltpu.CompilerParams(dimension_semantics=sem, vmem_limit_bytes=VMEM_LIMIT)


def _resident(shape):
    nd = len(shape)
    return pl.BlockSpec(shape, lambda *_: (0,) * nd, pipeline_mode=pl.Buffered(1))


def _rms(x, g):
    return x * lax.rsqrt(jnp.mean(x * x, axis=-1, keepdims=True) + RMS_EPS) * g


NBV = 8


def _to_col_view(val, scr, out_ref, dtype):
    rows, width = val.shape
    for l in range(width // LANES):
        scr[l] = val[:, l * LANES:(l + 1) * LANES]
    for b in range(NBV):
        for l in range(width // LANES):
            out_ref[:, b * width + l * LANES:b * width + (l + 1) * LANES] = (
                scr[l, pl.ds(b, rows // NBV, stride=NBV), :].astype(dtype))


def _in_proj_kernel(x_ref, ln_ref, w_ref, q_ref, kv_ref, xr_ref, yr_ref, g_ref, q_scr, kv_scr):
    hn = _rms(x_ref[...], ln_ref[...]).astype(BF16)

    def proj(a, b):
        return jnp.dot(hn, w_ref[:, a:b], preferred_element_type=F32)

    _to_col_view(proj(IN_OFFS[0], IN_OFFS[1]) * ATT_SCALE, q_scr, q_ref, BF16)
    _to_col_view(proj(IN_OFFS[1], IN_OFFS[2]), kv_scr, kv_ref, F32)
    xr_ref[...] = proj(IN_OFFS[2], IN_OFFS[3])
    yr_ref[...] = proj(IN_OFFS[3], IN_OFFS[4])
    g_ref[...] = proj(IN_OFFS[4], IN_OFFS[5])


def _in_proj(x, ln, w_in):
    t = x.shape[0]
    row = lambda w: pl.BlockSpec((TM, w), lambda i: (i, 0))
    view = lambda w: pl.BlockSpec((TM // NBV, NBV * w), lambda i: (i, 0))
    return pl.pallas_call(
        _in_proj_kernel,
        grid=(t // TM,),
        in_specs=[row(D_MODEL), _resident((1, D_MODEL)), _resident(w_in.shape)],
        out_specs=[view(ATT_W), view(2 * KV_W), row(REC_W), row(REC_W), row(2 * D_MODEL)],
        out_shape=[jax.ShapeDtypeStruct((t // NBV, NBV * ATT_W), BF16),
                   jax.ShapeDtypeStruct((t // NBV, NBV * 2 * KV_W), F32),
                   jax.ShapeDtypeStruct((t, REC_W), F32), jax.ShapeDtypeStruct((t, REC_W), F32),
                   jax.ShapeDtypeStruct((t, 2 * D_MODEL), F32)],
        scratch_shapes=[pltpu.VMEM((ATT_W // LANES, TM, LANES), F32), pltpu.VMEM((2 * KV_W // LANES, TM, LANES), F32)],
        compiler_params=_cparams("parallel"),
        name="in_proj",
    )(x, ln, w_in)


def _sink_softmax(s, mask, sk):
    s = jnp.where(mask, s, -jnp.inf)
    m = jnp.maximum(jnp.max(s, axis=-1, keepdims=True), sk)
    p = jnp.exp(s - m)
    return p / (jnp.sum(p, axis=-1, keepdims=True) + jnp.exp(sk - m))


def _attn_prompt_kernel(pad, sink_ref, q_ref, kvp_ref, kvc_ref, o_ref):
    n = pl.program_id(1)
    q = q_ref[...]
    kv = jnp.concatenate([kvp_ref[...], kvc_ref[...]], axis=0).astype(BF16)
    i = lax.broadcasted_iota(jnp.int32, (BLOCK, 2 * BLOCK), 0)
    j = lax.broadcasted_iota(jnp.int32, (BLOCK, 2 * BLOCK), 1)
    mask = (j >= i) & (j <= i + WINDOW) & (j >= pad + BLOCK - n * BLOCK)
    outs = []
    for h in range(N_HEADS):
        kvh = h // GROUP
        qh = q[:, h * HEAD_DIM:(h + 1) * HEAD_DIM]
        k = kv[:, kvh * HEAD_DIM:(kvh + 1) * HEAD_DIM]
        v = kv[:, KV_W + kvh * HEAD_DIM:KV_W + (kvh + 1) * HEAD_DIM]
        s = lax.dot_general(qh, k, (((1,), (1,)), ((), ())), preferred_element_type=F32)
        p = _sink_softmax(s, mask, sink_ref[h])
        outs.append(jnp.dot(p.astype(BF16), v, preferred_element_type=F32))
    o_ref[...] = jnp.concatenate(outs, axis=1).astype(BF16)


def _attn_prompt(q2, kv2, sinks, nb, lp, pad):
    return pl.pallas_call(
        functools.partial(_attn_prompt_kernel, pad),
        grid=(nb, lp // BLOCK),
        in_specs=[pl.BlockSpec(memory_space=pltpu.SMEM),
                  pl.BlockSpec((BLOCK, ATT_W), lambda b, n: (n, b)),
                  pl.BlockSpec((BLOCK, 2 * KV_W), lambda b, n: (jnp.maximum(n - 1, 0), b)),
                  pl.BlockSpec((BLOCK, 2 * KV_W), lambda b, n: (n, b))],
        out_specs=pl.BlockSpec((BLOCK, ATT_W), lambda b, n: (n, b)),
        out_shape=jax.ShapeDtypeStruct((lp, nb * ATT_W), BF16),
        compiler_params=_cparams("parallel", "parallel"),
        name="attn_prompt",
    )(sinks, q2, kv2, kv2)


SAMPLE_BB = 8


def _attn_sample_kernel(ds, sink_ref, q_ref, kvn_ref, ck_ref, cv_ref, o_ref, nk_ref, nv_ref):
    rows = GROUP * ds
    r = lax.broadcasted_iota(jnp.int32, (rows, WINDOW + ds), 0)
    j = lax.broadcasted_iota(jnp.int32, (rows, WINDOW + ds), 1)
    tq = r & (ds - 1)
    mask = (j >= tq) & (j <= tq + WINDOW)
    for bl in range(SAMPLE_BB):
        qb = q_ref[:, bl * ATT_W:(bl + 1) * ATT_W].astype(F32)
        kvb = kvn_ref[:, bl * 2 * KV_W:(bl + 1) * 2 * KV_W]
        ck = ck_ref[bl]
        cv = cv_ref[bl]
        nk_ref[bl] = jnp.concatenate([ck[ds:], kvb[:, :KV_W]], axis=0)
        nv_ref[bl] = jnp.concatenate([cv[ds:], kvb[:, KV_W:]], axis=0)
        pieces = []
        for kvh in range(N_KV_HEADS):
            lo, hi = kvh * HEAD_DIM, (kvh + 1) * HEAD_DIM
            k = jnp.concatenate([ck[:, lo:hi], kvb[:, lo:hi]], axis=0).astype(BF16)
            v = jnp.concatenate([cv[:, lo:hi], kvb[:, KV_W + lo:KV_W + hi]], axis=0).astype(BF16)
            qg = jnp.concatenate(
                [qb[:, (kvh * GROUP + g) * HEAD_DIM:(kvh * GROUP + g + 1) * HEAD_DIM] for g in range(GROUP)],
                axis=0).astype(BF16)
            s = lax.dot_general(qg, k, (((1,), (1,)), ((), ())), preferred_element_type=F32)
            p = _sink_softmax(s, mask, sink_ref[kvh][:, 0:1])
            o = jnp.dot(p.astype(BF16), v, preferred_element_type=F32)
            pieces += [o[g * ds:(g + 1) * ds] for g in range(GROUP)]
        o_ref[:, bl * ATT_W:(bl + 1) * ATT_W] = jnp.concatenate(pieces, axis=1).astype(BF16)


def _attn_sample(q, kv, cache_k, cache_v, sinks, nb, ds):
    q2 = q.reshape(ds, nb * ATT_W)
    kv2 = kv.reshape(ds, nb * 2 * KV_W)
    sink_rows = jnp.broadcast_to(
        jnp.repeat(sinks.reshape(N_KV_HEADS, GROUP), ds, axis=1)[:, :, None], (N_KV_HEADS, GROUP * ds, LANES))
    cache_spec = pl.BlockSpec((SAMPLE_BB, WINDOW, KV_W), lambda i: (i, 0, 0))
    att, nk, nv = pl.pallas_call(
        functools.partial(_attn_sample_kernel, ds),
        grid=(nb // SAMPLE_BB,),
        in_specs=[_resident(sink_rows.shape),
                  pl.BlockSpec((ds, SAMPLE_BB * ATT_W), lambda i: (0, i)),
                  pl.BlockSpec((ds, SAMPLE_BB * 2 * KV_W), lambda i: (0, i)),
                  cache_spec, cache_spec],
        out_specs=[pl.BlockSpec((ds, SAMPLE_BB * ATT_W), lambda i: (0, i)), cache_spec, cache_spec],
        out_shape=[jax.ShapeDtypeStruct((ds, nb * ATT_W), BF16),
                   jax.ShapeDtypeStruct(cache_k.shape, F32), jax.ShapeDtypeStruct(cache_v.shape, F32)],
        compiler_params=_cparams("parallel"),
        name="attn_sample",
    )(sink_rows, q2, kv2, cache_k, cache_v)
    return att.reshape(ds * nb, ATT_W), nk, nv


RG_ROWS = 256
RG_TCN = 64


def _rglru_kernel(nb, tcn, pos0, xr_ref, yr_ref, cprev_ref, hprev_ref, wc_ref, bc_ref, wg_ref, ba_ref, bi_ref,
                  lam_ref, rec_ref, hlast_ref, xbuf, a_scr, b_scr, hc):
    step = pl.program_id(0)
    rows = tcn * nb
    tail = (CONV_W - 1) * nb

    @pl.when(step == 0)
    def _():
        xbuf[0:tail, :] = cprev_ref[...]
        hc[...] = hprev_ref[...]

    @pl.when(step > 0)
    def _():
        xbuf[0:tail, :] = xbuf[rows:rows + tail, :]

    def row_pos(r0, n):
        t_row = lax.shift_right_logical(r0 + lax.broadcasted_iota(jnp.int32, (n, 1), 0), nb.bit_length() - 1)
        return pos0 + step * tcn + t_row

    sub = min(RG_ROWS, rows)
    for r0 in range(0, rows, sub):
        x = xr_ref[r0:r0 + sub, :]
        if pos0 < 0:
            x = jnp.where(row_pos(r0, sub) >= 0, x, 0.0)
        xbuf[tail + r0:tail + r0 + sub, :] = x

    lam = lam_ref[...]
    neg_c_softplus = -LRU_C * (jnp.maximum(-lam, 0.0) + jnp.log1p(jnp.exp(-jnp.abs(lam))))

    for r0 in range(0, rows, sub):
        xc = bc_ref[...]
        for jj in range(CONV_W):
            xc = xc + xbuf[r0 + jj * nb:r0 + jj * nb + sub, :] * wc_ref[jj:jj + 1, :]
        xcb = xc.astype(BF16)
        pos = row_pos(r0, sub)
        for c in range(2):
            lo, hi = c * REC_HALF, (c + 1) * REC_HALF
            gates = jnp.dot(xcb[:, lo:hi], wg_ref[c], preferred_element_type=F32)
            r = jax.nn.sigmoid(gates[:, :REC_HALF] + ba_ref[:, lo:hi])
            ig = jax.nn.sigmoid(gates[:, REC_HALF:] + bi_ref[:, lo:hi])
            a = jnp.exp(r * neg_c_softplus[:, lo:hi])
            mult = jnp.sqrt(1.0 - a * a)
            if pos0 <= 0:
                mult = jnp.where(pos == 0, 1.0, mult)
            bt = mult * ig * xc[:, lo:hi]
            if pos0 < 0:
                bt = jnp.where(pos >= 0, bt, 0.0)
            a_scr[r0:r0 + sub, lo:hi] = a
            b_scr[r0:r0 + sub, lo:hi] = bt

    if nb * REC_W <= 16 * 1024:
        def body(t, h):
            rr = pl.multiple_of(t * nb, nb)
            h = a_scr[pl.ds(rr, nb), :] * h + b_scr[pl.ds(rr, nb), :]
            b_scr[pl.ds(rr, nb), :] = h
            return h
        hc[...] = lax.fori_loop(0, tcn, body, hc[...])
    else:
        for t in range(tcn):
            h = a_scr[t * nb:(t + 1) * nb, :] * hc[...] + b_scr[t * nb:(t + 1) * nb, :]
            b_scr[t * nb:(t + 1) * nb, :] = h
            hc[...] = h

    hlast_ref[...] = hc[...]
    for r0 in range(0, rows, sub):
        rec_ref[r0:r0 + sub, :] = (b_scr[r0:r0 + sub, :] * jax.nn.gelu(yr_ref[r0:r0 + sub, :])).astype(BF16)


def _rglru(xr, yr, row_block0, n_chunks, nb, tcn, pos0, conv_prev, h_prev, wts):
    rows = tcn * nb
    tail = (CONV_W - 1) * nb
    blk = pl.BlockSpec((rows, REC_W), lambda i: (row_block0 + i, 0))
    rec, h_last = pl.pallas_call(
        functools.partial(_rglru_kernel, nb, tcn, pos0),
        grid=(n_chunks,),
        in_specs=[blk, blk, _resident(conv_prev.shape), _resident(h_prev.shape)] + [_resident(w.shape) for w in wts],
        out_specs=[pl.BlockSpec((rows, REC_W), lambda i: (i, 0)), pl.BlockSpec((nb, REC_W), lambda i: (0, 0))],
        out_shape=[jax.ShapeDtypeStruct((n_chunks * rows, REC_W), BF16), jax.ShapeDtypeStruct((nb, REC_W), F32)],
        scratch_shapes=[pltpu.VMEM((rows + tail, REC_W), F32), pltpu.VMEM((rows, REC_W), F32),
                        pltpu.VMEM((rows, REC_W), F32), pltpu.VMEM((nb, REC_W), F32)],
        compiler_params=_cparams("arbitrary"),
        name="rglru",
    )(xr, yr, conv_prev, h_prev, *wts)
    return rec, h_last


def _merge_kernel(n_p, attp_ref, atts_ref, recp_ref, recs_ref, g_ref, x_ref, wa_ref, wr_ref, wo_ref, o_ref,
                  att_scr, att_rows):
    is_p = pl.program_id(0) < n_p

    @pl.when(is_p)
    def _():
        for b in range(NBV):
            for l in range(ATT_W // LANES):
                att_scr[l, pl.ds(b, TM // NBV, stride=NBV), :] = (
                    attp_ref[:, b * ATT_W + l * LANES:b * ATT_W + (l + 1) * LANES].astype(F32))
        for l in range(ATT_W // LANES):
            att_rows[:, l * LANES:(l + 1) * LANES] = att_scr[l].astype(BF16)

    @pl.when(jnp.logical_not(is_p))
    def _():
        att_rows[...] = atts_ref[...]

    att = att_rows[...]
    rec = jnp.where(is_p, recp_ref[...], recs_ref[...])
    ya = jnp.dot(att, wa_ref[...], preferred_element_type=F32)
    yr = jnp.dot(rec, wr_ref[...], preferred_element_type=F32)
    merged = jax.nn.sigmoid(g_ref[:, :D_MODEL]) * ya + jax.nn.sigmoid(g_ref[:, D_MODEL:]) * yr
    o_ref[...] = x_ref[...] + jnp.dot(merged.astype(BF16), wo_ref[...], preferred_element_type=F32)


def _merge(att_p, att_s, rec_p, rec_s, g, x, wa, wr, wo):
    t = x.shape[0]
    n_p = rec_p.shape[0] // TM
    n_s = att_s.shape[0] // TM
    p_spec = lambda w: pl.BlockSpec((TM, w), lambda i: (jnp.minimum(i, n_p - 1), 0))
    s_spec = lambda w: pl.BlockSpec((TM, w), lambda i: (jnp.clip(i - n_p, 0, n_s - 1), 0))
    row = lambda w: pl.BlockSpec((TM, w), lambda i: (i, 0))
    view_spec = pl.BlockSpec((TM // NBV, NBV * ATT_W), lambda i: (jnp.minimum(i, n_p - 1), 0))
    return pl.pallas_call(
        functools.partial(_merge_kernel, n_p),
        grid=(t // TM,),
        in_specs=[view_spec, s_spec(ATT_W), p_spec(REC_W), s_spec(REC_W), row(2 * D_MODEL), row(D_MODEL),
                  _resident(wa.shape), _resident(wr.shape), _resident(wo.shape)],
        out_specs=row(D_MODEL),
        out_shape=jax.ShapeDtypeStruct((t, D_MODEL), F32),
        scratch_shapes=[pltpu.VMEM((ATT_W // LANES, TM, LANES), F32), pltpu.VMEM((TM, ATT_W), BF16)],
        compiler_params=_cparams("parallel"),
        name="merge_out",
    )(att_p, att_s, rec_p, rec_s, g, x, wa, wr, wo)


FF_CHUNK = 512


def _swiglu_mid(hn, wg_ref, wu_ref, mid_ref, width):
    for c in range(0, width, FF_CHUNK):
        gate = jnp.dot(hn, wg_ref[:, c:c + FF_CHUNK], preferred_element_type=F32)
        up = jnp.dot(hn, wu_ref[:, c:c + FF_CHUNK], preferred_element_type=F32)
        mid_ref[:, c:c + FF_CHUNK] = (jax.nn.silu(gate) * up).astype(BF16)


def _ffn_kernel(x_ref, ln_ref, wg_ref, wu_ref, wd_ref, o_ref, mid_ref):
    x = x_ref[...]
    hn = _rms(x, ln_ref[...]).astype(BF16)
    _swiglu_mid(hn, wg_ref, wu_ref, mid_ref, D_FF)
    o_ref[...] = x + jnp.dot(mid_ref[...], wd_ref[...], preferred_element_type=F32)


def _ffn(x, ln, wg, wu, wd):
    t = x.shape[0]
    row = pl.BlockSpec((TM, D_MODEL), lambda i: (i, 0))
    return pl.pallas_call(
        _ffn_kernel,
        grid=(t // TM,),
        in_specs=[row, _resident((1, D_MODEL)), _resident(wg.shape), _resident(wu.shape), _resident(wd.shape)],
        out_specs=row,
        out_shape=jax.ShapeDtypeStruct((t, D_MODEL), F32),
        scratch_shapes=[pltpu.VMEM((TM, D_FF), BF16)],
        compiler_params=_cparams("parallel"),
        name="ffn_dense",
    )(x, ln, wg, wu, wd)


TMG = 512
TOP_K = 2
DISPATCH_ROWS = 512
INFO_E, INFO_RANK, INFO_W = 0, 2, 4


def _route_kernel(x_ref, ln_ref, wr_ref, br_ref, hn_ref, info_ref, cnt_ref, base_scr):
    @pl.when(pl.program_id(0) == 0)
    def _():
        base_scr[...] = jnp.zeros_like(base_scr)

    hn = _rms(x_ref[...], ln_ref[...])
    hn_ref[...] = hn
    lane = lax.broadcasted_iota(jnp.int32, (TM, LANES), 1).astype(F32)
    logits = jnp.dot(hn.astype(BF16), wr_ref[...], preferred_element_type=F32) + br_ref[...]
    m1 = jnp.max(logits, axis=-1, keepdims=True)
    i1 = jnp.min(jnp.where(logits == m1, lane, float(LANES)), axis=-1, keepdims=True)
    rest = jnp.where(lane == i1, -jnp.inf, logits)
    m2 = jnp.max(rest, axis=-1, keepdims=True)
    i2 = jnp.min(jnp.where(rest == m2, lane, float(LANES)), axis=-1, keepdims=True)
    e2 = jnp.exp(m2 - m1)
    den = 1.0 + e2
    hit1 = lane == i1
    hit2 = lane == i2
    onehot = jnp.where(hit1 | hit2, 1.0, 0.0)
    r = lax.broadcasted_iota(jnp.int32, (TM, TM), 0)
    c = lax.broadcasted_iota(jnp.int32, (TM, TM), 1)
    before = jnp.dot(jnp.where(c < r, 1.0, 0.0).astype(BF16), onehot.astype(BF16), preferred_element_type=F32)
    before = before + base_scr[0:1, :]
    rank1 = jnp.sum(jnp.where(hit1, before, 0.0), axis=-1, keepdims=True)
    rank2 = jnp.sum(jnp.where(hit2, before, 0.0), axis=-1, keepdims=True)
    base_scr[0:1, :] = base_scr[0:1, :] + jnp.sum(onehot, axis=0, keepdims=True)
    cnt_ref[...] = base_scr[...]
    info = jnp.zeros((TM, LANES), F32)
    for k, v in ((INFO_E, i1), (INFO_E + 1, i2), (INFO_RANK, rank1), (INFO_RANK + 1, rank2),
                 (INFO_W, 1.0 / den), (INFO_W + 1, e2 / den)):
        info = jnp.where(lane == float(k), v, info)
    info_ref[...] = info


def _route(x, ln, w_router, b_router):
    t = x.shape[0]
    row = lambda w: pl.BlockSpec((TM, w), lambda i: (i, 0))
    return pl.pallas_call(
        _route_kernel,
        grid=(t // TM,),
        in_specs=[row(D_MODEL), _resident((1, D_MODEL)), _resident(w_router.shape), _resident(b_router.shape)],
        out_specs=[row(D_MODEL), row(LANES), pl.BlockSpec((8, LANES), lambda i: (0, 0))],
        out_shape=[jax.ShapeDtypeStruct((t, D_MODEL), F32), jax.ShapeDtypeStruct((t, LANES), F32),
                   jax.ShapeDtypeStruct((8, LANES), F32)],
        scratch_shapes=[pltpu.VMEM((8, LANES), F32)],
        compiler_params=_cparams("arbitrary"),
        name="moe_route",
    )(x, ln, w_router, b_router)


def _dispatch_kernel(pos_hbm, hn_ref, xs_in, xs_out, pos_smem, idx_sem, row_sem):
    del xs_in
    i = pl.program_id(0)
    idx_copy = pltpu.make_async_copy(pos_hbm.at[i], pos_smem, idx_sem)
    idx_copy.start()
    idx_copy.wait()

    def issue(tk, carry):
        for k in range(TOP_K):
            pltpu.make_async_copy(hn_ref.at[pl.ds(tk, 1)], xs_out.at[pl.ds(pos_smem[TOP_K * tk + k], 1)],
                                  row_sem).start(priority=k)
        return carry

    lax.fori_loop(0, DISPATCH_ROWS, issue, 0, unroll=8)
    for k in range(TOP_K):
        pltpu.make_async_copy(hn_ref, xs_out.at[pl.ds(0, DISPATCH_ROWS)], row_sem).wait()


def _dispatch(pos, hn, n_rows):
    t = hn.shape[0]
    steps = t // DISPATCH_ROWS
    xs0 = jnp.zeros((n_rows, D_MODEL), F32)
    return pl.pallas_call(
        _dispatch_kernel,
        grid=(steps,),
        in_specs=[pl.BlockSpec(memory_space=pl.ANY), pl.BlockSpec((DISPATCH_ROWS, D_MODEL), lambda i: (i, 0)),
                  pl.BlockSpec(memory_space=pl.ANY)],
        out_specs=pl.BlockSpec(memory_space=pl.ANY),
        out_shape=jax.ShapeDtypeStruct((n_rows, D_MODEL), F32),
        scratch_shapes=[pltpu.SMEM((TOP_K * DISPATCH_ROWS,), jnp.int32), pltpu.SemaphoreType.DMA,
                        pltpu.SemaphoreType.DMA],
        input_output_aliases={2: 0},
        compiler_params=_cparams("arbitrary"),
        name="moe_dispatch",
    )(pos.reshape(steps, TOP_K * DISPATCH_ROWS), hn, xs0)


def _expert_kernel(te_ref, nu_ref, xs_ref, wg_ref, wu_ref, wd_ref, y_ref, mid_ref):
    del te_ref

    @pl.when(pl.program_id(0) < nu_ref[0])
    def _():
        _swiglu_mid(xs_ref[...].astype(BF16), wg_ref, wu_ref, mid_ref, EXPERT_FF)
        y_ref[...] = jnp.dot(mid_ref[...], wd_ref[...], preferred_element_type=F32)

    @pl.when(pl.program_id(0) >= nu_ref[0])
    def _():
        y_ref[...] = jnp.zeros_like(y_ref)


def _experts(tile_expert, n_used, xs, wg, wu, wd):
    n_rows = xs.shape[0]
    row = pl.BlockSpec((TMG, D_MODEL), lambda i, te, nu: (i, 0))
    ex = lambda a, b: pl.BlockSpec((None, a, b), lambda i, te, nu: (te[i], 0, 0))
    return pl.pallas_call(
        _expert_kernel,
        grid_spec=pltpu.PrefetchScalarGridSpec(
            num_scalar_prefetch=2,
            grid=(n_rows // TMG,),
            in_specs=[row, ex(D_MODEL, EXPERT_FF), ex(D_MODEL, EXPERT_FF), ex(EXPERT_FF, D_MODEL)],
            out_specs=row,
            scratch_shapes=[pltpu.VMEM((TMG, EXPERT_FF), BF16)]),
        out_shape=jax.ShapeDtypeStruct((n_rows, D_MODEL), F32),
        compiler_params=_cparams("arbitrary"),
        name="moe_experts",
    )(tile_expert, n_used, xs, wg, wu, wd)


def _combine_kernel(final, pos_hbm, x_ref, info_ref, ln_ref, y_hbm, o_ref, pos_smem, buf, idx_sem, row_sem):
    i = pl.program_id(0)
    n = pl.num_programs(0)

    def gather(tile, slot):
        idx_copy = pltpu.make_async_copy(pos_hbm.at[tile], pos_smem.at[slot], idx_sem)
        idx_copy.start()
        idx_copy.wait()

        def issue(tk, carry):
            for k in range(TOP_K):
                pltpu.make_async_copy(y_hbm.at[pl.ds(pos_smem[slot, TOP_K * tk + k], 1)],
                                      buf.at[slot, k, pl.ds(tk, 1)], row_sem.at[slot]).start(priority=k)
            return carry

        lax.fori_loop(0, TM, issue, 0, unroll=8)

    @pl.when(i == 0)
    def _():
        gather(0, 0)

    @pl.when(i + 1 < n)
    def _():
        gather(i + 1, (i + 1) % 2)

    slot = i % 2
    for k in range(TOP_K):
        pltpu.make_async_copy(y_hbm.at[pl.ds(0, TM)], buf.at[slot, k], row_sem.at[slot]).wait()
    lane = lax.broadcasted_iota(jnp.int32, (TM, LANES), 1)
    info = info_ref[...]
    out = x_ref[...]
    for k in range(TOP_K):
        w_k = jnp.sum(jnp.where(lane == INFO_W + k, info, 0.0), axis=-1, keepdims=True)
        out = out + w_k * buf[slot, k]
    o_ref[...] = _rms(out, ln_ref[...]) if final else out


def _combine(pos, x, info, y, ln_final):
    t = x.shape[0]
    steps = t // TM
    row = lambda w: pl.BlockSpec((TM, w), lambda i: (i, 0))
    final = ln_final is not None
    ln = ln_final if final else jnp.ones((1, D_MODEL), F32)
    return pl.pallas_call(
        functools.partial(_combine_kernel, final),
        grid=(steps,),
        in_specs=[pl.BlockSpec(memory_space=pl.ANY), row(D_MODEL), row(LANES), _resident((1, D_MODEL)),
                  pl.BlockSpec(memory_space=pl.ANY)],
        out_specs=row(D_MODEL),
        out_shape=jax.ShapeDtypeStruct((t, D_MODEL), F32),
        scratch_shapes=[pltpu.SMEM((2, TOP_K * TM), jnp.int32), pltpu.VMEM((2, TOP_K, TM, D_MODEL), F32),
                        pltpu.SemaphoreType.DMA, pltpu.SemaphoreType.DMA((2,))],
        compiler_params=_cparams("arbitrary"),
        name="moe_combine",
    )(pos.reshape(steps, TOP_K * TM), x, info, ln, y)


def _moe(x, ln, w_router, b_router, wg, wu, wd, ln_final):
    t = x.shape[0]
    hn, info, cnt = _route(x, ln, w_router, b_router)
    counts = cnt[0, :N_EXPERTS].astype(jnp.int32)
    tiles = (counts + TMG - 1) // TMG
    tile_end = jnp.cumsum(tiles)
    offsets = (tile_end - tiles) * TMG
    n_tiles = TOP_K * t // TMG + N_EXPERTS
    tile_expert = jnp.minimum(jnp.sum(jnp.arange(n_tiles)[:, None] >= tile_end[None, :], axis=1),
                              N_EXPERTS - 1).astype(jnp.int32)
    n_used = tile_end[N_EXPERTS - 1:].astype(jnp.int32)
    e = info[:, INFO_E:INFO_E + TOP_K].astype(jnp.int32)
    pos = (jnp.sum(jnp.where(e[..., None] == jnp.arange(N_EXPERTS), offsets, 0), axis=-1)
           + info[:, INFO_RANK:INFO_RANK + TOP_K].astype(jnp.int32)).reshape(-1)
    xs = _dispatch(pos, hn, n_tiles * TMG)
    y = _experts(tile_expert, n_used, xs, wg, wu, wd)
    return _combine(pos, x, info, y, ln_final)


def _final_norm_kernel(x_ref, ln_ref, o_ref):
    o_ref[...] = _rms(x_ref[...], ln_ref[...])


def _final_norm(x, ln):
    t = x.shape[0]
    row = pl.BlockSpec((TM, D_MODEL), lambda i: (i, 0))
    return pl.pallas_call(
        _final_norm_kernel,
        grid=(t // TM,),
        in_specs=[row, _resident((1, D_MODEL))],
        out_specs=row,
        out_shape=jax.ShapeDtypeStruct((t, D_MODEL), F32),
        compiler_params=_cparams("parallel"),
        name="final_norm",
    )(x, ln)


def _gate_weights(w_a, w_i):
    half_blocks = REC_BLOCKS // 2
    eye = jnp.eye(half_blocks, dtype=F32)

    def dense(w):
        return (eye[:, None, :, None] * w[:, :, None, :]).reshape(REC_HALF, REC_HALF)

    halves = [jnp.concatenate([dense(w_a[c * half_blocks:(c + 1) * half_blocks]),
                               dense(w_i[c * half_blocks:(c + 1) * half_blocks])], axis=1) for c in range(2)]
    return jnp.stack(halves).astype(BF16)


def kernel(x_prompt, x_sample, cache_k, cache_v, state_conv, state_rglru, meta_tokens, ln_mix, w_in, attn_sinks,
           w_conv, b_conv, w_rg_a, b_rg_a, w_rg_i, b_rg_i, rg_lambda, w_att_proj, w_rec_proj, w_out, ln_ffn,
           w_ffn_gate, w_ffn_up, w_ffn_down, w_router, b_router, w_exp_gate, w_exp_up, w_exp_down, ln_final):
    bp, seq, _ = x_prompt.shape
    db, ds, _ = x_sample.shape
    depth = w_in.shape[0]
    pad = (-(N_META + seq)) % BLOCK
    lp = pad + N_META + seq
    tp, ts = lp * bp, ds * db
    assert tp % TM == 0 and ts % TM == 0 and db % SAMPLE_BB == 0 and ds >= CONV_W - 1
    assert seq >= WINDOW and bp & (bp - 1) == 0 and db & (db - 1) == 0 and ds & (ds - 1) == 0
    assert lp % RG_TCN == 0 and tp % ts == 0 and bp == NBV

    meta = jnp.broadcast_to(meta_tokens[:, None, :], (N_META, bp, D_MODEL))
    xp = jnp.concatenate([jnp.zeros((pad, bp, D_MODEL), F32), meta, jnp.transpose(x_prompt, (1, 0, 2))], axis=0)
    x = jnp.concatenate([xp.reshape(tp, D_MODEL), jnp.transpose(x_sample, (1, 0, 2)).reshape(ts, D_MODEL)], axis=0)

    zeros_conv = jnp.zeros(((CONV_W - 1) * bp, REC_W), F32)
    zeros_h = jnp.zeros((bp, REC_W), F32)
    outs = {k: [] for k in ("kp", "vp", "cp", "hp", "ks", "vs", "cs", "hs")}

    for l in range(depth):
        q2, kv2, xr, yr, g = _in_proj(x, ln_mix[l].reshape(1, D_MODEL), w_in[l].astype(BF16))

        att_p = _attn_prompt(q2, kv2, attn_sinks[l], bp, lp, pad)
        att_s, nk, nv = _attn_sample(q2[lp:].reshape(ts, ATT_W), kv2[lp:].reshape(ts, 2 * KV_W),
                                     cache_k[l].reshape(db, WINDOW, KV_W),
                                     cache_v[l].reshape(db, WINDOW, KV_W), attn_sinks[l], db, ds)

        rg_w = (w_conv[l], b_conv[l].reshape(1, REC_W), _gate_weights(w_rg_a[l], w_rg_i[l]),
                b_rg_a[l].reshape(1, REC_W), b_rg_i[l].reshape(1, REC_W), rg_lambda[l].reshape(1, REC_W))
        rec_p, h_p = _rglru(xr, yr, 0, lp // RG_TCN, bp, RG_TCN, -pad, zeros_conv, zeros_h, rg_w)
        conv_s = jnp.transpose(state_conv[l], (1, 0, 2)).reshape((CONV_W - 1) * db, REC_W)
        rec_s, h_s = _rglru(xr, yr, tp // ts, 1, db, ds, PAST_LEN, conv_s, state_rglru[l], rg_w)

        x = _merge(att_p, att_s, rec_p, rec_s, g, x, w_att_proj[l].astype(BF16), w_rec_proj[l].astype(BF16),
                   w_out[l].astype(BF16))

        j = l // 2
        ln_f = ln_ffn[l].reshape(1, D_MODEL)
        if l % 2 == 0:
            x = _ffn(x, ln_f, w_ffn_gate[j].astype(BF16), w_ffn_up[j].astype(BF16), w_ffn_down[j].astype(BF16))
        else:
            w_r = jnp.zeros((D_MODEL, LANES), F32).at[:, :N_EXPERTS].set(w_router[j]).astype(BF16)
            b_r = jnp.full((1, LANES), ROUTER_PAD_BIAS, F32).at[0, :N_EXPERTS].set(b_router[j])
            x = _moe(x, ln_f, w_r, b_r, w_exp_gate[j].astype(BF16), w_exp_up[j].astype(BF16),
                     w_exp_down[j].astype(BF16), ln_final.reshape(1, D_MODEL) if l == depth - 1 else None)

        kv_p = kv2[lp - WINDOW:lp].reshape(WINDOW, bp, 2 * KV_W)
        outs["kp"].append(jnp.transpose(kv_p[..., :KV_W], (1, 0, 2)).reshape(bp, WINDOW, N_KV_HEADS, HEAD_DIM))
        outs["vp"].append(jnp.transpose(kv_p[..., KV_W:], (1, 0, 2)).reshape(bp, WINDOW, N_KV_HEADS, HEAD_DIM))
        outs["cp"].append(jnp.transpose(xr[tp - (CONV_W - 1) * bp:tp].reshape(CONV_W - 1, bp, REC_W), (1, 0, 2)))
        outs["hp"].append(h_p)
        outs["ks"].append(nk.reshape(db, WINDOW, N_KV_HEADS, HEAD_DIM))
        outs["vs"].append(nv.reshape(db, WINDOW, N_KV_HEADS, HEAD_DIM))
        outs["cs"].append(jnp.transpose(xr[tp + ts - (CONV_W - 1) * db:].reshape(CONV_W - 1, db, REC_W), (1, 0, 2)))
        outs["hs"].append(h_s)

    y = x if (depth - 1) % 2 == 1 else _final_norm(x, ln_final.reshape(1, D_MODEL))
    y_prompt = jnp.transpose(y[(pad + N_META) * bp:tp].reshape(seq, bp, D_MODEL), (1, 0, 2))
    y_sample = jnp.transpose(y[tp:].reshape(ds, db, D_MODEL), (1, 0, 2))
    st = {k: jnp.stack(v) for k, v in outs.items()}
    return (y_prompt, y_sample, st["kp"], st["vp"], st["cp"], st["hp"], st["ks"], st["vs"], st["cs"], st["hs"])
```

```python
import functools

import jax
import jax.numpy as jnp
from jax import lax
from jax.experimental import pallas as pl
from jax.experimental.pallas import tpu as pltpu

F32 = jnp.float32
BF16 = jnp.bfloat16

D_MODEL = 1024
N_META = 16
N_HEADS = 16
N_KV_HEADS = 2
HEAD_DIM = D_MODEL // N_HEADS
GROUP = N_HEADS // N_KV_HEADS
ATT_W = N_HEADS * HEAD_DIM
KV_W = N_KV_HEADS * HEAD_DIM
WINDOW = 128
BLOCK = 128
ATT_SCALE = HEAD_DIM ** -0.5
REC_W = 5 * D_MODEL // 4
REC_BLOCKS = 16
REC_BW = REC_W // REC_BLOCKS
REC_HALF = REC_W // 2
CONV_W = 4
LRU_C = 8.0
D_FF = 3 * D_MODEL
N_EXPERTS = 8
EXPERT_FF = 3 * D_MODEL // 2
PAST_LEN = 8192
RMS_EPS = 1e-6
IN_OFFS = (0, ATT_W, ATT_W + 2 * KV_W, ATT_W + 2 * KV_W + REC_W, ATT_W + 2 * KV_W + 2 * REC_W,
           ATT_W + 2 * KV_W + 2 * REC_W + 2 * D_MODEL)
LANES = 128
VMEM_LIMIT = 52 * 1024 * 1024

TM = 512
ROUTER_PAD_BIAS = -1e30


def _cparams(*sem):
    return pltpu.CompilerParams(dimension_semantics=sem, vmem_limit_bytes=VMEM_LIMIT)


def _resident(shape):
    nd = len(shape)
    return pl.BlockSpec(shape, lambda *_: (0,) * nd, pipeline_mode=pl.Buffered(1))


def _rms(x, g):
    return x * lax.rsqrt(jnp.mean(x * x, axis=-1, keepdims=True) + RMS_EPS) * g


SQRT_GUARD = 1e-30


def _sigmoid(x):
    return 0.5 * jnp.tanh(0.5 * x) + 0.5


NBV = 8


def _to_col_view(val, scr, out_ref, dtype):
    rows, width = val.shape
    for l in range(width // LANES):
        scr[l] = val[:, l * LANES:(l + 1) * LANES]
    for b in range(NBV):
        for l in range(width // LANES):
            out_ref[:, b * width + l * LANES:b * width + (l + 1) * LANES] = (
                scr[l, pl.ds(b, rows // NBV, stride=NBV), :].astype(dtype))


def _in_proj_kernel(x_ref, ln_ref, w_ref, q_ref, kv_ref, xr_ref, yr_ref, g_ref, q_scr, kv_scr):
    hn = _rms(x_ref[...], ln_ref[...]).astype(BF16)

    def proj(a, b):
        return jnp.dot(hn, w_ref[:, a:b], preferred_element_type=F32)

    _to_col_view(proj(IN_OFFS[0], IN_OFFS[1]) * ATT_SCALE, q_scr, q_ref, BF16)
    _to_col_view(proj(IN_OFFS[1], IN_OFFS[2]), kv_scr, kv_ref, F32)
    xr_ref[...] = proj(IN_OFFS[2], IN_OFFS[3])
    yr_ref[...] = proj(IN_OFFS[3], IN_OFFS[4])
    g_ref[...] = proj(IN_OFFS[4], IN_OFFS[5])


def _in_proj(x, ln, w_in):
    t = x.shape[0]
    row = lambda w: pl.BlockSpec((TM, w), lambda i: (i, 0))
    view = lambda w: pl.BlockSpec((TM // NBV, NBV * w), lambda i: (i, 0))
    return pl.pallas_call(
        _in_proj_kernel,
        grid=(t // TM,),
        in_specs=[row(D_MODEL), _resident((1, D_MODEL)), _resident(w_in.shape)],
        out_specs=[view(ATT_W), view(2 * KV_W), row(REC_W), row(REC_W), row(2 * D_MODEL)],
        out_shape=[jax.ShapeDtypeStruct((t // NBV, NBV * ATT_W), BF16),
                   jax.ShapeDtypeStruct((t // NBV, NBV * 2 * KV_W), F32),
                   jax.ShapeDtypeStruct((t, REC_W), F32), jax.ShapeDtypeStruct((t, REC_W), F32),
                   jax.ShapeDtypeStruct((t, 2 * D_MODEL), F32)],
        scratch_shapes=[pltpu.VMEM((ATT_W // LANES, TM, LANES), F32), pltpu.VMEM((2 * KV_W // LANES, TM, LANES), F32)],
        compiler_params=_cparams("parallel"),
        name="in_proj",
    )(x, ln, w_in)


def _sink_softmax(s, mask, sk):
    s = jnp.where(mask, s, -jnp.inf)
    m = jnp.maximum(jnp.max(s, axis=-1, keepdims=True), sk)
    p = jnp.exp(s - m)
    return p / (jnp.sum(p, axis=-1, keepdims=True) + jnp.exp(sk - m))


def _attn_prompt_kernel(pad, sink_ref, q_ref, kvp_ref, kvc_ref, o_ref):
    n = pl.program_id(1)
    q = q_ref[...]
    kv = jnp.concatenate([kvp_ref[...], kvc_ref[...]], axis=0).astype(BF16)
    i = lax.broadcasted_iota(jnp.int32, (BLOCK, 2 * BLOCK), 0)
    j = lax.broadcasted_iota(jnp.int32, (BLOCK, 2 * BLOCK), 1)
    mask = (j >= i) & (j <= i + WINDOW) & (j >= pad + BLOCK - n * BLOCK)
    outs = []
    for h in range(N_HEADS):
        kvh = h // GROUP
        qh = q[:, h * HEAD_DIM:(h + 1) * HEAD_DIM]
        k = kv[:, kvh * HEAD_DIM:(kvh + 1) * HEAD_DIM]
        v = kv[:, KV_W + kvh * HEAD_DIM:KV_W + (kvh + 1) * HEAD_DIM]
        s = lax.dot_general(qh, k, (((1,), (1,)), ((), ())), preferred_element_type=F32)
        p = _sink_softmax(s, mask, sink_ref[h])
        outs.append(jnp.dot(p.astype(BF16), v, preferred_element_type=F32))
    o_ref[...] = jnp.concatenate(outs, axis=1).astype(BF16)


def _attn_prompt(q2, kv2, sinks, nb, lp, pad):
    return pl.pallas_call(
        functools.partial(_attn_prompt_kernel, pad),
        grid=(nb, lp // BLOCK),
        in_specs=[pl.BlockSpec(memory_space=pltpu.SMEM),
                  pl.BlockSpec((BLOCK, ATT_W), lambda b, n: (n, b)),
                  pl.BlockSpec((BLOCK, 2 * KV_W), lambda b, n: (jnp.maximum(n - 1, 0), b)),
                  pl.BlockSpec((BLOCK, 2 * KV_W), lambda b, n: (n, b))],
        out_specs=pl.BlockSpec((BLOCK, ATT_W), lambda b, n: (n, b)),
        out_shape=jax.ShapeDtypeStruct((lp, nb * ATT_W), BF16),
        compiler_params=_cparams("parallel", "parallel"),
        name="attn_prompt",
    )(sinks, q2, kv2, kv2)


SAMPLE_BB = 8


def _attn_sample_kernel(ds, sink_ref, q_ref, kvn_ref, ck_ref, cv_ref, o_ref, nk_ref, nv_ref):
    rows = GROUP * ds
    r = lax.broadcasted_iota(jnp.int32, (rows, WINDOW + ds), 0)
    j = lax.broadcasted_iota(jnp.int32, (rows, WINDOW + ds), 1)
    tq = r & (ds - 1)
    mask = (j >= tq) & (j <= tq + WINDOW)
    for bl in range(SAMPLE_BB):
        qb = q_ref[:, bl * ATT_W:(bl + 1) * ATT_W].astype(F32)
        kvb = kvn_ref[:, bl * 2 * KV_W:(bl + 1) * 2 * KV_W]
        ck = ck_ref[bl]
        cv = cv_ref[bl]
        nk_ref[bl] = jnp.concatenate([ck[ds:], kvb[:, :KV_W]], axis=0)
        nv_ref[bl] = jnp.concatenate([cv[ds:], kvb[:, KV_W:]], axis=0)
        pieces = []
        for kvh in range(N_KV_HEADS):
            lo, hi = kvh * HEAD_DIM, (kvh + 1) * HEAD_DIM
            k = jnp.concatenate([ck[:, lo:hi], kvb[:, lo:hi]], axis=0).astype(BF16)
            v = jnp.concatenate([cv[:, lo:hi], kvb[:, KV_W + lo:KV_W + hi]], axis=0).astype(BF16)
            qg = jnp.concatenate(
                [qb[:, (kvh * GROUP + g) * HEAD_DIM:(kvh * GROUP + g + 1) * HEAD_DIM] for g in range(GROUP)],
                axis=0).astype(BF16)
            s = lax.dot_general(qg, k, (((1,), (1,)), ((), ())), preferred_element_type=F32)
            p = _sink_softmax(s, mask, sink_ref[kvh][:, 0:1])
            o = jnp.dot(p.astype(BF16), v, preferred_element_type=F32)
            pieces += [o[g * ds:(g + 1) * ds] for g in range(GROUP)]
        o_ref[:, bl * ATT_W:(bl + 1) * ATT_W] = jnp.concatenate(pieces, axis=1).astype(BF16)


def _attn_sample(q, kv, cache_k, cache_v, sinks, nb, ds):
    q2 = q.reshape(ds, nb * ATT_W)
    kv2 = kv.reshape(ds, nb * 2 * KV_W)
    sink_rows = jnp.broadcast_to(
        jnp.repeat(sinks.reshape(N_KV_HEADS, GROUP), ds, axis=1)[:, :, None], (N_KV_HEADS, GROUP * ds, LANES))
    cache_spec = pl.BlockSpec((SAMPLE_BB, WINDOW, KV_W), lambda i: (i, 0, 0))
    att, nk, nv = pl.pallas_call(
        functools.partial(_attn_sample_kernel, ds),
        grid=(nb // SAMPLE_BB,),
        in_specs=[_resident(sink_rows.shape),
                  pl.BlockSpec((ds, SAMPLE_BB * ATT_W), lambda i: (0, i)),
                  pl.BlockSpec((ds, SAMPLE_BB * 2 * KV_W), lambda i: (0, i)),
                  cache_spec, cache_spec],
        out_specs=[pl.BlockSpec((ds, SAMPLE_BB * ATT_W), lambda i: (0, i)), cache_spec, cache_spec],
        out_shape=[jax.ShapeDtypeStruct((ds, nb * ATT_W), BF16),
                   jax.ShapeDtypeStruct(cache_k.shape, F32), jax.ShapeDtypeStruct(cache_v.shape, F32)],
        compiler_params=_cparams("parallel"),
        name="attn_sample",
    )(sink_rows, q2, kv2, cache_k, cache_v)
    return att.reshape(ds * nb, ATT_W), nk, nv


RG_ROWS = 256
RG_TCN = 64


def _rglru_kernel(nb, tcn, pos0, xr_ref, yr_ref, cprev_ref, hprev_ref, wc_ref, bc_ref, wg_ref, ba_ref, bi_ref,
                  lam_ref, rec_ref, hlast_ref, xbuf, a_scr, b_scr, hc):
    step = pl.program_id(0)
    rows = tcn * nb
    tail = (CONV_W - 1) * nb

    @pl.when(step == 0)
    def _():
        xbuf[0:tail, :] = cprev_ref[...]
        hc[...] = hprev_ref[...]

    @pl.when(step > 0)
    def _():
        xbuf[0:tail, :] = xbuf[rows:rows + tail, :]

    def row_pos(r0, n):
        t_row = lax.shift_right_logical(r0 + lax.broadcasted_iota(jnp.int32, (n, 1), 0), nb.bit_length() - 1)
        return pos0 + step * tcn + t_row

    sub = min(RG_ROWS, rows)
    for r0 in range(0, rows, sub):
        x = xr_ref[r0:r0 + sub, :]
        if pos0 < 0:
            x = jnp.where(row_pos(r0, sub) >= 0, x, 0.0)
        xbuf[tail + r0:tail + r0 + sub, :] = x

    lam = lam_ref[...]
    neg_c_softplus = -LRU_C * (jnp.maximum(-lam, 0.0) + jnp.log1p(jnp.exp(-jnp.abs(lam))))

    for r0 in range(0, rows, sub):
        xc = bc_ref[...]
        for jj in range(CONV_W):
            xc = xc + xbuf[r0 + jj * nb:r0 + jj * nb + sub, :] * wc_ref[jj:jj + 1, :]
        xcb = xc.astype(BF16)
        pos = row_pos(r0, sub)
        for c in range(2):
            lo, hi = c * REC_HALF, (c + 1) * REC_HALF
            gates = jnp.dot(xcb[:, lo:hi], wg_ref[c], preferred_element_type=F32)
            r = _sigmoid(gates[:, :REC_HALF] + ba_ref[:, lo:hi])
            ig = _sigmoid(gates[:, REC_HALF:] + bi_ref[:, lo:hi])
            a = jnp.exp(r * neg_c_softplus[:, lo:hi])
            u = 1.0 - a * a
            mult = u * lax.rsqrt(jnp.maximum(u, SQRT_GUARD))
            if pos0 <= 0:
                mult = jnp.where(pos == 0, 1.0, mult)
            bt = mult * ig * xc[:, lo:hi]
            if pos0 < 0:
                bt = jnp.where(pos >= 0, bt, 0.0)
            a_scr[r0:r0 + sub, lo:hi] = a
            b_scr[r0:r0 + sub, lo:hi] = bt

    if nb * REC_W <= 16 * 1024:
        def body(t, h):
            rr = pl.multiple_of(t * nb, nb)
            h = a_scr[pl.ds(rr, nb), :] * h + b_scr[pl.ds(rr, nb), :]
            b_scr[pl.ds(rr, nb), :] = h
            return h
        hc[...] = lax.fori_loop(0, tcn, body, hc[...])
    else:
        for t in range(tcn):
            h = a_scr[t * nb:(t + 1) * nb, :] * hc[...] + b_scr[t * nb:(t + 1) * nb, :]
            b_scr[t * nb:(t + 1) * nb, :] = h
            hc[...] = h

    hlast_ref[...] = hc[...]
    for r0 in range(0, rows, sub):
        rec_ref[r0:r0 + sub, :] = (b_scr[r0:r0 + sub, :] * jax.nn.gelu(yr_ref[r0:r0 + sub, :])).astype(BF16)


def _rglru(xr, yr, row_block0, n_chunks, nb, tcn, pos0, conv_prev, h_prev, wts):
    rows = tcn * nb
    tail = (CONV_W - 1) * nb
    blk = pl.BlockSpec((rows, REC_W), lambda i: (row_block0 + i, 0))
    rec, h_last = pl.pallas_call(
        functools.partial(_rglru_kernel, nb, tcn, pos0),
        grid=(n_chunks,),
        in_specs=[blk, blk, _resident(conv_prev.shape), _resident(h_prev.shape)] + [_resident(w.shape) for w in wts],
        out_specs=[pl.BlockSpec((rows, REC_W), lambda i: (i, 0)), pl.BlockSpec((nb, REC_W), lambda i: (0, 0))],
        out_shape=[jax.ShapeDtypeStruct((n_chunks * rows, REC_W), BF16), jax.ShapeDtypeStruct((nb, REC_W), F32)],
        scratch_shapes=[pltpu.VMEM((rows + tail, REC_W), F32), pltpu.VMEM((rows, REC_W), F32),
                        pltpu.VMEM((rows, REC_W), F32), pltpu.VMEM((nb, REC_W), F32)],
        compiler_params=_cparams("arbitrary"),
        name="rglru",
    )(xr, yr, conv_prev, h_prev, *wts)
    return rec, h_last


def _merge_kernel(n_p, attp_ref, atts_ref, recp_ref, recs_ref, g_ref, x_ref, wa_ref, wr_ref, wo_ref, o_ref,
                  att_scr, att_rows):
    is_p = pl.program_id(0) < n_p

    @pl.when(is_p)
    def _():
        for b in range(NBV):
            for l in range(ATT_W // LANES):
                att_scr[l, pl.ds(b, TM // NBV, stride=NBV), :] = (
                    attp_ref[:, b * ATT_W + l * LANES:b * ATT_W + (l + 1) * LANES].astype(F32))
        for l in range(ATT_W // LANES):
            att_rows[:, l * LANES:(l + 1) * LANES] = att_scr[l].astype(BF16)

    @pl.when(jnp.logical_not(is_p))
    def _():
        att_rows[...] = atts_ref[...]

    att = att_rows[...]
    rec = jnp.where(is_p, recp_ref[...], recs_ref[...])
    ya = jnp.dot(att, wa_ref[...], preferred_element_type=F32)
    yr = jnp.dot(rec, wr_ref[...], preferred_element_type=F32)
    merged = _sigmoid(g_ref[:, :D_MODEL]) * ya + _sigmoid(g_ref[:, D_MODEL:]) * yr
    o_ref[...] = x_ref[...] + jnp.dot(merged.astype(BF16), wo_ref[...], preferred_element_type=F32)


def _merge(att_p, att_s, rec_p, rec_s, g, x, wa, wr, wo):
    t = x.shape[0]
    n_p = rec_p.shape[0] // TM
    n_s = att_s.shape[0] // TM
    p_spec = lambda w: pl.BlockSpec((TM, w), lambda i: (jnp.minimum(i, n_p - 1), 0))
    s_spec = lambda w: pl.BlockSpec((TM, w), lambda i: (jnp.clip(i - n_p, 0, n_s - 1), 0))
    row = lambda w: pl.BlockSpec((TM, w), lambda i: (i, 0))
    view_spec = pl.BlockSpec((TM // NBV, NBV * ATT_W), lambda i: (jnp.minimum(i, n_p - 1), 0))
    return pl.pallas_call(
        functools.partial(_merge_kernel, n_p),
        grid=(t // TM,),
        in_specs=[view_spec, s_spec(ATT_W), p_spec(REC_W), s_spec(REC_W), row(2 * D_MODEL), row(D_MODEL),
                  _resident(wa.shape), _resident(wr.shape), _resident(wo.shape)],
        out_specs=row(D_MODEL),
        out_shape=jax.ShapeDtypeStruct((t, D_MODEL), F32),
        scratch_shapes=[pltpu.VMEM((ATT_W // LANES, TM, LANES), F32), pltpu.VMEM((TM, ATT_W), BF16)],
        compiler_params=_cparams("parallel"),
        name="merge_out",
    )(att_p, att_s, rec_p, rec_s, g, x, wa, wr, wo)


FF_CHUNK = 512


def _swiglu_mid(hn, wg_ref, wu_ref, mid_ref, width):
    for c in range(0, width, FF_CHUNK):
        gate = jnp.dot(hn, wg_ref[:, c:c + FF_CHUNK], preferred_element_type=F32)
        up = jnp.dot(hn, wu_ref[:, c:c + FF_CHUNK], preferred_element_type=F32)
        mid_ref[:, c:c + FF_CHUNK] = (jax.nn.silu(gate) * up).astype(BF16)


def _ffn_kernel(x_ref, ln_ref, wg_ref, wu_ref, wd_ref, o_ref, mid_ref):
    x = x_ref[...]
    hn = _rms(x, ln_ref[...]).astype(BF16)
    _swiglu_mid(hn, wg_ref, wu_ref, mid_ref, D_FF)
    o_ref[...] = x + jnp.dot(mid_ref[...], wd_ref[...], preferred_element_type=F32)


def _ffn(x, ln, wg, wu, wd):
    t = x.shape[0]
    row = pl.BlockSpec((TM, D_MODEL), lambda i: (i, 0))
    return pl.pallas_call(
        _ffn_kernel,
        grid=(t // TM,),
        in_specs=[row, _resident((1, D_MODEL)), _resident(wg.shape), _resident(wu.shape), _resident(wd.shape)],
        out_specs=row,
        out_shape=jax.ShapeDtypeStruct((t, D_MODEL), F32),
        scratch_shapes=[pltpu.VMEM((TM, D_FF), BF16)],
        compiler_params=_cparams("parallel"),
        name="ffn_dense",
    )(x, ln, wg, wu, wd)


TMG = 512
TOP_K = 2
DISPATCH_ROWS = 512
INFO_E, INFO_RANK, INFO_W = 0, 2, 4


def _route_kernel(x_ref, ln_ref, wr_ref, br_ref, hn_ref, info_ref, cnt_ref, base_scr):
    @pl.when(pl.program_id(0) == 0)
    def _():
        base_scr[...] = jnp.zeros_like(base_scr)

    hn = _rms(x_ref[...], ln_ref[...])
    hn_ref[...] = hn
    lane = lax.broadcasted_iota(jnp.int32, (TM, LANES), 1).astype(F32)
    logits = jnp.dot(hn.astype(BF16), wr_ref[...], preferred_element_type=F32) + br_ref[...]
    m1 = jnp.max(logits, axis=-1, keepdims=True)
    i1 = jnp.min(jnp.where(logits == m1, lane, float(LANES)), axis=-1, keepdims=True)
    rest = jnp.where(lane == i1, -jnp.inf, logits)
    m2 = jnp.max(rest, axis=-1, keepdims=True)
    i2 = jnp.min(jnp.where(rest == m2, lane, float(LANES)), axis=-1, keepdims=True)
    e2 = jnp.exp(m2 - m1)
    den = 1.0 + e2
    hit1 = lane == i1
    hit2 = lane == i2
    onehot = jnp.where(hit1 | hit2, 1.0, 0.0)
    r = lax.broadcasted_iota(jnp.int32, (TM, TM), 0)
    c = lax.broadcasted_iota(jnp.int32, (TM, TM), 1)
    before = jnp.dot(jnp.where(c < r, 1.0, 0.0).astype(BF16), onehot.astype(BF16), preferred_element_type=F32)
    before = before + base_scr[0:1, :]
    rank1 = jnp.sum(jnp.where(hit1, before, 0.0), axis=-1, keepdims=True)
    rank2 = jnp.sum(jnp.where(hit2, before, 0.0), axis=-1, keepdims=True)
    base_scr[0:1, :] = base_scr[0:1, :] + jnp.sum(onehot, axis=0, keepdims=True)
    cnt_ref[...] = base_scr[...]
    info = jnp.zeros((TM, LANES), F32)
    for k, v in ((INFO_E, i1), (INFO_E + 1, i2), (INFO_RANK, rank1), (INFO_RANK + 1, rank2),
                 (INFO_W, 1.0 / den), (INFO_W + 1, e2 / den)):
        info = jnp.where(lane == float(k), v, info)
    info_ref[...] = info


def _route(x, ln, w_router, b_router):
    t = x.shape[0]
    row = lambda w: pl.BlockSpec((TM, w), lambda i: (i, 0))
    return pl.pallas_call(
        _route_kernel,
        grid=(t // TM,),
        in_specs=[row(D_MODEL), _resident((1, D_MODEL)), _resident(w_router.shape), _resident(b_router.shape)],
        out_specs=[row(D_MODEL), row(LANES), pl.BlockSpec((8, LANES), lambda i: (0, 0))],
        out_shape=[jax.ShapeDtypeStruct((t, D_MODEL), F32), jax.ShapeDtypeStruct((t, LANES), F32),
                   jax.ShapeDtypeStruct((8, LANES), F32)],
        scratch_shapes=[pltpu.VMEM((8, LANES), F32)],
        compiler_params=_cparams("arbitrary"),
        name="moe_route",
    )(x, ln, w_router, b_router)


def _dispatch_kernel(pos_hbm, hn_ref, xs_in, xs_out, pos_smem, idx_sem, row_sem):
    del xs_in
    i = pl.program_id(0)
    idx_copy = pltpu.make_async_copy(pos_hbm.at[i], pos_smem, idx_sem)
    idx_copy.start()
    idx_copy.wait()

    def issue(j, carry):
        base = pl.multiple_of(j * 8, 8)
        for u in range(8):
            for k in range(TOP_K):
                pltpu.make_async_copy(hn_ref.at[pl.ds(base + u, 1)],
                                      xs_out.at[pl.ds(pos_smem[TOP_K * (base + u) + k], 1)],
                                      row_sem).start(priority=k)
        return carry

    lax.fori_loop(0, DISPATCH_ROWS // 8, issue, 0)
    for k in range(TOP_K):
        pltpu.make_async_copy(hn_ref, xs_out.at[pl.ds(0, DISPATCH_ROWS)], row_sem).wait()


def _dispatch(pos, hn, n_rows):
    t = hn.shape[0]
    steps = t // DISPATCH_ROWS
    xs0 = jnp.zeros((n_rows, D_MODEL), F32)
    return pl.pallas_call(
        _dispatch_kernel,
        grid=(steps,),
        in_specs=[pl.BlockSpec(memory_space=pl.ANY), pl.BlockSpec((DISPATCH_ROWS, D_MODEL), lambda i: (i, 0)),
                  pl.BlockSpec(memory_space=pl.ANY)],
        out_specs=pl.BlockSpec(memory_space=pl.ANY),
        out_shape=jax.ShapeDtypeStruct((n_rows, D_MODEL), F32),
        scratch_shapes=[pltpu.SMEM((TOP_K * DISPATCH_ROWS,), jnp.int32), pltpu.SemaphoreType.DMA,
                        pltpu.SemaphoreType.DMA],
        input_output_aliases={2: 0},
        compiler_params=_cparams("arbitrary"),
        name="moe_dispatch",
    )(pos.reshape(steps, TOP_K * DISPATCH_ROWS), hn, xs0)


def _expert_kernel(te_ref, nu_ref, xs_ref, wg_ref, wu_ref, wd_ref, y_ref, mid_ref):
    del te_ref

    @pl.when(pl.program_id(0) < nu_ref[0])
    def _():
        _swiglu_mid(xs_ref[...].astype(BF16), wg_ref, wu_ref, mid_ref, EXPERT_FF)
        y_ref[...] = jnp.dot(mid_ref[...], wd_ref[...], preferred_element_type=F32)

    @pl.when(pl.program_id(0) >= nu_ref[0])
    def _():
        y_ref[...] = jnp.zeros_like(y_ref)


def _experts(tile_expert, n_used, xs, wg, wu, wd):
    n_rows = xs.shape[0]
    row = pl.BlockSpec((TMG, D_MODEL), lambda i, te, nu: (i, 0))
    ex = lambda a, b: pl.BlockSpec((None, a, b), lambda i, te, nu: (te[i], 0, 0))
    return pl.pallas_call(
        _expert_kernel,
        grid_spec=pltpu.PrefetchScalarGridSpec(
            num_scalar_prefetch=2,
            grid=(n_rows // TMG,),
            in_specs=[row, ex(D_MODEL, EXPERT_FF), ex(D_MODEL, EXPERT_FF), ex(EXPERT_FF, D_MODEL)],
            out_specs=row,
            scratch_shapes=[pltpu.VMEM((TMG, EXPERT_FF), BF16)]),
        out_shape=jax.ShapeDtypeStruct((n_rows, D_MODEL), F32),
        compiler_params=_cparams("arbitrary"),
        name="moe_experts",
    )(tile_expert, n_used, xs, wg, wu, wd)


def _combine_kernel(final, pos_hbm, x_ref, info_ref, ln_ref, y_hbm, o_ref, pos_smem, buf, idx_sem, row_sem):
    i = pl.program_id(0)
    n = pl.num_programs(0)

    def gather(tile, slot):
        idx_copy = pltpu.make_async_copy(pos_hbm.at[tile], pos_smem.at[slot], idx_sem)
        idx_copy.start()
        idx_copy.wait()

        def issue(j, carry):
            base = pl.multiple_of(j * 8, 8)
            for u in range(8):
                for k in range(TOP_K):
                    pltpu.make_async_copy(y_hbm.at[pl.ds(pos_smem[slot, TOP_K * (base + u) + k], 1)],
                                          buf.at[slot, k, pl.ds(base + u, 1)], row_sem.at[slot]).start(priority=k)
            return carry

        lax.fori_loop(0, TM // 8, issue, 0)

    @pl.when(i == 0)
    def _():
        gather(0, 0)

    for par in range(2):
        @pl.when((i + 1 < n) & ((i + 1) % 2 == par))
        def _():
            gather(i + 1, par)

    slot = i % 2
    for k in range(TOP_K):
        pltpu.make_async_copy(y_hbm.at[pl.ds(0, TM)], buf.at[slot, k], row_sem.at[slot]).wait()
    lane = lax.broadcasted_iota(jnp.int32, (TM, LANES), 1)
    info = info_ref[...]
    out = x_ref[...]
    for k in range(TOP_K):
        w_k = jnp.sum(jnp.where(lane == INFO_W + k, info, 0.0), axis=-1, keepdims=True)
        out = out + w_k * buf[slot, k]
    o_ref[...] = _rms(out, ln_ref[...]) if final else out


def _combine(pos, x, info, y, ln_final):
    t = x.shape[0]
    steps = t // TM
    row = lambda w: pl.BlockSpec((TM, w), lambda i: (i, 0))
    final = ln_final is not None
    ln = ln_final if final else jnp.ones((1, D_MODEL), F32)
    return pl.pallas_call(
        functools.partial(_combine_kernel, final),
        grid=(steps,),
        in_specs=[pl.BlockSpec(memory_space=pl.ANY), row(D_MODEL), row(LANES), _resident((1, D_MODEL)),
                  pl.BlockSpec(memory_space=pl.ANY)],
        out_specs=row(D_MODEL),
        out_shape=jax.ShapeDtypeStruct((t, D_MODEL), F32),
        scratch_shapes=[pltpu.SMEM((2, TOP_K * TM), jnp.int32), pltpu.VMEM((2, TOP_K, TM, D_MODEL), F32),
                        pltpu.SemaphoreType.DMA, pltpu.SemaphoreType.DMA((2,))],
        compiler_params=_cparams("arbitrary"),
        name="moe_combine",
    )(pos.reshape(steps, TOP_K * TM), x, info, ln, y)


def _moe(x, ln, w_router, b_router, wg, wu, wd, ln_final):
    t = x.shape[0]
    hn, info, cnt = _route(x, ln, w_router, b_router)
    counts = cnt[0, :N_EXPERTS].astype(jnp.int32)
    tiles = (counts + TMG - 1) // TMG
    tile_end = jnp.cumsum(tiles)
    offsets = (tile_end - tiles) * TMG
    n_tiles = TOP_K * t // TMG + N_EXPERTS
    tile_expert = jnp.minimum(jnp.sum(jnp.arange(n_tiles)[:, None] >= tile_end[None, :], axis=1),
                              N_EXPERTS - 1).astype(jnp.int32)
    n_used = tile_end[N_EXPERTS - 1:].astype(jnp.int32)
    e = info[:, INFO_E:INFO_E + TOP_K].astype(jnp.int32)
    pos = (jnp.sum(jnp.where(e[..., None] == jnp.arange(N_EXPERTS), offsets, 0), axis=-1)
           + info[:, INFO_RANK:INFO_RANK + TOP_K].astype(jnp.int32)).reshape(-1)
    xs = _dispatch(pos, hn, n_tiles * TMG)
    y = _experts(tile_expert, n_used, xs, wg, wu, wd)
    return _combine(pos, x, info, y, ln_final)


def _final_norm_kernel(x_ref, ln_ref, o_ref):
    o_ref[...] = _rms(x_ref[...], ln_ref[...])


def _final_norm(x, ln):
    t = x.shape[0]
    row = pl.BlockSpec((TM, D_MODEL), lambda i: (i, 0))
    return pl.pallas_call(
        _final_norm_kernel,
        grid=(t // TM,),
        in_specs=[row, _resident((1, D_MODEL))],
        out_specs=row,
        out_shape=jax.ShapeDtypeStruct((t, D_MODEL), F32),
        compiler_params=_cparams("parallel"),
        name="final_norm",
    )(x, ln)


def _gate_weights(w_a, w_i):
    half_blocks = REC_BLOCKS // 2
    eye = jnp.eye(half_blocks, dtype=F32)

    def dense(w):
        return (eye[:, None, :, None] * w[:, :, None, :]).reshape(REC_HALF, REC_HALF)

    halves = [jnp.concatenate([dense(w_a[c * half_blocks:(c + 1) * half_blocks]),
                               dense(w_i[c * half_blocks:(c + 1) * half_blocks])], axis=1) for c in range(2)]
    return jnp.stack(halves).astype(BF16)


def kernel(x_prompt, x_sample, cache_k, cache_v, state_conv, state_rglru, meta_tokens, ln_mix, w_in, attn_sinks,
           w_conv, b_conv, w_rg_a, b_rg_a, w_rg_i, b_rg_i, rg_lambda, w_att_proj, w_rec_proj, w_out, ln_ffn,
           w_ffn_gate, w_ffn_up, w_ffn_down, w_router, b_router, w_exp_gate, w_exp_up, w_exp_down, ln_final):
    bp, seq, _ = x_prompt.shape
    db, ds, _ = x_sample.shape
    depth = w_in.shape[0]
    pad = (-(N_META + seq)) % BLOCK
    lp = pad + N_META + seq
    tp, ts = lp * bp, ds * db
    assert tp % TM == 0 and ts % TM == 0 and db % SAMPLE_BB == 0 and ds >= CONV_W - 1
    assert seq >= WINDOW and bp & (bp - 1) == 0 and db & (db - 1) == 0 and ds & (ds - 1) == 0
    assert lp % RG_TCN == 0 and tp % ts == 0 and bp == NBV

    meta = jnp.broadcast_to(meta_tokens[:, None, :], (N_META, bp, D_MODEL))
    xp = jnp.concatenate([jnp.zeros((pad, bp, D_MODEL), F32), meta, jnp.transpose(x_prompt, (1, 0, 2))], axis=0)
    x = jnp.concatenate([xp.reshape(tp, D_MODEL), jnp.transpose(x_sample, (1, 0, 2)).reshape(ts, D_MODEL)], axis=0)

    zeros_conv = jnp.zeros(((CONV_W - 1) * bp, REC_W), F32)
    zeros_h = jnp.zeros((bp, REC_W), F32)
    outs = {k: [] for k in ("kp", "vp", "cp", "hp", "ks", "vs", "cs", "hs")}

    for l in range(depth):
        q2, kv2, xr, yr, g = _in_proj(x, ln_mix[l].reshape(1, D_MODEL), w_in[l].astype(BF16))

        att_p = _attn_prompt(q2, kv2, attn_sinks[l], bp, lp, pad)
        att_s, nk, nv = _attn_sample(q2[lp:].reshape(ts, ATT_W), kv2[lp:].reshape(ts, 2 * KV_W),
                                     cache_k[l].reshape(db, WINDOW, KV_W),
                                     cache_v[l].reshape(db, WINDOW, KV_W), attn_sinks[l], db, ds)

        rg_w = (w_conv[l], b_conv[l].reshape(1, REC_W), _gate_weights(w_rg_a[l], w_rg_i[l]),
                b_rg_a[l].reshape(1, REC_W), b_rg_i[l].reshape(1, REC_W), rg_lambda[l].reshape(1, REC_W))
        rec_p, h_p = _rglru(xr, yr, 0, lp // RG_TCN, bp, RG_TCN, -pad, zeros_conv, zeros_h, rg_w)
        conv_s = jnp.transpose(state_conv[l], (1, 0, 2)).reshape((CONV_W - 1) * db, REC_W)
        rec_s, h_s = _rglru(xr, yr, tp // ts, 1, db, ds, PAST_LEN, conv_s, state_rglru[l], rg_w)

        x = _merge(att_p, att_s, rec_p, rec_s, g, x, w_att_proj[l].astype(BF16), w_rec_proj[l].astype(BF16),
                   w_out[l].astype(BF16))

        j = l // 2
        ln_f = ln_ffn[l].reshape(1, D_MODEL)
        if l % 2 == 0:
            x = _ffn(x, ln_f, w_ffn_gate[j].astype(BF16), w_ffn_up[j].astype(BF16), w_ffn_down[j].astype(BF16))
        else:
            w_r = jnp.zeros((D_MODEL, LANES), F32).at[:, :N_EXPERTS].set(w_router[j]).astype(BF16)
            b_r = jnp.full((1, LANES), ROUTER_PAD_BIAS, F32).at[0, :N_EXPERTS].set(b_router[j])
            x = _moe(x, ln_f, w_r, b_r, w_exp_gate[j].astype(BF16), w_exp_up[j].astype(BF16),
                     w_exp_down[j].astype(BF16), ln_final.reshape(1, D_MODEL) if l == depth - 1 else None)

        kv_p = kv2[lp - WINDOW:lp].reshape(WINDOW, bp, 2 * KV_W)
        outs["kp"].append(jnp.transpose(kv_p[..., :KV_W], (1, 0, 2)).reshape(bp, WINDOW, N_KV_HEADS, HEAD_DIM))
        outs["vp"].append(jnp.transpose(kv_p[..., KV_W:], (1, 0, 2)).reshape(bp, WINDOW, N_KV_HEADS, HEAD_DIM))
        outs["cp"].append(jnp.transpose(xr[tp - (CONV_W - 1) * bp:tp].reshape(CONV_W - 1, bp, REC_W), (1, 0, 2)))
        outs["hp"].append(h_p)
        outs["ks"].append(nk.reshape(db, WINDOW, N_KV_HEADS, HEAD_DIM))
        outs["vs"].append(nv.reshape(db, WINDOW, N_KV_HEADS, HEAD_DIM))
        outs["cs"].append(jnp.transpose(xr[tp + ts - (CONV_W - 1) * db:].reshape(CONV_W - 1, db, REC_W), (1, 0, 2)))
        outs["hs"].append(h_s)

    y = x if (depth - 1) % 2 == 1 else _final_norm(x, ln_final.reshape(1, D_MODEL))
    y_prompt = jnp.transpose(y[(pad + N_META) * bp:tp].reshape(seq, bp, D_MODEL), (1, 0, 2))
    y_sample = jnp.transpose(y[tp:].reshape(ds, db, D_MODEL), (1, 0, 2))
    st = {k: jnp.stack(v) for k, v in outs.items()}
    return (y_prompt, y_sample, st["kp"], st["vp"], st["cp"], st["hp"], st["ks"], st["vs"], st["cs"], st["hs"])
```

```python
import functools

import jax
import jax.numpy as jnp
from jax import lax
from jax.experimental import pallas as pl
from jax.experimental.pallas import tpu as pltpu

F32 = jnp.float32
BF16 = jnp.bfloat16

D_MODEL = 1024
N_META = 16
N_HEADS = 16
N_KV_HEADS = 2
HEAD_DIM = D_MODEL // N_HEADS
GROUP = N_HEADS // N_KV_HEADS
ATT_W = N_HEADS * HEAD_DIM
KV_W = N_KV_HEADS * HEAD_DIM
WINDOW = 128
BLOCK = 128
ATT_SCALE = HEAD_DIM ** -0.5
REC_W = 5 * D_MODEL // 4
REC_BLOCKS = 16
REC_BW = REC_W // REC_BLOCKS
REC_HALF = REC_W // 2
CONV_W = 4
LRU_C = 8.0
D_FF = 3 * D_MODEL
N_EXPERTS = 8
EXPERT_FF = 3 * D_MODEL // 2
PAST_LEN = 8192
RMS_EPS = 1e-6
IN_OFFS = (0, ATT_W, ATT_W + 2 * KV_W, ATT_W + 2 * KV_W + REC_W, ATT_W + 2 * KV_W + 2 * REC_W,
           ATT_W + 2 * KV_W + 2 * REC_W + 2 * D_MODEL)
LANES = 128
VMEM_LIMIT = 52 * 1024 * 1024

TM = 512
ROUTER_PAD_BIAS = -1e30


def _cparams(*sem):
    return pltpu.CompilerParams(dimension_semantics=sem, vmem_limit_bytes=VMEM_LIMIT)


def _resident(shape):
    nd = len(shape)
    return pl.BlockSpec(shape, lambda *_: (0,) * nd, pipeline_mode=pl.Buffered(1))


def _rms(x, g):
    return x * lax.rsqrt(jnp.mean(x * x, axis=-1, keepdims=True) + RMS_EPS) * g


SQRT_GUARD = 1e-30


def _sigmoid(x):
    return 0.5 * jnp.tanh(0.5 * x) + 0.5


NBV = 8


def _to_col_view(val, scr, out_ref, dtype):
    rows, width = val.shape
    for l in range(width // LANES):
        scr[l] = val[:, l * LANES:(l + 1) * LANES]
    for b in range(NBV):
        for l in range(width // LANES):
            out_ref[:, b * width + l * LANES:b * width + (l + 1) * LANES] = (
                scr[l, pl.ds(b, rows // NBV, stride=NBV), :].astype(dtype))


def _in_proj_kernel(x_ref, ln_ref, w_ref, q_ref, kv_ref, xr_ref, yr_ref, g_ref, q_scr, kv_scr):
    hn = _rms(x_ref[...], ln_ref[...]).astype(BF16)

    def proj(a, b):
        return jnp.dot(hn, w_ref[:, a:b], preferred_element_type=F32)

    _to_col_view(proj(IN_OFFS[0], IN_OFFS[1]) * ATT_SCALE, q_scr, q_ref, BF16)
    _to_col_view(proj(IN_OFFS[1], IN_OFFS[2]), kv_scr, kv_ref, F32)
    xr_ref[...] = proj(IN_OFFS[2], IN_OFFS[3])
    yr_ref[...] = proj(IN_OFFS[3], IN_OFFS[4])
    g_ref[...] = proj(IN_OFFS[4], IN_OFFS[5])


def _in_proj(x, ln, w_in):
    t = x.shape[0]
    row = lambda w: pl.BlockSpec((TM, w), lambda i: (i, 0))
    view = lambda w: pl.BlockSpec((TM // NBV, NBV * w), lambda i: (i, 0))
    return pl.pallas_call(
        _in_proj_kernel,
        grid=(t // TM,),
        in_specs=[row(D_MODEL), _resident((1, D_MODEL)), _resident(w_in.shape)],
        out_specs=[view(ATT_W), view(2 * KV_W), row(REC_W), row(REC_W), row(2 * D_MODEL)],
        out_shape=[jax.ShapeDtypeStruct((t // NBV, NBV * ATT_W), BF16),
                   jax.ShapeDtypeStruct((t // NBV, NBV * 2 * KV_W), F32),
                   jax.ShapeDtypeStruct((t, REC_W), F32), jax.ShapeDtypeStruct((t, REC_W), F32),
                   jax.ShapeDtypeStruct((t, 2 * D_MODEL), F32)],
        scratch_shapes=[pltpu.VMEM((ATT_W // LANES, TM, LANES), F32), pltpu.VMEM((2 * KV_W // LANES, TM, LANES), F32)],
        compiler_params=_cparams("parallel"),
        name="in_proj",
    )(x, ln, w_in)


def _sink_softmax(s, mask, sk):
    s = jnp.where(mask, s, -jnp.inf)
    m = jnp.maximum(jnp.max(s, axis=-1, keepdims=True), sk)
    p = jnp.exp(s - m)
    return p / (jnp.sum(p, axis=-1, keepdims=True) + jnp.exp(sk - m))


def _attn_prompt_kernel(pad, sink_ref, q_ref, kvp_ref, kvc_ref, o_ref):
    n = pl.program_id(1)
    q = q_ref[...]
    kv = jnp.concatenate([kvp_ref[...], kvc_ref[...]], axis=0).astype(BF16)
    i = lax.broadcasted_iota(jnp.int32, (BLOCK, 2 * BLOCK), 0)
    j = lax.broadcasted_iota(jnp.int32, (BLOCK, 2 * BLOCK), 1)
    mask = (j >= i) & (j <= i + WINDOW) & (j >= pad + BLOCK - n * BLOCK)
    outs = []
    for h in range(N_HEADS):
        kvh = h // GROUP
        qh = q[:, h * HEAD_DIM:(h + 1) * HEAD_DIM]
        k = kv[:, kvh * HEAD_DIM:(kvh + 1) * HEAD_DIM]
        v = kv[:, KV_W + kvh * HEAD_DIM:KV_W + (kvh + 1) * HEAD_DIM]
        s = lax.dot_general(qh, k, (((1,), (1,)), ((), ())), preferred_element_type=F32)
        p = _sink_softmax(s, mask, sink_ref[h])
        outs.append(jnp.dot(p.astype(BF16), v, preferred_element_type=F32))
    o_ref[...] = jnp.concatenate(outs, axis=1).astype(BF16)


def _attn_prompt(q2, kv2, sinks, nb, lp, pad):
    return pl.pallas_call(
        functools.partial(_attn_prompt_kernel, pad),
        grid=(nb, lp // BLOCK),
        in_specs=[pl.BlockSpec(memory_space=pltpu.SMEM),
                  pl.BlockSpec((BLOCK, ATT_W), lambda b, n: (n, b)),
                  pl.BlockSpec((BLOCK, 2 * KV_W), lambda b, n: (jnp.maximum(n - 1, 0), b)),
                  pl.BlockSpec((BLOCK, 2 * KV_W), lambda b, n: (n, b))],
        out_specs=pl.BlockSpec((BLOCK, ATT_W), lambda b, n: (n, b)),
        out_shape=jax.ShapeDtypeStruct((lp, nb * ATT_W), BF16),
        compiler_params=_cparams("parallel", "parallel"),
        name="attn_prompt",
    )(sinks, q2, kv2, kv2)


SAMPLE_BB = 8


def _attn_sample_kernel(ds, sink_ref, q_ref, kvn_ref, ck_ref, cv_ref, o_ref, nk_ref, nv_ref):
    rows = GROUP * ds
    r = lax.broadcasted_iota(jnp.int32, (rows, WINDOW + ds), 0)
    j = lax.broadcasted_iota(jnp.int32, (rows, WINDOW + ds), 1)
    tq = r & (ds - 1)
    mask = (j >= tq) & (j <= tq + WINDOW)
    for bl in range(SAMPLE_BB):
        qb = q_ref[:, bl * ATT_W:(bl + 1) * ATT_W].astype(F32)
        kvb = kvn_ref[:, bl * 2 * KV_W:(bl + 1) * 2 * KV_W]
        ck = ck_ref[bl]
        cv = cv_ref[bl]
        nk_ref[bl] = jnp.concatenate([ck[ds:], kvb[:, :KV_W]], axis=0)
        nv_ref[bl] = jnp.concatenate([cv[ds:], kvb[:, KV_W:]], axis=0)
        pieces = []
        for kvh in range(N_KV_HEADS):
            lo, hi = kvh * HEAD_DIM, (kvh + 1) * HEAD_DIM
            k = jnp.concatenate([ck[:, lo:hi], kvb[:, lo:hi]], axis=0).astype(BF16)
            v = jnp.concatenate([cv[:, lo:hi], kvb[:, KV_W + lo:KV_W + hi]], axis=0).astype(BF16)
            qg = jnp.concatenate(
                [qb[:, (kvh * GROUP + g) * HEAD_DIM:(kvh * GROUP + g + 1) * HEAD_DIM] for g in range(GROUP)],
                axis=0).astype(BF16)
            s = lax.dot_general(qg, k, (((1,), (1,)), ((), ())), preferred_element_type=F32)
            p = _sink_softmax(s, mask, sink_ref[kvh][:, 0:1])
            o = jnp.dot(p.astype(BF16), v, preferred_element_type=F32)
            pieces += [o[g * ds:(g + 1) * ds] for g in range(GROUP)]
        o_ref[:, bl * ATT_W:(bl + 1) * ATT_W] = jnp.concatenate(pieces, axis=1).astype(BF16)


def _attn_sample(q, kv, cache_k, cache_v, sinks, nb, ds):
    q2 = q.reshape(ds, nb * ATT_W)
    kv2 = kv.reshape(ds, nb * 2 * KV_W)
    sink_rows = jnp.broadcast_to(
        jnp.repeat(sinks.reshape(N_KV_HEADS, GROUP), ds, axis=1)[:, :, None], (N_KV_HEADS, GROUP * ds, LANES))
    cache_spec = pl.BlockSpec((SAMPLE_BB, WINDOW, KV_W), lambda i: (i, 0, 0))
    att, nk, nv = pl.pallas_call(
        functools.partial(_attn_sample_kernel, ds),
        grid=(nb // SAMPLE_BB,),
        in_specs=[_resident(sink_rows.shape),
                  pl.BlockSpec((ds, SAMPLE_BB * ATT_W), lambda i: (0, i)),
                  pl.BlockSpec((ds, SAMPLE_BB * 2 * KV_W), lambda i: (0, i)),
                  cache_spec, cache_spec],
        out_specs=[pl.BlockSpec((ds, SAMPLE_BB * ATT_W), lambda i: (0, i)), cache_spec, cache_spec],
        out_shape=[jax.ShapeDtypeStruct((ds, nb * ATT_W), BF16),
                   jax.ShapeDtypeStruct(cache_k.shape, F32), jax.ShapeDtypeStruct(cache_v.shape, F32)],
        compiler_params=_cparams("parallel"),
        name="attn_sample",
    )(sink_rows, q2, kv2, cache_k, cache_v)
    return att.reshape(ds * nb, ATT_W), nk, nv


RG_ROWS = 256
RG_TCN = 64


def _rglru_kernel(nb, tcn, pos0, xr_ref, yr_ref, cprev_ref, hprev_ref, wc_ref, bc_ref, wg_ref, ba_ref, bi_ref,
                  lam_ref, rec_ref, hlast_ref, xbuf, a_scr, b_scr, hc):
    step = pl.program_id(0)
    rows = tcn * nb
    tail = (CONV_W - 1) * nb

    @pl.when(step == 0)
    def _():
        xbuf[0:tail, :] = cprev_ref[...]
        hc[...] = hprev_ref[...]

    @pl.when(step > 0)
    def _():
        xbuf[0:tail, :] = xbuf[rows:rows + tail, :]

    def row_pos(r0, n):
        t_row = lax.shift_right_logical(r0 + lax.broadcasted_iota(jnp.int32, (n, 1), 0), nb.bit_length() - 1)
        return pos0 + step * tcn + t_row

    sub = min(RG_ROWS, rows)
    for r0 in range(0, rows, sub):
        x = xr_ref[r0:r0 + sub, :]
        if pos0 < 0:
            x = jnp.where(row_pos(r0, sub) >= 0, x, 0.0)
        xbuf[tail + r0:tail + r0 + sub, :] = x

    lam = lam_ref[...]
    neg_c_softplus = -LRU_C * (jnp.maximum(-lam, 0.0) + jnp.log1p(jnp.exp(-jnp.abs(lam))))

    for r0 in range(0, rows, sub):
        xc = bc_ref[...]
        for jj in range(CONV_W):
            xc = xc + xbuf[r0 + jj * nb:r0 + jj * nb + sub, :] * wc_ref[jj:jj + 1, :]
        xcb = xc.astype(BF16)
        pos = row_pos(r0, sub)
        for c in range(2):
            lo, hi = c * REC_HALF, (c + 1) * REC_HALF
            gates = jnp.dot(xcb[:, lo:hi], wg_ref[c], preferred_element_type=F32)
            r = _sigmoid(gates[:, :REC_HALF] + ba_ref[:, lo:hi])
            ig = _sigmoid(gates[:, REC_HALF:] + bi_ref[:, lo:hi])
            a = jnp.exp(r * neg_c_softplus[:, lo:hi])
            u = 1.0 - a * a
            mult = u * lax.rsqrt(jnp.maximum(u, SQRT_GUARD))
            if pos0 <= 0:
                mult = jnp.where(pos == 0, 1.0, mult)
            bt = mult * ig * xc[:, lo:hi]
            if pos0 < 0:
                bt = jnp.where(pos >= 0, bt, 0.0)
            a_scr[r0:r0 + sub, lo:hi] = a
            b_scr[r0:r0 + sub, lo:hi] = bt

    if nb * REC_W <= 16 * 1024:
        def body(t, h):
            rr = pl.multiple_of(t * nb, nb)
            h = a_scr[pl.ds(rr, nb), :] * h + b_scr[pl.ds(rr, nb), :]
            b_scr[pl.ds(rr, nb), :] = h
            return h
        hc[...] = lax.fori_loop(0, tcn, body, hc[...])
    else:
        for t in range(tcn):
            h = a_scr[t * nb:(t + 1) * nb, :] * hc[...] + b_scr[t * nb:(t + 1) * nb, :]
            b_scr[t * nb:(t + 1) * nb, :] = h
            hc[...] = h

    hlast_ref[...] = hc[...]
    for r0 in range(0, rows, sub):
        rec_ref[r0:r0 + sub, :] = (b_scr[r0:r0 + sub, :] * jax.nn.gelu(yr_ref[r0:r0 + sub, :])).astype(BF16)


def _rglru(xr, yr, row_block0, n_chunks, nb, tcn, pos0, conv_prev, h_prev, wts):
    rows = tcn * nb
    tail = (CONV_W - 1) * nb
    blk = pl.BlockSpec((rows, REC_W), lambda i: (row_block0 + i, 0))
    rec, h_last = pl.pallas_call(
        functools.partial(_rglru_kernel, nb, tcn, pos0),
        grid=(n_chunks,),
        in_specs=[blk, blk, _resident(conv_prev.shape), _resident(h_prev.shape)] + [_resident(w.shape) for w in wts],
        out_specs=[pl.BlockSpec((rows, REC_W), lambda i: (i, 0)), pl.BlockSpec((nb, REC_W), lambda i: (0, 0))],
        out_shape=[jax.ShapeDtypeStruct((n_chunks * rows, REC_W), BF16), jax.ShapeDtypeStruct((nb, REC_W), F32)],
        scratch_shapes=[pltpu.VMEM((rows + tail, REC_W), F32), pltpu.VMEM((rows, REC_W), F32),
                        pltpu.VMEM((rows, REC_W), F32), pltpu.VMEM((nb, REC_W), F32)],
        compiler_params=_cparams("arbitrary"),
        name="rglru",
    )(xr, yr, conv_prev, h_prev, *wts)
    return rec, h_last


def _merge_kernel(n_p, attp_ref, atts_ref, recp_ref, recs_ref, g_ref, x_ref, wa_ref, wr_ref, wo_ref, o_ref,
                  att_scr, att_rows):
    is_p = pl.program_id(0) < n_p

    @pl.when(is_p)
    def _():
        for b in range(NBV):
            for l in range(ATT_W // LANES):
                att_scr[l, pl.ds(b, TM // NBV, stride=NBV), :] = (
                    attp_ref[:, b * ATT_W + l * LANES:b * ATT_W + (l + 1) * LANES].astype(F32))
        for l in range(ATT_W // LANES):
            att_rows[:, l * LANES:(l + 1) * LANES] = att_scr[l].astype(BF16)

    @pl.when(jnp.logical_not(is_p))
    def _():
        att_rows[...] = atts_ref[...]

    att = att_rows[...]
    rec = jnp.where(is_p, recp_ref[...], recs_ref[...])
    ya = jnp.dot(att, wa_ref[...], preferred_element_type=F32)
    yr = jnp.dot(rec, wr_ref[...], preferred_element_type=F32)
    merged = _sigmoid(g_ref[:, :D_MODEL]) * ya + _sigmoid(g_ref[:, D_MODEL:]) * yr
    o_ref[...] = x_ref[...] + jnp.dot(merged.astype(BF16), wo_ref[...], preferred_element_type=F32)


def _merge(att_p, att_s, rec_p, rec_s, g, x, wa, wr, wo):
    t = x.shape[0]
    n_p = rec_p.shape[0] // TM
    n_s = att_s.shape[0] // TM
    p_spec = lambda w: pl.BlockSpec((TM, w), lambda i: (jnp.minimum(i, n_p - 1), 0))
    s_spec = lambda w: pl.BlockSpec((TM, w), lambda i: (jnp.clip(i - n_p, 0, n_s - 1), 0))
    row = lambda w: pl.BlockSpec((TM, w), lambda i: (i, 0))
    view_spec = pl.BlockSpec((TM // NBV, NBV * ATT_W), lambda i: (jnp.minimum(i, n_p - 1), 0))
    return pl.pallas_call(
        functools.partial(_merge_kernel, n_p),
        grid=(t // TM,),
        in_specs=[view_spec, s_spec(ATT_W), p_spec(REC_W), s_spec(REC_W), row(2 * D_MODEL), row(D_MODEL),
                  _resident(wa.shape), _resident(wr.shape), _resident(wo.shape)],
        out_specs=row(D_MODEL),
        out_shape=jax.ShapeDtypeStruct((t, D_MODEL), F32),
        scratch_shapes=[pltpu.VMEM((ATT_W // LANES, TM, LANES), F32), pltpu.VMEM((TM, ATT_W), BF16)],
        compiler_params=_cparams("parallel"),
        name="merge_out",
    )(att_p, att_s, rec_p, rec_s, g, x, wa, wr, wo)


FF_CHUNK = 512


def _swiglu_mid(hn, wg_ref, wu_ref, mid_ref, width):
    for c in range(0, width, FF_CHUNK):
        gate = jnp.dot(hn, wg_ref[:, c:c + FF_CHUNK], preferred_element_type=F32)
        up = jnp.dot(hn, wu_ref[:, c:c + FF_CHUNK], preferred_element_type=F32)
        mid_ref[:, c:c + FF_CHUNK] = (jax.nn.silu(gate) * up).astype(BF16)


def _ffn_kernel(x_ref, ln_ref, wg_ref, wu_ref, wd_ref, o_ref, mid_ref):
    x = x_ref[...]
    hn = _rms(x, ln_ref[...]).astype(BF16)
    _swiglu_mid(hn, wg_ref, wu_ref, mid_ref, D_FF)
    o_ref[...] = x + jnp.dot(mid_ref[...], wd_ref[...], preferred_element_type=F32)


def _ffn(x, ln, wg, wu, wd):
    t = x.shape[0]
    row = pl.BlockSpec((TM, D_MODEL), lambda i: (i, 0))
    return pl.pallas_call(
        _ffn_kernel,
        grid=(t // TM,),
        in_specs=[row, _resident((1, D_MODEL)), _resident(wg.shape), _resident(wu.shape), _resident(wd.shape)],
        out_specs=row,
        out_shape=jax.ShapeDtypeStruct((t, D_MODEL), F32),
        scratch_shapes=[pltpu.VMEM((TM, D_FF), BF16)],
        compiler_params=_cparams("parallel"),
        name="ffn_dense",
    )(x, ln, wg, wu, wd)


TMG = 512
TOP_K = 2
DISPATCH_ROWS = 512
INFO_E, INFO_RANK, INFO_W = 0, 2, 4


def _route_kernel(x_ref, ln_ref, wr_ref, br_ref, hn_ref, info_ref, cnt_ref, base_scr):
    @pl.when(pl.program_id(0) == 0)
    def _():
        base_scr[...] = jnp.zeros_like(base_scr)

    hn = _rms(x_ref[...], ln_ref[...])
    hn_ref[...] = hn
    lane = lax.broadcasted_iota(jnp.int32, (TM, LANES), 1).astype(F32)
    logits = jnp.dot(hn.astype(BF16), wr_ref[...], preferred_element_type=F32) + br_ref[...]
    m1 = jnp.max(logits, axis=-1, keepdims=True)
    i1 = jnp.min(jnp.where(logits == m1, lane, float(LANES)), axis=-1, keepdims=True)
    rest = jnp.where(lane == i1, -jnp.inf, logits)
    m2 = jnp.max(rest, axis=-1, keepdims=True)
    i2 = jnp.min(jnp.where(rest == m2, lane, float(LANES)), axis=-1, keepdims=True)
    e2 = jnp.exp(m2 - m1)
    den = 1.0 + e2
    hit1 = lane == i1
    hit2 = lane == i2
    onehot = jnp.where(hit1 | hit2, 1.0, 0.0)
    r = lax.broadcasted_iota(jnp.int32, (TM, TM), 0)
    c = lax.broadcasted_iota(jnp.int32, (TM, TM), 1)
    before = jnp.dot(jnp.where(c < r, 1.0, 0.0).astype(BF16), onehot.astype(BF16), preferred_element_type=F32)
    before = before + base_scr[0:1, :]
    rank1 = jnp.sum(jnp.where(hit1, before, 0.0), axis=-1, keepdims=True)
    rank2 = jnp.sum(jnp.where(hit2, before, 0.0), axis=-1, keepdims=True)
    base_scr[0:1, :] = base_scr[0:1, :] + jnp.sum(onehot, axis=0, keepdims=True)
    cnt_ref[...] = base_scr[...]
    info = jnp.zeros((TM, LANES), F32)
    for k, v in ((INFO_E, i1), (INFO_E + 1, i2), (INFO_RANK, rank1), (INFO_RANK + 1, rank2),
                 (INFO_W, 1.0 / den), (INFO_W + 1, e2 / den)):
        info = jnp.where(lane == float(k), v, info)
    info_ref[...] = info


def _route(x, ln, w_router, b_router):
    t = x.shape[0]
    row = lambda w: pl.BlockSpec((TM, w), lambda i: (i, 0))
    return pl.pallas_call(
        _route_kernel,
        grid=(t // TM,),
        in_specs=[row(D_MODEL), _resident((1, D_MODEL)), _resident(w_router.shape), _resident(b_router.shape)],
        out_specs=[row(D_MODEL), row(LANES), pl.BlockSpec((8, LANES), lambda i: (0, 0))],
        out_shape=[jax.ShapeDtypeStruct((t, D_MODEL), F32), jax.ShapeDtypeStruct((t, LANES), F32),
                   jax.ShapeDtypeStruct((8, LANES), F32)],
        scratch_shapes=[pltpu.VMEM((8, LANES), F32)],
        compiler_params=_cparams("arbitrary"),
        name="moe_route",
    )(x, ln, w_router, b_router)


def _dispatch_kernel(pos_hbm, hn_ref, xs_in, xs_out, pos_smem, idx_sem, row_sem):
    del xs_in
    i = pl.program_id(0)
    idx_copy = pltpu.make_async_copy(pos_hbm.at[i], pos_smem, idx_sem)
    idx_copy.start()
    idx_copy.wait()

    def issue(j, carry):
        base = pl.multiple_of(j * 8, 8)
        for u in range(8):
            for k in range(TOP_K):
                pltpu.make_async_copy(hn_ref.at[pl.ds(base + u, 1)],
                                      xs_out.at[pl.ds(pos_smem[TOP_K * (base + u) + k], 1)],
                                      row_sem).start(priority=k)
        return carry

    lax.fori_loop(0, DISPATCH_ROWS // 8, issue, 0)
    for k in range(TOP_K):
        pltpu.make_async_copy(hn_ref, xs_out.at[pl.ds(0, DISPATCH_ROWS)], row_sem).wait()


def _dispatch(pos, hn, n_rows):
    t = hn.shape[0]
    steps = t // DISPATCH_ROWS
    xs0 = jnp.zeros((n_rows, D_MODEL), F32)
    return pl.pallas_call(
        _dispatch_kernel,
        grid=(steps,),
        in_specs=[pl.BlockSpec(memory_space=pl.ANY), pl.BlockSpec((DISPATCH_ROWS, D_MODEL), lambda i: (i, 0)),
                  pl.BlockSpec(memory_space=pl.ANY)],
        out_specs=pl.BlockSpec(memory_space=pl.ANY),
        out_shape=jax.ShapeDtypeStruct((n_rows, D_MODEL), F32),
        scratch_shapes=[pltpu.SMEM((TOP_K * DISPATCH_ROWS,), jnp.int32), pltpu.SemaphoreType.DMA,
                        pltpu.SemaphoreType.DMA],
        input_output_aliases={2: 0},
        compiler_params=_cparams("arbitrary"),
        name="moe_dispatch",
    )(pos.reshape(steps, TOP_K * DISPATCH_ROWS), hn, xs0)


def _expert_kernel(te_ref, nu_ref, xs_ref, wg_ref, wu_ref, wd_ref, y_ref, mid_ref):
    del te_ref

    @pl.when(pl.program_id(0) < nu_ref[0])
    def _():
        _swiglu_mid(xs_ref[...].astype(BF16), wg_ref, wu_ref, mid_ref, EXPERT_FF)
        y_ref[...] = jnp.dot(mid_ref[...], wd_ref[...], preferred_element_type=F32)

    @pl.when(pl.program_id(0) >= nu_ref[0])
    def _():
        y_ref[...] = jnp.zeros_like(y_ref)


def _experts(tile_expert, n_used, xs, wg, wu, wd):
    n_rows = xs.shape[0]
    row = pl.BlockSpec((TMG, D_MODEL), lambda i, te, nu: (i, 0))
    ex = lambda a, b: pl.BlockSpec((None, a, b), lambda i, te, nu: (te[i], 0, 0))
    return pl.pallas_call(
        _expert_kernel,
        grid_spec=pltpu.PrefetchScalarGridSpec(
            num_scalar_prefetch=2,
            grid=(n_rows // TMG,),
            in_specs=[row, ex(D_MODEL, EXPERT_FF), ex(D_MODEL, EXPERT_FF), ex(EXPERT_FF, D_MODEL)],
            out_specs=row,
            scratch_shapes=[pltpu.VMEM((TMG, EXPERT_FF), BF16)]),
        out_shape=jax.ShapeDtypeStruct((n_rows, D_MODEL), F32),
        compiler_params=_cparams("arbitrary"),
        name="moe_experts",
    )(tile_expert, n_used, xs, wg, wu, wd)


def _combine_kernel(final, pos_hbm, x_ref, info_ref, ln_ref, y_hbm, *rest):
    if final:
        yp_ref, ys_ref, pos_smem, buf, idx_sem, row_sem, out_scr = rest
    else:
        o_ref, pos_smem, buf, idx_sem, row_sem = rest
    i = pl.program_id(0)
    n = pl.num_programs(0)

    def gather(tile, slot):
        idx_copy = pltpu.make_async_copy(pos_hbm.at[tile], pos_smem.at[slot], idx_sem)
        idx_copy.start()
        idx_copy.wait()

        def issue(j, carry):
            base = pl.multiple_of(j * 8, 8)
            for u in range(8):
                for k in range(TOP_K):
                    pltpu.make_async_copy(y_hbm.at[pl.ds(pos_smem[slot, TOP_K * (base + u) + k], 1)],
                                          buf.at[slot, k, pl.ds(base + u, 1)], row_sem.at[slot]).start(priority=k)
            return carry

        lax.fori_loop(0, TM // 8, issue, 0)

    @pl.when(i == 0)
    def _():
        gather(0, 0)

    for par in range(2):
        @pl.when((i + 1 < n) & ((i + 1) % 2 == par))
        def _():
            gather(i + 1, par)

    slot = i % 2
    for k in range(TOP_K):
        pltpu.make_async_copy(y_hbm.at[pl.ds(0, TM)], buf.at[slot, k], row_sem.at[slot]).wait()
    lane = lax.broadcasted_iota(jnp.int32, (TM, LANES), 1)
    info = info_ref[...]
    out = x_ref[...]
    for k in range(TOP_K):
        w_k = jnp.sum(jnp.where(lane == INFO_W + k, info, 0.0), axis=-1, keepdims=True)
        out = out + w_k * buf[slot, k]
    if not final:
        o_ref[...] = out
        return
    skip, n_p = final
    res = _rms(out, ln_ref[...])

    @pl.when((i >= skip) & (i < n_p))
    def _():
        for l in range(D_MODEL // LANES):
            out_scr[l] = res[:, l * LANES:(l + 1) * LANES]
        for l in range(D_MODEL // LANES):
            for b in range(NBV):
                yp_ref[b, :, l * LANES:(l + 1) * LANES] = out_scr[l, pl.ds(b, TM // NBV, stride=NBV), :]

    @pl.when(i >= n_p)
    def _():
        ys_ref[...] = res


def _combine(pos, x, info, y, final):
    t = x.shape[0]
    steps = t // TM
    row = lambda w: pl.BlockSpec((TM, w), lambda i: (i, 0))
    scratch = [pltpu.SMEM((2, TOP_K * TM), jnp.int32), pltpu.VMEM((2, TOP_K, TM, D_MODEL), F32),
               pltpu.SemaphoreType.DMA, pltpu.SemaphoreType.DMA((2,))]
    if final is None:
        ln, mode = jnp.ones((1, D_MODEL), F32), None
        out_specs = row(D_MODEL)
        out_shape = jax.ShapeDtypeStruct((t, D_MODEL), F32)
    else:
        ln, skip, n_p, n_s, bp, seq = final
        mode = (skip, n_p)
        out_specs = [pl.BlockSpec((NBV, TM // NBV, D_MODEL), lambda i: (0, jnp.clip(i - skip, 0, n_p - skip - 1), 0)),
                     pl.BlockSpec((TM, D_MODEL), lambda i: (jnp.clip(i - n_p, 0, n_s - 1), 0))]
        out_shape = [jax.ShapeDtypeStruct((bp, seq, D_MODEL), F32), jax.ShapeDtypeStruct((n_s * TM, D_MODEL), F32)]
        scratch = scratch + [pltpu.VMEM((D_MODEL // LANES, TM, LANES), F32)]
    return pl.pallas_call(
        functools.partial(_combine_kernel, mode),
        grid=(steps,),
        in_specs=[pl.BlockSpec(memory_space=pl.ANY), row(D_MODEL), row(LANES), _resident((1, D_MODEL)),
                  pl.BlockSpec(memory_space=pl.ANY)],
        out_specs=out_specs,
        out_shape=out_shape,
        scratch_shapes=scratch,
        compiler_params=_cparams("arbitrary"),
        name="moe_combine",
    )(pos.reshape(steps, TOP_K * TM), x, info, ln, y)


def _moe(x, ln, w_router, b_router, wg, wu, wd, final):
    t = x.shape[0]
    hn, info, cnt = _route(x, ln, w_router, b_router)
    counts = cnt[0, :N_EXPERTS].astype(jnp.int32)
    tiles = (counts + TMG - 1) // TMG
    tile_end = jnp.cumsum(tiles)
    offsets = (tile_end - tiles) * TMG
    n_tiles = TOP_K * t // TMG + N_EXPERTS
    tile_expert = jnp.minimum(jnp.sum(jnp.arange(n_tiles)[:, None] >= tile_end[None, :], axis=1),
                              N_EXPERTS - 1).astype(jnp.int32)
    n_used = tile_end[N_EXPERTS - 1:].astype(jnp.int32)
    e = info[:, INFO_E:INFO_E + TOP_K].astype(jnp.int32)
    pos = (jnp.sum(jnp.where(e[..., None] == jnp.arange(N_EXPERTS), offsets, 0), axis=-1)
           + info[:, INFO_RANK:INFO_RANK + TOP_K].astype(jnp.int32)).reshape(-1)
    xs = _dispatch(pos, hn, n_tiles * TMG)
    y = _experts(tile_expert, n_used, xs, wg, wu, wd)
    return _combine(pos, x, info, y, final)


def _pack_kernel(skip, n_p, head_ref, xp_ref, xs_ref, o_ref, scr):
    i = pl.program_id(0)

    @pl.when(i < skip)
    def _():
        o_ref[...] = head_ref[...]

    @pl.when((i >= skip) & (i < n_p))
    def _():
        for l in range(D_MODEL // LANES):
            for b in range(NBV):
                scr[l, pl.ds(b, TM // NBV, stride=NBV), :] = xp_ref[b, :, l * LANES:(l + 1) * LANES]
        for l in range(D_MODEL // LANES):
            o_ref[:, l * LANES:(l + 1) * LANES] = scr[l]

    @pl.when(i >= n_p)
    def _():
        o_ref[...] = xs_ref[...]


def _pack_tokens(head, x_prompt, xs_tm):
    bp, seq, _ = x_prompt.shape
    skip = head.shape[0] // TM
    n_p = skip + bp * seq // TM
    n_s = xs_tm.shape[0] // TM
    return pl.pallas_call(
        functools.partial(_pack_kernel, skip, n_p),
        grid=(n_p + n_s,),
        in_specs=[pl.BlockSpec((TM, D_MODEL), lambda i: (jnp.minimum(i, skip - 1), 0)),
                  pl.BlockSpec((NBV, TM // NBV, D_MODEL), lambda i: (0, jnp.clip(i - skip, 0, n_p - skip - 1), 0)),
                  pl.BlockSpec((TM, D_MODEL), lambda i: (jnp.clip(i - n_p, 0, n_s - 1), 0))],
        out_specs=pl.BlockSpec((TM, D_MODEL), lambda i: (i, 0)),
        out_shape=jax.ShapeDtypeStruct(((n_p + n_s) * TM, D_MODEL), F32),
        scratch_shapes=[pltpu.VMEM((D_MODEL // LANES, TM, LANES), F32)],
        compiler_params=_cparams("parallel"),
        name="pack_tokens",
    )(head, x_prompt, xs_tm)


def _final_norm_kernel(x_ref, ln_ref, o_ref):
    o_ref[...] = _rms(x_ref[...], ln_ref[...])


def _final_norm(x, ln):
    t = x.shape[0]
    row = pl.BlockSpec((TM, D_MODEL), lambda i: (i, 0))
    return pl.pallas_call(
        _final_norm_kernel,
        grid=(t // TM,),
        in_specs=[row, _resident((1, D_MODEL))],
        out_specs=row,
        out_shape=jax.ShapeDtypeStruct((t, D_MODEL), F32),
        compiler_params=_cparams("parallel"),
        name="final_norm",
    )(x, ln)


def _gate_weights(w_a, w_i):
    half_blocks = REC_BLOCKS // 2
    eye = jnp.eye(half_blocks, dtype=F32)

    def dense(w):
        return (eye[:, None, :, None] * w[:, :, None, :]).reshape(REC_HALF, REC_HALF)

    halves = [jnp.concatenate([dense(w_a[c * half_blocks:(c + 1) * half_blocks]),
                               dense(w_i[c * half_blocks:(c + 1) * half_blocks])], axis=1) for c in range(2)]
    return jnp.stack(halves).astype(BF16)


def kernel(x_prompt, x_sample, cache_k, cache_v, state_conv, state_rglru, meta_tokens, ln_mix, w_in, attn_sinks,
           w_conv, b_conv, w_rg_a, b_rg_a, w_rg_i, b_rg_i, rg_lambda, w_att_proj, w_rec_proj, w_out, ln_ffn,
           w_ffn_gate, w_ffn_up, w_ffn_down, w_router, b_router, w_exp_gate, w_exp_up, w_exp_down, ln_final):
    bp, seq, _ = x_prompt.shape
    db, ds, _ = x_sample.shape
    depth = w_in.shape[0]
    pad = (-(N_META + seq)) % BLOCK
    lp = pad + N_META + seq
    tp, ts = lp * bp, ds * db
    assert tp % TM == 0 and ts % TM == 0 and db % SAMPLE_BB == 0 and ds >= CONV_W - 1
    assert seq >= WINDOW and bp & (bp - 1) == 0 and db & (db - 1) == 0 and ds & (ds - 1) == 0
    assert lp % RG_TCN == 0 and tp % ts == 0 and bp == NBV

    head_rows = (pad + N_META) * bp
    assert head_rows % TM == 0 and (seq * bp) % TM == 0
    head = jnp.concatenate([jnp.zeros((pad * bp, D_MODEL), F32), jnp.repeat(meta_tokens, bp, axis=0)], axis=0)
    x = _pack_tokens(head, x_prompt, jnp.transpose(x_sample, (1, 0, 2)).reshape(ts, D_MODEL))

    zeros_conv = jnp.zeros(((CONV_W - 1) * bp, REC_W), F32)
    zeros_h = jnp.zeros((bp, REC_W), F32)
    outs = {k: [] for k in ("kp", "vp", "cp", "hp", "ks", "vs", "cs", "hs")}

    for l in range(depth):
        q2, kv2, xr, yr, g = _in_proj(x, ln_mix[l].reshape(1, D_MODEL), w_in[l].astype(BF16))

        att_p = _attn_prompt(q2, kv2, attn_sinks[l], bp, lp, pad)
        att_s, nk, nv = _attn_sample(q2[lp:].reshape(ts, ATT_W), kv2[lp:].reshape(ts, 2 * KV_W),
                                     cache_k[l].reshape(db, WINDOW, KV_W),
                                     cache_v[l].reshape(db, WINDOW, KV_W), attn_sinks[l], db, ds)

        rg_w = (w_conv[l], b_conv[l].reshape(1, REC_W), _gate_weights(w_rg_a[l], w_rg_i[l]),
                b_rg_a[l].reshape(1, REC_W), b_rg_i[l].reshape(1, REC_W), rg_lambda[l].reshape(1, REC_W))
        rec_p, h_p = _rglru(xr, yr, 0, lp // RG_TCN, bp, RG_TCN, -pad, zeros_conv, zeros_h, rg_w)
        conv_s = jnp.transpose(state_conv[l], (1, 0, 2)).reshape((CONV_W - 1) * db, REC_W)
        rec_s, h_s = _rglru(xr, yr, tp // ts, 1, db, ds, PAST_LEN, conv_s, state_rglru[l], rg_w)

        x = _merge(att_p, att_s, rec_p, rec_s, g, x, w_att_proj[l].astype(BF16), w_rec_proj[l].astype(BF16),
                   w_out[l].astype(BF16))

        j = l // 2
        ln_f = ln_ffn[l].reshape(1, D_MODEL)
        if l % 2 == 0:
            x = _ffn(x, ln_f, w_ffn_gate[j].astype(BF16), w_ffn_up[j].astype(BF16), w_ffn_down[j].astype(BF16))
        else:
            w_r = jnp.zeros((D_MODEL, LANES), F32).at[:, :N_EXPERTS].set(w_router[j]).astype(BF16)
            b_r = jnp.full((1, LANES), ROUTER_PAD_BIAS, F32).at[0, :N_EXPERTS].set(b_router[j])
            x = _moe(x, ln_f, w_r, b_r, w_exp_gate[j].astype(BF16), w_exp_up[j].astype(BF16),
                     w_exp_down[j].astype(BF16),
                     (ln_final.reshape(1, D_MODEL), head_rows // TM, tp // TM, ts // TM, bp, seq)
                     if l == depth - 1 else None)

        kv_p = kv2[lp - WINDOW:lp].reshape(WINDOW, bp, 2 * KV_W)
        outs["kp"].append(jnp.transpose(kv_p[..., :KV_W], (1, 0, 2)).reshape(bp, WINDOW, N_KV_HEADS, HEAD_DIM))
        outs["vp"].append(jnp.transpose(kv_p[..., KV_W:], (1, 0, 2)).reshape(bp, WINDOW, N_KV_HEADS, HEAD_DIM))
        outs["cp"].append(jnp.transpose(xr[tp - (CONV_W - 1) * bp:tp].reshape(CONV_W - 1, bp, REC_W), (1, 0, 2)))
        outs["hp"].append(h_p)
        outs["ks"].append(nk.reshape(db, WINDOW, N_KV_HEADS, HEAD_DIM))
        outs["vs"].append(nv.reshape(db, WINDOW, N_KV_HEADS, HEAD_DIM))
        outs["cs"].append(jnp.transpose(xr[tp + ts - (CONV_W - 1) * db:].reshape(CONV_W - 1, db, REC_W), (1, 0, 2)))
        outs["hs"].append(h_s)

    if (depth - 1) % 2 == 1:
        y_prompt, ys_tm = x
    else:
        y = _final_norm(x, ln_final.reshape(1, D_MODEL))
        y_prompt = jnp.transpose(y[head_rows:tp].reshape(seq, bp, D_MODEL), (1, 0, 2))
        ys_tm = y[tp:]
    y_sample = jnp.transpose(ys_tm.reshape(ds, db, D_MODEL), (1, 0, 2))
    st = {k: jnp.stack(v) for k, v in outs.items()}
    return (y_prompt, y_sample, st["kp"], st["vp"], st["cp"], st["hp"], st["ks"], st["vs"], st["cs"], st["hs"])
```

```python
import functools

import jax
import jax.numpy as jnp
from jax import lax
from jax.experimental import pallas as pl
from jax.experimental.pallas import tpu as pltpu

F32 = jnp.float32
BF16 = jnp.bfloat16

D_MODEL = 1024
N_META = 16
N_HEADS = 16
N_KV_HEADS = 2
HEAD_DIM = D_MODEL // N_HEADS
GROUP = N_HEADS // N_KV_HEADS
ATT_W = N_HEADS * HEAD_DIM
KV_W = N_KV_HEADS * HEAD_DIM
WINDOW = 128
BLOCK = 128
ATT_SCALE = HEAD_DIM ** -0.5
REC_W = 5 * D_MODEL // 4
REC_BLOCKS = 16
REC_BW = REC_W // REC_BLOCKS
REC_HALF = REC_W // 2
CONV_W = 4
LRU_C = 8.0
D_FF = 3 * D_MODEL
N_EXPERTS = 8
EXPERT_FF = 3 * D_MODEL // 2
PAST_LEN = 8192
RMS_EPS = 1e-6
IN_OFFS = (0, ATT_W, ATT_W + 2 * KV_W, ATT_W + 2 * KV_W + REC_W, ATT_W + 2 * KV_W + 2 * REC_W,
           ATT_W + 2 * KV_W + 2 * REC_W + 2 * D_MODEL)
LANES = 128
VMEM_LIMIT = 52 * 1024 * 1024

TM = 512
ROUTER_PAD_BIAS = -1e30


def _cparams(*sem):
    return pltpu.CompilerParams(dimension_semantics=sem, vmem_limit_bytes=VMEM_LIMIT)


def _resident(shape):
    nd = len(shape)
    return pl.BlockSpec(shape, lambda *_: (0,) * nd, pipeline_mode=pl.Buffered(1))


def _rms(x, g):
    return x * lax.rsqrt(jnp.mean(x * x, axis=-1, keepdims=True) + RMS_EPS) * g


SQRT_GUARD = 1e-30


def _sigmoid(x):
    return 0.5 * jnp.tanh(0.5 * x) + 0.5


NBV = 8


def _to_col_view(val, scr, out_ref, dtype):
    rows, width = val.shape
    for l in range(width // LANES):
        scr[l] = val[:, l * LANES:(l + 1) * LANES]
    for b in range(NBV):
        for l in range(width // LANES):
            out_ref[:, b * width + l * LANES:b * width + (l + 1) * LANES] = (
                scr[l, pl.ds(b, rows // NBV, stride=NBV), :].astype(dtype))


def _in_proj_kernel(x_ref, ln_ref, w_ref, q_ref, kv_ref, xr_ref, yr_ref, g_ref, q_scr, kv_scr):
    hn = _rms(x_ref[...], ln_ref[...]).astype(BF16)

    def proj(a, b):
        return jnp.dot(hn, w_ref[:, a:b], preferred_element_type=F32)

    _to_col_view(proj(IN_OFFS[0], IN_OFFS[1]) * ATT_SCALE, q_scr, q_ref, BF16)
    _to_col_view(proj(IN_OFFS[1], IN_OFFS[2]), kv_scr, kv_ref, F32)
    xr_ref[...] = proj(IN_OFFS[2], IN_OFFS[3])
    yr_ref[...] = proj(IN_OFFS[3], IN_OFFS[4])
    g_ref[...] = proj(IN_OFFS[4], IN_OFFS[5])


def _in_proj(x, ln, w_in):
    t = x.shape[0]
    row = lambda w: pl.BlockSpec((TM, w), lambda i: (i, 0))
    view = lambda w: pl.BlockSpec((TM // NBV, NBV * w), lambda i: (i, 0))
    return pl.pallas_call(
        _in_proj_kernel,
        grid=(t // TM,),
        in_specs=[row(D_MODEL), _resident((1, D_MODEL)), _resident(w_in.shape)],
        out_specs=[view(ATT_W), view(2 * KV_W), row(REC_W), row(REC_W), row(2 * D_MODEL)],
        out_shape=[jax.ShapeDtypeStruct((t // NBV, NBV * ATT_W), BF16),
                   jax.ShapeDtypeStruct((t // NBV, NBV * 2 * KV_W), F32),
                   jax.ShapeDtypeStruct((t, REC_W), F32), jax.ShapeDtypeStruct((t, REC_W), F32),
                   jax.ShapeDtypeStruct((t, 2 * D_MODEL), F32)],
        scratch_shapes=[pltpu.VMEM((ATT_W // LANES, TM, LANES), F32), pltpu.VMEM((2 * KV_W // LANES, TM, LANES), F32)],
        compiler_params=_cparams("parallel"),
        name="in_proj",
    )(x, ln, w_in)


def _sink_softmax(s, mask, sk):
    s = jnp.where(mask, s, -jnp.inf)
    m = jnp.maximum(jnp.max(s, axis=-1, keepdims=True), sk)
    p = jnp.exp(s - m)
    return p / (jnp.sum(p, axis=-1, keepdims=True) + jnp.exp(sk - m))


def _attn_prompt_kernel(pad, sink_ref, q_ref, kvp_ref, kvc_ref, o_ref):
    n = pl.program_id(1)
    q = q_ref[...]
    kv = jnp.concatenate([kvp_ref[...], kvc_ref[...]], axis=0).astype(BF16)
    i = lax.broadcasted_iota(jnp.int32, (BLOCK, 2 * BLOCK), 0)
    j = lax.broadcasted_iota(jnp.int32, (BLOCK, 2 * BLOCK), 1)
    mask = (j >= i) & (j <= i + WINDOW) & (j >= pad + BLOCK - n * BLOCK)
    outs = []
    for h in range(N_HEADS):
        kvh = h // GROUP
        qh = q[:, h * HEAD_DIM:(h + 1) * HEAD_DIM]
        k = kv[:, kvh * HEAD_DIM:(kvh + 1) * HEAD_DIM]
        v = kv[:, KV_W + kvh * HEAD_DIM:KV_W + (kvh + 1) * HEAD_DIM]
        s = lax.dot_general(qh, k, (((1,), (1,)), ((), ())), preferred_element_type=F32)
        p = _sink_softmax(s, mask, sink_ref[h])
        outs.append(jnp.dot(p.astype(BF16), v, preferred_element_type=F32))
    o_ref[...] = jnp.concatenate(outs, axis=1).astype(BF16)


def _attn_prompt(q2, kv2, sinks, nb, lp, pad):
    return pl.pallas_call(
        functools.partial(_attn_prompt_kernel, pad),
        grid=(nb, lp // BLOCK),
        in_specs=[pl.BlockSpec(memory_space=pltpu.SMEM),
                  pl.BlockSpec((BLOCK, ATT_W), lambda b, n: (n, b)),
                  pl.BlockSpec((BLOCK, 2 * KV_W), lambda b, n: (jnp.maximum(n - 1, 0), b)),
                  pl.BlockSpec((BLOCK, 2 * KV_W), lambda b, n: (n, b))],
        out_specs=pl.BlockSpec((BLOCK, ATT_W), lambda b, n: (n, b)),
        out_shape=jax.ShapeDtypeStruct((lp, nb * ATT_W), BF16),
        compiler_params=_cparams("parallel", "parallel"),
        name="attn_prompt",
    )(sinks, q2, kv2, kv2)


SAMPLE_BB = 8


def _attn_sample_kernel(ds, sink_ref, q_ref, kvn_ref, ck_ref, cv_ref, o_ref, nk_ref, nv_ref):
    rows = GROUP * ds
    r = lax.broadcasted_iota(jnp.int32, (rows, WINDOW + ds), 0)
    j = lax.broadcasted_iota(jnp.int32, (rows, WINDOW + ds), 1)
    tq = r & (ds - 1)
    mask = (j >= tq) & (j <= tq + WINDOW)
    for bl in range(SAMPLE_BB):
        qb = q_ref[:, bl * ATT_W:(bl + 1) * ATT_W].astype(F32)
        kvb = kvn_ref[:, bl * 2 * KV_W:(bl + 1) * 2 * KV_W]
        ck = ck_ref[bl]
        cv = cv_ref[bl]
        nk_ref[bl] = jnp.concatenate([ck[ds:], kvb[:, :KV_W]], axis=0)
        nv_ref[bl] = jnp.concatenate([cv[ds:], kvb[:, KV_W:]], axis=0)
        pieces = []
        for kvh in range(N_KV_HEADS):
            lo, hi = kvh * HEAD_DIM, (kvh + 1) * HEAD_DIM
            k = jnp.concatenate([ck[:, lo:hi], kvb[:, lo:hi]], axis=0).astype(BF16)
            v = jnp.concatenate([cv[:, lo:hi], kvb[:, KV_W + lo:KV_W + hi]], axis=0).astype(BF16)
            qg = jnp.concatenate(
                [qb[:, (kvh * GROUP + g) * HEAD_DIM:(kvh * GROUP + g + 1) * HEAD_DIM] for g in range(GROUP)],
                axis=0).astype(BF16)
            s = lax.dot_general(qg, k, (((1,), (1,)), ((), ())), preferred_element_type=F32)
            p = _sink_softmax(s, mask, sink_ref[kvh][:, 0:1])
            o = jnp.dot(p.astype(BF16), v, preferred_element_type=F32)
            pieces += [o[g * ds:(g + 1) * ds] for g in range(GROUP)]
        o_ref[:, bl * ATT_W:(bl + 1) * ATT_W] = jnp.concatenate(pieces, axis=1).astype(BF16)


def _attn_sample(q, kv, cache_k, cache_v, sinks, nb, ds):
    q2 = q.reshape(ds, nb * ATT_W)
    kv2 = kv.reshape(ds, nb * 2 * KV_W)
    sink_rows = jnp.broadcast_to(
        jnp.repeat(sinks.reshape(N_KV_HEADS, GROUP), ds, axis=1)[:, :, None], (N_KV_HEADS, GROUP * ds, LANES))
    cache_spec = pl.BlockSpec((SAMPLE_BB, WINDOW, KV_W), lambda i: (i, 0, 0))
    att, nk, nv = pl.pallas_call(
        functools.partial(_attn_sample_kernel, ds),
        grid=(nb // SAMPLE_BB,),
        in_specs=[_resident(sink_rows.shape),
                  pl.BlockSpec((ds, SAMPLE_BB * ATT_W), lambda i: (0, i)),
                  pl.BlockSpec((ds, SAMPLE_BB * 2 * KV_W), lambda i: (0, i)),
                  cache_spec, cache_spec],
        out_specs=[pl.BlockSpec((ds, SAMPLE_BB * ATT_W), lambda i: (0, i)), cache_spec, cache_spec],
        out_shape=[jax.ShapeDtypeStruct((ds, nb * ATT_W), BF16),
                   jax.ShapeDtypeStruct(cache_k.shape, F32), jax.ShapeDtypeStruct(cache_v.shape, F32)],
        compiler_params=_cparams("parallel"),
        name="attn_sample",
    )(sink_rows, q2, kv2, cache_k, cache_v)
    return att.reshape(ds * nb, ATT_W), nk, nv


RG_ROWS = 256
RG_TCN = 64


def _rglru_kernel(nb, tcn, pos0, xr_ref, yr_ref, cprev_ref, hprev_ref, wc_ref, bc_ref, wg_ref, ba_ref, bi_ref,
                  lam_ref, rec_ref, hlast_ref, xbuf, a_scr, b_scr, hc):
    step = pl.program_id(0)
    rows = tcn * nb
    tail = (CONV_W - 1) * nb

    @pl.when(step == 0)
    def _():
        xbuf[0:tail, :] = cprev_ref[...]
        hc[...] = hprev_ref[...]

    @pl.when(step > 0)
    def _():
        xbuf[0:tail, :] = xbuf[rows:rows + tail, :]

    def row_pos(r0, n):
        t_row = lax.shift_right_logical(r0 + lax.broadcasted_iota(jnp.int32, (n, 1), 0), nb.bit_length() - 1)
        return pos0 + step * tcn + t_row

    sub = min(RG_ROWS, rows)
    for r0 in range(0, rows, sub):
        x = xr_ref[r0:r0 + sub, :]
        if pos0 < 0:
            x = jnp.where(row_pos(r0, sub) >= 0, x, 0.0)
        xbuf[tail + r0:tail + r0 + sub, :] = x

    lam = lam_ref[...]
    neg_c_softplus = -LRU_C * (jnp.maximum(-lam, 0.0) + jnp.log1p(jnp.exp(-jnp.abs(lam))))

    for r0 in range(0, rows, sub):
        xc = bc_ref[...]
        for jj in range(CONV_W):
            xc = xc + xbuf[r0 + jj * nb:r0 + jj * nb + sub, :] * wc_ref[jj:jj + 1, :]
        xcb = xc.astype(BF16)
        pos = row_pos(r0, sub)
        for c in range(2):
            lo, hi = c * REC_HALF, (c + 1) * REC_HALF
            gates = jnp.dot(xcb[:, lo:hi], wg_ref[c], preferred_element_type=F32)
            r = _sigmoid(gates[:, :REC_HALF] + ba_ref[:, lo:hi])
            ig = _sigmoid(gates[:, REC_HALF:] + bi_ref[:, lo:hi])
            a = jnp.exp(r * neg_c_softplus[:, lo:hi])
            u = 1.0 - a * a
            mult = u * lax.rsqrt(jnp.maximum(u, SQRT_GUARD))
            if pos0 <= 0:
                mult = jnp.where(pos == 0, 1.0, mult)
            bt = mult * ig * xc[:, lo:hi]
            if pos0 < 0:
                bt = jnp.where(pos >= 0, bt, 0.0)
            a_scr[r0:r0 + sub, lo:hi] = a
            b_scr[r0:r0 + sub, lo:hi] = bt

    if nb * REC_W <= 16 * 1024:
        def body(t, h):
            rr = pl.multiple_of(t * nb, nb)
            h = a_scr[pl.ds(rr, nb), :] * h + b_scr[pl.ds(rr, nb), :]
            b_scr[pl.ds(rr, nb), :] = h
            return h
        hc[...] = lax.fori_loop(0, tcn, body, hc[...])
    else:
        for t in range(tcn):
            h = a_scr[t * nb:(t + 1) * nb, :] * hc[...] + b_scr[t * nb:(t + 1) * nb, :]
            b_scr[t * nb:(t + 1) * nb, :] = h
            hc[...] = h

    hlast_ref[...] = hc[...]
    for r0 in range(0, rows, sub):
        rec_ref[r0:r0 + sub, :] = (b_scr[r0:r0 + sub, :] * jax.nn.gelu(yr_ref[r0:r0 + sub, :])).astype(BF16)


def _rglru(xr, yr, row_block0, n_chunks, nb, tcn, pos0, conv_prev, h_prev, wts):
    rows = tcn * nb
    tail = (CONV_W - 1) * nb
    blk = pl.BlockSpec((rows, REC_W), lambda i: (row_block0 + i, 0))
    rec, h_last = pl.pallas_call(
        functools.partial(_rglru_kernel, nb, tcn, pos0),
        grid=(n_chunks,),
        in_specs=[blk, blk, _resident(conv_prev.shape), _resident(h_prev.shape)] + [_resident(w.shape) for w in wts],
        out_specs=[pl.BlockSpec((rows, REC_W), lambda i: (i, 0)), pl.BlockSpec((nb, REC_W), lambda i: (0, 0))],
        out_shape=[jax.ShapeDtypeStruct((n_chunks * rows, REC_W), BF16), jax.ShapeDtypeStruct((nb, REC_W), F32)],
        scratch_shapes=[pltpu.VMEM((rows + tail, REC_W), F32), pltpu.VMEM((rows, REC_W), F32),
                        pltpu.VMEM((rows, REC_W), F32), pltpu.VMEM((nb, REC_W), F32)],
        compiler_params=_cparams("arbitrary"),
        name="rglru",
    )(xr, yr, conv_prev, h_prev, *wts)
    return rec, h_last


def _merge_kernel(n_p, attp_ref, atts_ref, recp_ref, recs_ref, g_ref, x_ref, wa_ref, wr_ref, wo_ref, o_ref,
                  att_scr, att_rows):
    is_p = pl.program_id(0) < n_p

    @pl.when(is_p)
    def _():
        for b in range(NBV):
            for l in range(ATT_W // LANES):
                att_scr[l, pl.ds(b, TM // NBV, stride=NBV), :] = (
                    attp_ref[:, b * ATT_W + l * LANES:b * ATT_W + (l + 1) * LANES].astype(F32))
        for l in range(ATT_W // LANES):
            att_rows[:, l * LANES:(l + 1) * LANES] = att_scr[l].astype(BF16)

    @pl.when(jnp.logical_not(is_p))
    def _():
        att_rows[...] = atts_ref[...]

    att = att_rows[...]
    rec = jnp.where(is_p, recp_ref[...], recs_ref[...])
    ya = jnp.dot(att, wa_ref[...], preferred_element_type=F32)
    yr = jnp.dot(rec, wr_ref[...], preferred_element_type=F32)
    merged = _sigmoid(g_ref[:, :D_MODEL]) * ya + _sigmoid(g_ref[:, D_MODEL:]) * yr
    o_ref[...] = x_ref[...] + jnp.dot(merged.astype(BF16), wo_ref[...], preferred_element_type=F32)


def _merge(att_p, att_s, rec_p, rec_s, g, x, wa, wr, wo):
    t = x.shape[0]
    n_p = rec_p.shape[0] // TM
    n_s = att_s.shape[0] // TM
    p_spec = lambda w: pl.BlockSpec((TM, w), lambda i: (jnp.minimum(i, n_p - 1), 0))
    s_spec = lambda w: pl.BlockSpec((TM, w), lambda i: (jnp.clip(i - n_p, 0, n_s - 1), 0))
    row = lambda w: pl.BlockSpec((TM, w), lambda i: (i, 0))
    view_spec = pl.BlockSpec((TM // NBV, NBV * ATT_W), lambda i: (jnp.minimum(i, n_p - 1), 0))
    return pl.pallas_call(
        functools.partial(_merge_kernel, n_p),
        grid=(t // TM,),
        in_specs=[view_spec, s_spec(ATT_W), p_spec(REC_W), s_spec(REC_W), row(2 * D_MODEL), row(D_MODEL),
                  _resident(wa.shape), _resident(wr.shape), _resident(wo.shape)],
        out_specs=row(D_MODEL),
        out_shape=jax.ShapeDtypeStruct((t, D_MODEL), F32),
        scratch_shapes=[pltpu.VMEM((ATT_W // LANES, TM, LANES), F32), pltpu.VMEM((TM, ATT_W), BF16)],
        compiler_params=_cparams("parallel"),
        name="merge_out",
    )(att_p, att_s, rec_p, rec_s, g, x, wa, wr, wo)


FF_CHUNK = 512


def _swiglu_mid(hn, wg_ref, wu_ref, mid_ref, width):
    for c in range(0, width, FF_CHUNK):
        gate = jnp.dot(hn, wg_ref[:, c:c + FF_CHUNK], preferred_element_type=F32)
        up = jnp.dot(hn, wu_ref[:, c:c + FF_CHUNK], preferred_element_type=F32)
        mid_ref[:, c:c + FF_CHUNK] = (jax.nn.silu(gate) * up).astype(BF16)


def _ffn_kernel(x_ref, ln_ref, wg_ref, wu_ref, wd_ref, o_ref, mid_ref):
    x = x_ref[...]
    hn = _rms(x, ln_ref[...]).astype(BF16)
    _swiglu_mid(hn, wg_ref, wu_ref, mid_ref, D_FF)
    o_ref[...] = x + jnp.dot(mid_ref[...], wd_ref[...], preferred_element_type=F32)


def _ffn(x, ln, wg, wu, wd):
    t = x.shape[0]
    row = pl.BlockSpec((TM, D_MODEL), lambda i: (i, 0))
    return pl.pallas_call(
        _ffn_kernel,
        grid=(t // TM,),
        in_specs=[row, _resident((1, D_MODEL)), _resident(wg.shape), _resident(wu.shape), _resident(wd.shape)],
        out_specs=row,
        out_shape=jax.ShapeDtypeStruct((t, D_MODEL), F32),
        scratch_shapes=[pltpu.VMEM((TM, D_FF), BF16)],
        compiler_params=_cparams("parallel"),
        name="ffn_dense",
    )(x, ln, wg, wu, wd)


TMG = 512
TOP_K = 2
DISPATCH_ROWS = 512
ZERO_TILES = 2 * N_EXPERTS
INFO_E, INFO_RANK, INFO_W = 0, 2, 4


def _route_kernel(x_ref, ln_ref, wr_ref, br_ref, info_ref, cnt_ref, base_scr):
    @pl.when(pl.program_id(0) == 0)
    def _():
        base_scr[...] = jnp.zeros_like(base_scr)

    hn = _rms(x_ref[...], ln_ref[...])
    lane = lax.broadcasted_iota(jnp.int32, (TM, LANES), 1).astype(F32)
    logits = jnp.dot(hn.astype(BF16), wr_ref[...], preferred_element_type=F32) + br_ref[...]
    m1 = jnp.max(logits, axis=-1, keepdims=True)
    i1 = jnp.min(jnp.where(logits == m1, lane, float(LANES)), axis=-1, keepdims=True)
    rest = jnp.where(lane == i1, -jnp.inf, logits)
    m2 = jnp.max(rest, axis=-1, keepdims=True)
    i2 = jnp.min(jnp.where(rest == m2, lane, float(LANES)), axis=-1, keepdims=True)
    e2 = jnp.exp(m2 - m1)
    den = 1.0 + e2
    hit1 = lane == i1
    hit2 = lane == i2
    onehot = jnp.where(hit1 | hit2, 1.0, 0.0)
    r = lax.broadcasted_iota(jnp.int32, (TM, TM), 0)
    c = lax.broadcasted_iota(jnp.int32, (TM, TM), 1)
    before = jnp.dot(jnp.where(c < r, 1.0, 0.0).astype(BF16), onehot.astype(BF16), preferred_element_type=F32)
    before = before + base_scr[0:1, :]
    rank1 = jnp.sum(jnp.where(hit1, before, 0.0), axis=-1, keepdims=True)
    rank2 = jnp.sum(jnp.where(hit2, before, 0.0), axis=-1, keepdims=True)
    base_scr[0:1, :] = base_scr[0:1, :] + jnp.sum(onehot, axis=0, keepdims=True)
    cnt_ref[...] = base_scr[...]
    info = jnp.zeros((TM, LANES), F32)
    for k, v in ((INFO_E, i1), (INFO_E + 1, i2), (INFO_RANK, rank1), (INFO_RANK + 1, rank2),
                 (INFO_W, 1.0 / den), (INFO_W + 1, e2 / den)):
        info = jnp.where(lane == float(k), v, info)
    info_ref[...] = info


def _route(x, ln, w_router, b_router):
    t = x.shape[0]
    row = lambda w: pl.BlockSpec((TM, w), lambda i: (i, 0))
    return pl.pallas_call(
        _route_kernel,
        grid=(t // TM,),
        in_specs=[row(D_MODEL), _resident((1, D_MODEL)), _resident(w_router.shape), _resident(b_router.shape)],
        out_specs=[row(LANES), pl.BlockSpec((8, LANES), lambda i: (0, 0))],
        out_shape=[jax.ShapeDtypeStruct((t, LANES), F32), jax.ShapeDtypeStruct((8, LANES), F32)],
        scratch_shapes=[pltpu.VMEM((8, LANES), F32)],
        compiler_params=_cparams("arbitrary"),
        name="moe_route",
    )(x, ln, w_router, b_router)


def _dispatch_kernel(zt_ref, pos_hbm, x_ref, ln_ref, xs_out, hn_ref, pos_smem, idx_sem, row_sem, zero_sem):
    i = pl.program_id(0)

    @pl.when(i == 0)
    def _():
        hn_ref[...] = jnp.zeros_like(hn_ref)
        zero = lambda m: pltpu.make_async_copy(
            hn_ref, xs_out.at[pl.ds(pl.multiple_of(zt_ref[m] * TMG, TMG), TMG)], zero_sem)
        for m in range(ZERO_TILES):
            @pl.when(zt_ref[m] >= 0)
            def _():
                zero(m).start()
        for m in range(ZERO_TILES):
            @pl.when(zt_ref[m] >= 0)
            def _():
                zero(m).wait()

    hn_ref[...] = _rms(x_ref[...], ln_ref[...])
    idx_copy = pltpu.make_async_copy(pos_hbm.at[i], pos_smem, idx_sem)
    idx_copy.start()
    idx_copy.wait()

    def issue(j, carry):
        base = pl.multiple_of(j * 8, 8)
        for u in range(8):
            for k in range(TOP_K):
                pltpu.make_async_copy(hn_ref.at[pl.ds(base + u, 1)],
                                      xs_out.at[pl.ds(pos_smem[TOP_K * (base + u) + k], 1)],
                                      row_sem).start(priority=k)
        return carry

    lax.fori_loop(0, DISPATCH_ROWS // 8, issue, 0)
    for k in range(TOP_K):
        pltpu.make_async_copy(hn_ref, xs_out.at[pl.ds(0, DISPATCH_ROWS)], row_sem).wait()


def _dispatch(zero_tiles, pos, x, ln, n_rows):
    t = x.shape[0]
    steps = t // DISPATCH_ROWS
    assert DISPATCH_ROWS == TMG
    return pl.pallas_call(
        _dispatch_kernel,
        grid_spec=pltpu.PrefetchScalarGridSpec(
            num_scalar_prefetch=1,
            grid=(steps,),
            in_specs=[pl.BlockSpec(memory_space=pl.ANY),
                      pl.BlockSpec((DISPATCH_ROWS, D_MODEL), lambda i, zt: (i, 0)),
                      pl.BlockSpec((1, D_MODEL), lambda i, zt: (0, 0))],
            out_specs=pl.BlockSpec(memory_space=pl.ANY),
            scratch_shapes=[pltpu.VMEM((DISPATCH_ROWS, D_MODEL), F32),
                            pltpu.SMEM((TOP_K * DISPATCH_ROWS,), jnp.int32), pltpu.SemaphoreType.DMA,
                            pltpu.SemaphoreType.DMA, pltpu.SemaphoreType.DMA]),
        out_shape=jax.ShapeDtypeStruct((n_rows, D_MODEL), F32),
        compiler_params=_cparams("arbitrary"),
        name="moe_dispatch",
    )(zero_tiles, pos.reshape(steps, TOP_K * DISPATCH_ROWS), x, ln)


def _expert_kernel(te_ref, nu_ref, xs_ref, wg_ref, wu_ref, wd_ref, y_ref, mid_ref):
    del te_ref

    @pl.when(pl.program_id(0) < nu_ref[0])
    def _():
        _swiglu_mid(xs_ref[...].astype(BF16), wg_ref, wu_ref, mid_ref, EXPERT_FF)
        y_ref[...] = jnp.dot(mid_ref[...], wd_ref[...], preferred_element_type=F32)

    @pl.when(pl.program_id(0) >= nu_ref[0])
    def _():
        y_ref[...] = jnp.zeros_like(y_ref)


def _experts(tile_expert, n_used, xs, wg, wu, wd):
    n_rows = xs.shape[0]
    row = pl.BlockSpec((TMG, D_MODEL), lambda i, te, nu: (i, 0))
    ex = lambda a, b: pl.BlockSpec((None, a, b), lambda i, te, nu: (te[i], 0, 0))
    return pl.pallas_call(
        _expert_kernel,
        grid_spec=pltpu.PrefetchScalarGridSpec(
            num_scalar_prefetch=2,
            grid=(n_rows // TMG,),
            in_specs=[row, ex(D_MODEL, EXPERT_FF), ex(D_MODEL, EXPERT_FF), ex(EXPERT_FF, D_MODEL)],
            out_specs=row,
            scratch_shapes=[pltpu.VMEM((TMG, EXPERT_FF), BF16)]),
        out_shape=jax.ShapeDtypeStruct((n_rows, D_MODEL), F32),
        compiler_params=_cparams("arbitrary"),
        name="moe_experts",
    )(tile_expert, n_used, xs, wg, wu, wd)


def _combine_kernel(final, pos_hbm, x_ref, info_ref, ln_ref, y_hbm, *rest):
    if final:
        yp_ref, ys_ref, pos_smem, buf, idx_sem, row_sem, out_scr = rest
    else:
        o_ref, pos_smem, buf, idx_sem, row_sem = rest
    i = pl.program_id(0)
    n = pl.num_programs(0)

    def gather(tile, slot):
        idx_copy = pltpu.make_async_copy(pos_hbm.at[tile], pos_smem.at[slot], idx_sem)
        idx_copy.start()
        idx_copy.wait()

        def issue(j, carry):
            base = pl.multiple_of(j * 8, 8)
            for u in range(8):
                for k in range(TOP_K):
                    pltpu.make_async_copy(y_hbm.at[pl.ds(pos_smem[slot, TOP_K * (base + u) + k], 1)],
                                          buf.at[slot, k, pl.ds(base + u, 1)], row_sem.at[slot]).start(priority=k)
            return carry

        lax.fori_loop(0, TM // 8, issue, 0)

    @pl.when(i == 0)
    def _():
        gather(0, 0)

    for par in range(2):
        @pl.when((i + 1 < n) & ((i + 1) % 2 == par))
        def _():
            gather(i + 1, par)

    slot = i % 2
    for k in range(TOP_K):
        pltpu.make_async_copy(y_hbm.at[pl.ds(0, TM)], buf.at[slot, k], row_sem.at[slot]).wait()
    lane = lax.broadcasted_iota(jnp.int32, (TM, LANES), 1)
    info = info_ref[...]
    out = x_ref[...]
    for k in range(TOP_K):
        w_k = jnp.sum(jnp.where(lane == INFO_W + k, info, 0.0), axis=-1, keepdims=True)
        out = out + w_k * buf[slot, k]
    if not final:
        o_ref[...] = out
        return
    skip, n_p = final
    res = _rms(out, ln_ref[...])

    @pl.when((i >= skip) & (i < n_p))
    def _():
        for l in range(D_MODEL // LANES):
            out_scr[l] = res[:, l * LANES:(l + 1) * LANES]
        for l in range(D_MODEL // LANES):
            for b in range(NBV):
                yp_ref[b, :, l * LANES:(l + 1) * LANES] = out_scr[l, pl.ds(b, TM // NBV, stride=NBV), :]

    @pl.when(i >= n_p)
    def _():
        ys_ref[...] = res


def _combine(pos, x, info, y, final):
    t = x.shape[0]
    steps = t // TM
    row = lambda w: pl.BlockSpec((TM, w), lambda i: (i, 0))
    scratch = [pltpu.SMEM((2, TOP_K * TM), jnp.int32), pltpu.VMEM((2, TOP_K, TM, D_MODEL), F32),
               pltpu.SemaphoreType.DMA, pltpu.SemaphoreType.DMA((2,))]
    if final is None:
        ln, mode = jnp.ones((1, D_MODEL), F32), None
        out_specs = row(D_MODEL)
        out_shape = jax.ShapeDtypeStruct((t, D_MODEL), F32)
    else:
        ln, skip, n_p, n_s, bp, seq = final
        mode = (skip, n_p)
        out_specs = [pl.BlockSpec((NBV, TM // NBV, D_MODEL), lambda i: (0, jnp.clip(i - skip, 0, n_p - skip - 1), 0)),
                     pl.BlockSpec((TM, D_MODEL), lambda i: (jnp.clip(i - n_p, 0, n_s - 1), 0))]
        out_shape = [jax.ShapeDtypeStruct((bp, seq, D_MODEL), F32), jax.ShapeDtypeStruct((n_s * TM, D_MODEL), F32)]
        scratch = scratch + [pltpu.VMEM((D_MODEL // LANES, TM, LANES), F32)]
    return pl.pallas_call(
        functools.partial(_combine_kernel, mode),
        grid=(steps,),
        in_specs=[pl.BlockSpec(memory_space=pl.ANY), row(D_MODEL), row(LANES), _resident((1, D_MODEL)),
                  pl.BlockSpec(memory_space=pl.ANY)],
        out_specs=out_specs,
        out_shape=out_shape,
        scratch_shapes=scratch,
        compiler_params=_cparams("arbitrary"),
        name="moe_combine",
    )(pos.reshape(steps, TOP_K * TM), x, info, ln, y)


def _moe(x, ln, w_router, b_router, wg, wu, wd, final):
    t = x.shape[0]
    info, cnt = _route(x, ln, w_router, b_router)
    counts = cnt[0, :N_EXPERTS].astype(jnp.int32)
    tiles = (counts + TMG - 1) // TMG
    tile_end = jnp.cumsum(tiles)
    offsets = (tile_end - tiles) * TMG
    n_tiles = TOP_K * t // TMG + N_EXPERTS
    tile_expert = jnp.minimum(jnp.sum(jnp.arange(n_tiles)[:, None] >= tile_end[None, :], axis=1),
                              N_EXPERTS - 1).astype(jnp.int32)
    n_used = tile_end[N_EXPERTS - 1:].astype(jnp.int32)
    e = info[:, INFO_E:INFO_E + TOP_K].astype(jnp.int32)
    pos = (jnp.sum(jnp.where(e[..., None] == jnp.arange(N_EXPERTS), offsets, 0), axis=-1)
           + info[:, INFO_RANK:INFO_RANK + TOP_K].astype(jnp.int32)).reshape(-1)
    last_tile = jnp.where(tiles > 0, tile_end - 1, -1)
    spare = n_used[0] + jnp.arange(N_EXPERTS)
    zero_tiles = jnp.concatenate([last_tile, jnp.where(spare < n_tiles, spare, -1)]).astype(jnp.int32)
    xs = _dispatch(zero_tiles, pos, x, ln, n_tiles * TMG)
    y = _experts(tile_expert, n_used, xs, wg, wu, wd)
    return _combine(pos, x, info, y, final)


def _pack_kernel(skip, n_p, head_ref, xp_ref, xs_ref, o_ref, scr):
    i = pl.program_id(0)

    @pl.when(i < skip)
    def _():
        o_ref[...] = head_ref[...]

    @pl.when((i >= skip) & (i < n_p))
    def _():
        for l in range(D_MODEL // LANES):
            for b in range(NBV):
                scr[l, pl.ds(b, TM // NBV, stride=NBV), :] = xp_ref[b, :, l * LANES:(l + 1) * LANES]
        for l in range(D_MODEL // LANES):
            o_ref[:, l * LANES:(l + 1) * LANES] = scr[l]

    @pl.when(i >= n_p)
    def _():
        o_ref[...] = xs_ref[...]


def _pack_tokens(head, x_prompt, xs_tm):
    bp, seq, _ = x_prompt.shape
    skip = head.shape[0] // TM
    n_p = skip + bp * seq // TM
    n_s = xs_tm.shape[0] // TM
    return pl.pallas_call(
        functools.partial(_pack_kernel, skip, n_p),
        grid=(n_p + n_s,),
        in_specs=[pl.BlockSpec((TM, D_MODEL), lambda i: (jnp.minimum(i, skip - 1), 0)),
                  pl.BlockSpec((NBV, TM // NBV, D_MODEL), lambda i: (0, jnp.clip(i - skip, 0, n_p - skip - 1), 0)),
                  pl.BlockSpec((TM, D_MODEL), lambda i: (jnp.clip(i - n_p, 0, n_s - 1), 0))],
        out_specs=pl.BlockSpec((TM, D_MODEL), lambda i: (i, 0)),
        out_shape=jax.ShapeDtypeStruct(((n_p + n_s) * TM, D_MODEL), F32),
        scratch_shapes=[pltpu.VMEM((D_MODEL // LANES, TM, LANES), F32)],
        compiler_params=_cparams("parallel"),
        name="pack_tokens",
    )(head, x_prompt, xs_tm)


def _final_norm_kernel(x_ref, ln_ref, o_ref):
    o_ref[...] = _rms(x_ref[...], ln_ref[...])


def _final_norm(x, ln):
    t = x.shape[0]
    row = pl.BlockSpec((TM, D_MODEL), lambda i: (i, 0))
    return pl.pallas_call(
        _final_norm_kernel,
        grid=(t // TM,),
        in_specs=[row, _resident((1, D_MODEL))],
        out_specs=row,
        out_shape=jax.ShapeDtypeStruct((t, D_MODEL), F32),
        compiler_params=_cparams("parallel"),
        name="final_norm",
    )(x, ln)


def _gate_weights(w_a, w_i):
    half_blocks = REC_BLOCKS // 2
    eye = jnp.eye(half_blocks, dtype=F32)

    def dense(w):
        return (eye[:, None, :, None] * w[:, :, None, :]).reshape(REC_HALF, REC_HALF)

    halves = [jnp.concatenate([dense(w_a[c * half_blocks:(c + 1) * half_blocks]),
                               dense(w_i[c * half_blocks:(c + 1) * half_blocks])], axis=1) for c in range(2)]
    return jnp.stack(halves).astype(BF16)


def kernel(x_prompt, x_sample, cache_k, cache_v, state_conv, state_rglru, meta_tokens, ln_mix, w_in, attn_sinks,
           w_conv, b_conv, w_rg_a, b_rg_a, w_rg_i, b_rg_i, rg_lambda, w_att_proj, w_rec_proj, w_out, ln_ffn,
           w_ffn_gate, w_ffn_up, w_ffn_down, w_router, b_router, w_exp_gate, w_exp_up, w_exp_down, ln_final):
    bp, seq, _ = x_prompt.shape
    db, ds, _ = x_sample.shape
    depth = w_in.shape[0]
    pad = (-(N_META + seq)) % BLOCK
    lp = pad + N_META + seq
    tp, ts = lp * bp, ds * db
    assert tp % TM == 0 and ts % TM == 0 and db % SAMPLE_BB == 0 and ds >= CONV_W - 1
    assert seq >= WINDOW and bp & (bp - 1) == 0 and db & (db - 1) == 0 and ds & (ds - 1) == 0
    assert lp % RG_TCN == 0 and tp % ts == 0 and bp == NBV

    head_rows = (pad + N_META) * bp
    assert head_rows % TM == 0 and (seq * bp) % TM == 0
    head = jnp.concatenate([jnp.zeros((pad * bp, D_MODEL), F32), jnp.repeat(meta_tokens, bp, axis=0)], axis=0)
    x = _pack_tokens(head, x_prompt, jnp.transpose(x_sample, (1, 0, 2)).reshape(ts, D_MODEL))

    zeros_conv = jnp.zeros(((CONV_W - 1) * bp, REC_W), F32)
    zeros_h = jnp.zeros((bp, REC_W), F32)
    outs = {k: [] for k in ("kp", "vp", "cp", "hp", "ks", "vs", "cs", "hs")}

    for l in range(depth):
        q2, kv2, xr, yr, g = _in_proj(x, ln_mix[l].reshape(1, D_MODEL), w_in[l].astype(BF16))

        att_p = _attn_prompt(q2, kv2, attn_sinks[l], bp, lp, pad)
        att_s, nk, nv = _attn_sample(q2[lp:].reshape(ts, ATT_W), kv2[lp:].reshape(ts, 2 * KV_W),
                                     cache_k[l].reshape(db, WINDOW, KV_W),
                                     cache_v[l].reshape(db, WINDOW, KV_W), attn_sinks[l], db, ds)

        rg_w = (w_conv[l], b_conv[l].reshape(1, REC_W), _gate_weights(w_rg_a[l], w_rg_i[l]),
                b_rg_a[l].reshape(1, REC_W), b_rg_i[l].reshape(1, REC_W), rg_lambda[l].reshape(1, REC_W))
        rec_p, h_p = _rglru(xr, yr, 0, lp // RG_TCN, bp, RG_TCN, -pad, zeros_conv, zeros_h, rg_w)
        conv_s = jnp.transpose(state_conv[l], (1, 0, 2)).reshape((CONV_W - 1) * db, REC_W)
        rec_s, h_s = _rglru(xr, yr, tp // ts, 1, db, ds, PAST_LEN, conv_s, state_rglru[l], rg_w)

        x = _merge(att_p, att_s, rec_p, rec_s, g, x, w_att_proj[l].astype(BF16), w_rec_proj[l].astype(BF16),
                   w_out[l].astype(BF16))

        j = l // 2
        ln_f = ln_ffn[l].reshape(1, D_MODEL)
        if l % 2 == 0:
            x = _ffn(x, ln_f, w_ffn_gate[j].astype(BF16), w_ffn_up[j].astype(BF16), w_ffn_down[j].astype(BF16))
        else:
            w_r = jnp.zeros((D_MODEL, LANES), F32).at[:, :N_EXPERTS].set(w_router[j]).astype(BF16)
            b_r = jnp.full((1, LANES), ROUTER_PAD_BIAS, F32).at[0, :N_EXPERTS].set(b_router[j])
            x = _moe(x, ln_f, w_r, b_r, w_exp_gate[j].astype(BF16), w_exp_up[j].astype(BF16),
                     w_exp_down[j].astype(BF16),
                     (ln_final.reshape(1, D_MODEL), head_rows // TM, tp // TM, ts // TM, bp, seq)
                     if l == depth - 1 else None)

        kv_p = kv2[lp - WINDOW:lp].reshape(WINDOW, bp, 2 * KV_W)
        outs["kp"].append(jnp.transpose(kv_p[..., :KV_W], (1, 0, 2)).reshape(bp, WINDOW, N_KV_HEADS, HEAD_DIM))
        outs["vp"].append(jnp.transpose(kv_p[..., KV_W:], (1, 0, 2)).reshape(bp, WINDOW, N_KV_HEADS, HEAD_DIM))
        outs["cp"].append(jnp.transpose(xr[tp - (CONV_W - 1) * bp:tp].reshape(CONV_W - 1, bp, REC_W), (1, 0, 2)))
        outs["hp"].append(h_p)
        outs["ks"].append(nk.reshape(db, WINDOW, N_KV_HEADS, HEAD_DIM))
        outs["vs"].append(nv.reshape(db, WINDOW, N_KV_HEADS, HEAD_DIM))
        outs["cs"].append(jnp.transpose(xr[tp + ts - (CONV_W - 1) * db:].reshape(CONV_W - 1, db, REC_W), (1, 0, 2)))
        outs["hs"].append(h_s)

    if (depth - 1) % 2 == 1:
        y_prompt, ys_tm = x
    else:
        y = _final_norm(x, ln_final.reshape(1, D_MODEL))
        y_prompt = jnp.transpose(y[head_rows:tp].reshape(seq, bp, D_MODEL), (1, 0, 2))
        ys_tm = y[tp:]
    y_sample = jnp.transpose(ys_tm.reshape(ds, db, D_MODEL), (1, 0, 2))
    st = {k: jnp.stack(v) for k, v in outs.items()}
    return (y_prompt, y_sample, st["kp"], st["vp"], st["cp"], st["hp"], st["ks"], st["vs"], st["cs"], st["hs"])
```

```python
import functools

import jax
import jax.numpy as jnp
from jax import lax
from jax.experimental import pallas as pl
from jax.experimental.pallas import tpu as pltpu

F32 = jnp.float32
BF16 = jnp.bfloat16

D_MODEL = 1024
N_META = 16
N_HEADS = 16
N_KV_HEADS = 2
HEAD_DIM = D_MODEL // N_HEADS
GROUP = N_HEADS // N_KV_HEADS
ATT_W = N_HEADS * HEAD_DIM
KV_W = N_KV_HEADS * HEAD_DIM
WINDOW = 128
BLOCK = 128
ATT_SCALE = HEAD_DIM ** -0.5
REC_W = 5 * D_MODEL // 4
REC_BLOCKS = 16
REC_BW = REC_W // REC_BLOCKS
REC_HALF = REC_W // 2
CONV_W = 4
LRU_C = 8.0
D_FF = 3 * D_MODEL
N_EXPERTS = 8
EXPERT_FF = 3 * D_MODEL // 2
PAST_LEN = 8192
RMS_EPS = 1e-6
IN_OFFS = (0, ATT_W, ATT_W + 2 * KV_W, ATT_W + 2 * KV_W + REC_W, ATT_W + 2 * KV_W + 2 * REC_W,
           ATT_W + 2 * KV_W + 2 * REC_W + 2 * D_MODEL)
LANES = 128
VMEM_LIMIT = 52 * 1024 * 1024

TM = 512
ROUTER_PAD_BIAS = -1e30


def _cparams(*sem):
    return pltpu.CompilerParams(dimension_semantics=sem, vmem_limit_bytes=VMEM_LIMIT)


def _resident(shape):
    nd = len(shape)
    return pl.BlockSpec(shape, lambda *_: (0,) * nd, pipeline_mode=pl.Buffered(1))


def _rms(x, g):
    return x * lax.rsqrt(jnp.mean(x * x, axis=-1, keepdims=True) + RMS_EPS) * g


SQRT_GUARD = 1e-30


def _sigmoid(x):
    return 0.5 * jnp.tanh(0.5 * x) + 0.5


NBV = 8


def _to_col_view(val, scr, out_ref, dtype):
    rows, width = val.shape
    for l in range(width // LANES):
        scr[l] = val[:, l * LANES:(l + 1) * LANES]
    for b in range(NBV):
        for l in range(width // LANES):
            out_ref[:, b * width + l * LANES:b * width + (l + 1) * LANES] = (
                scr[l, pl.ds(b, rows // NBV, stride=NBV), :].astype(dtype))


def _in_proj_kernel(x_ref, ln_ref, w_ref, q_ref, kv_ref, xr_ref, yr_ref, g_ref, q_scr, kv_scr):
    hn = _rms(x_ref[...], ln_ref[...]).astype(BF16)

    def proj(a, b):
        return jnp.dot(hn, w_ref[:, a:b], preferred_element_type=F32)

    _to_col_view(proj(IN_OFFS[0], IN_OFFS[1]) * ATT_SCALE, q_scr, q_ref, BF16)
    _to_col_view(proj(IN_OFFS[1], IN_OFFS[2]), kv_scr, kv_ref, F32)
    xr_ref[...] = proj(IN_OFFS[2], IN_OFFS[3])
    yr_ref[...] = proj(IN_OFFS[3], IN_OFFS[4])
    g_ref[...] = proj(IN_OFFS[4], IN_OFFS[5])


def _in_proj(x, ln, w_in):
    t = x.shape[0]
    row = lambda w: pl.BlockSpec((TM, w), lambda i: (i, 0))
    view = lambda w: pl.BlockSpec((TM // NBV, NBV * w), lambda i: (i, 0))
    return pl.pallas_call(
        _in_proj_kernel,
        grid=(t // TM,),
        in_specs=[row(D_MODEL), _resident((1, D_MODEL)), _resident(w_in.shape)],
        out_specs=[view(ATT_W), view(2 * KV_W), row(REC_W), row(REC_W), row(2 * D_MODEL)],
        out_shape=[jax.ShapeDtypeStruct((t // NBV, NBV * ATT_W), BF16),
                   jax.ShapeDtypeStruct((t // NBV, NBV * 2 * KV_W), F32),
                   jax.ShapeDtypeStruct((t, REC_W), F32), jax.ShapeDtypeStruct((t, REC_W), F32),
                   jax.ShapeDtypeStruct((t, 2 * D_MODEL), F32)],
        scratch_shapes=[pltpu.VMEM((ATT_W // LANES, TM, LANES), F32), pltpu.VMEM((2 * KV_W // LANES, TM, LANES), F32)],
        compiler_params=_cparams("parallel"),
        name="in_proj",
    )(x, ln, w_in)


def _sink_softmax(s, mask, sk):
    s = jnp.where(mask, s, -jnp.inf)
    m = jnp.maximum(jnp.max(s, axis=-1, keepdims=True), sk)
    p = jnp.exp(s - m)
    return p / (jnp.sum(p, axis=-1, keepdims=True) + jnp.exp(sk - m))


def _attn_prompt_kernel(pad, sink_ref, q_ref, kvp_ref, kvc_ref, o_ref):
    n = pl.program_id(1)
    q = q_ref[...]
    kv = jnp.concatenate([kvp_ref[...], kvc_ref[...]], axis=0).astype(BF16)
    i = lax.broadcasted_iota(jnp.int32, (BLOCK, 2 * BLOCK), 0)
    j = lax.broadcasted_iota(jnp.int32, (BLOCK, 2 * BLOCK), 1)
    mask = (j >= i) & (j <= i + WINDOW) & (j >= pad + BLOCK - n * BLOCK)
    outs = []
    for h in range(N_HEADS):
        kvh = h // GROUP
        qh = q[:, h * HEAD_DIM:(h + 1) * HEAD_DIM]
        k = kv[:, kvh * HEAD_DIM:(kvh + 1) * HEAD_DIM]
        v = kv[:, KV_W + kvh * HEAD_DIM:KV_W + (kvh + 1) * HEAD_DIM]
        s = lax.dot_general(qh, k, (((1,), (1,)), ((), ())), preferred_element_type=F32)
        p = _sink_softmax(s, mask, sink_ref[h])
        outs.append(jnp.dot(p.astype(BF16), v, preferred_element_type=F32))
    o_ref[...] = jnp.concatenate(outs, axis=1).astype(BF16)


def _attn_prompt(q2, kv2, sinks, nb, lp, pad):
    return pl.pallas_call(
        functools.partial(_attn_prompt_kernel, pad),
        grid=(nb, lp // BLOCK),
        in_specs=[pl.BlockSpec(memory_space=pltpu.SMEM),
                  pl.BlockSpec((BLOCK, ATT_W), lambda b, n: (n, b)),
                  pl.BlockSpec((BLOCK, 2 * KV_W), lambda b, n: (jnp.maximum(n - 1, 0), b)),
                  pl.BlockSpec((BLOCK, 2 * KV_W), lambda b, n: (n, b))],
        out_specs=pl.BlockSpec((BLOCK, ATT_W), lambda b, n: (n, b)),
        out_shape=jax.ShapeDtypeStruct((lp, nb * ATT_W), BF16),
        compiler_params=_cparams("parallel", "parallel"),
        name="attn_prompt",
    )(sinks, q2, kv2, kv2)


SAMPLE_BB = 8


def _attn_sample_kernel(ds, sink_ref, q_ref, kvn_ref, ck_ref, cv_ref, o_ref, nk_ref, nv_ref):
    rows = GROUP * ds
    r = lax.broadcasted_iota(jnp.int32, (rows, WINDOW + ds), 0)
    j = lax.broadcasted_iota(jnp.int32, (rows, WINDOW + ds), 1)
    tq = r & (ds - 1)
    mask = (j >= tq) & (j <= tq + WINDOW)
    for bl in range(SAMPLE_BB):
        qb = q_ref[:, bl * ATT_W:(bl + 1) * ATT_W].astype(F32)
        kvb = kvn_ref[:, bl * 2 * KV_W:(bl + 1) * 2 * KV_W]
        ck = ck_ref[bl]
        cv = cv_ref[bl]
        nk_ref[bl] = jnp.concatenate([ck[ds:], kvb[:, :KV_W]], axis=0)
        nv_ref[bl] = jnp.concatenate([cv[ds:], kvb[:, KV_W:]], axis=0)
        pieces = []
        for kvh in range(N_KV_HEADS):
            lo, hi = kvh * HEAD_DIM, (kvh + 1) * HEAD_DIM
            k = jnp.concatenate([ck[:, lo:hi], kvb[:, lo:hi]], axis=0).astype(BF16)
            v = jnp.concatenate([cv[:, lo:hi], kvb[:, KV_W + lo:KV_W + hi]], axis=0).astype(BF16)
            qg = jnp.concatenate(
                [qb[:, (kvh * GROUP + g) * HEAD_DIM:(kvh * GROUP + g + 1) * HEAD_DIM] for g in range(GROUP)],
                axis=0).astype(BF16)
            s = lax.dot_general(qg, k, (((1,), (1,)), ((), ())), preferred_element_type=F32)
            p = _sink_softmax(s, mask, sink_ref[kvh][:, 0:1])
            o = jnp.dot(p.astype(BF16), v, preferred_element_type=F32)
            pieces += [o[g * ds:(g + 1) * ds] for g in range(GROUP)]
        o_ref[:, bl * ATT_W:(bl + 1) * ATT_W] = jnp.concatenate(pieces, axis=1).astype(BF16)


def _attn_sample(q, kv, cache_k, cache_v, sinks, nb, ds):
    q2 = q.reshape(ds, nb * ATT_W)
    kv2 = kv.reshape(ds, nb * 2 * KV_W)
    sink_rows = jnp.broadcast_to(
        jnp.repeat(sinks.reshape(N_KV_HEADS, GROUP), ds, axis=1)[:, :, None], (N_KV_HEADS, GROUP * ds, LANES))
    cache_spec = pl.BlockSpec((SAMPLE_BB, WINDOW, KV_W), lambda i: (i, 0, 0))
    att, nk, nv = pl.pallas_call(
        functools.partial(_attn_sample_kernel, ds),
        grid=(nb // SAMPLE_BB,),
        in_specs=[_resident(sink_rows.shape),
                  pl.BlockSpec((ds, SAMPLE_BB * ATT_W), lambda i: (0, i)),
                  pl.BlockSpec((ds, SAMPLE_BB * 2 * KV_W), lambda i: (0, i)),
                  cache_spec, cache_spec],
        out_specs=[pl.BlockSpec((ds, SAMPLE_BB * ATT_W), lambda i: (0, i)), cache_spec, cache_spec],
        out_shape=[jax.ShapeDtypeStruct((ds, nb * ATT_W), BF16),
                   jax.ShapeDtypeStruct(cache_k.shape, F32), jax.ShapeDtypeStruct(cache_v.shape, F32)],
        compiler_params=_cparams("parallel"),
        name="attn_sample",
    )(sink_rows, q2, kv2, cache_k, cache_v)
    return att.reshape(ds * nb, ATT_W), nk, nv


RG_ROWS = 256
RG_TCN = 64


def _rglru_kernel(nb, tcn, pos0, xr_ref, yr_ref, cprev_ref, hprev_ref, wc_ref, bc_ref, wg_ref, ba_ref, bi_ref,
                  lam_ref, rec_ref, hlast_ref, xbuf, a_scr, b_scr, hc):
    step = pl.program_id(0)
    rows = tcn * nb
    tail = (CONV_W - 1) * nb

    @pl.when(step == 0)
    def _():
        xbuf[0:tail, :] = cprev_ref[...]
        hc[...] = hprev_ref[...]

    @pl.when(step > 0)
    def _():
        xbuf[0:tail, :] = xbuf[rows:rows + tail, :]

    def row_pos(r0, n):
        t_row = lax.shift_right_logical(r0 + lax.broadcasted_iota(jnp.int32, (n, 1), 0), nb.bit_length() - 1)
        return pos0 + step * tcn + t_row

    sub = min(RG_ROWS, rows)
    for r0 in range(0, rows, sub):
        x = xr_ref[r0:r0 + sub, :]
        if pos0 < 0:
            x = jnp.where(row_pos(r0, sub) >= 0, x, 0.0)
        xbuf[tail + r0:tail + r0 + sub, :] = x

    lam = lam_ref[...]
    neg_c_softplus = -LRU_C * (jnp.maximum(-lam, 0.0) + jnp.log1p(jnp.exp(-jnp.abs(lam))))

    for r0 in range(0, rows, sub):
        xc = bc_ref[...]
        for jj in range(CONV_W):
            xc = xc + xbuf[r0 + jj * nb:r0 + jj * nb + sub, :] * wc_ref[jj:jj + 1, :]
        xcb = xc.astype(BF16)
        pos = row_pos(r0, sub)
        for c in range(2):
            lo, hi = c * REC_HALF, (c + 1) * REC_HALF
            gates = jnp.dot(xcb[:, lo:hi], wg_ref[c], preferred_element_type=F32)
            r = _sigmoid(gates[:, :REC_HALF] + ba_ref[:, lo:hi])
            ig = _sigmoid(gates[:, REC_HALF:] + bi_ref[:, lo:hi])
            a = jnp.exp(r * neg_c_softplus[:, lo:hi])
            u = 1.0 - a * a
            mult = u * lax.rsqrt(jnp.maximum(u, SQRT_GUARD))
            if pos0 <= 0:
                mult = jnp.where(pos == 0, 1.0, mult)
            bt = mult * ig * xc[:, lo:hi]
            if pos0 < 0:
                bt = jnp.where(pos >= 0, bt, 0.0)
            a_scr[r0:r0 + sub, lo:hi] = a
            b_scr[r0:r0 + sub, lo:hi] = bt

    if nb * REC_W <= 16 * 1024:
        def body(t, h):
            rr = pl.multiple_of(t * nb, nb)
            h = a_scr[pl.ds(rr, nb), :] * h + b_scr[pl.ds(rr, nb), :]
            b_scr[pl.ds(rr, nb), :] = h
            return h
        hc[...] = lax.fori_loop(0, tcn, body, hc[...])
    else:
        for t in range(tcn):
            h = a_scr[t * nb:(t + 1) * nb, :] * hc[...] + b_scr[t * nb:(t + 1) * nb, :]
            b_scr[t * nb:(t + 1) * nb, :] = h
            hc[...] = h

    hlast_ref[...] = hc[...]
    for r0 in range(0, rows, sub):
        rec_ref[r0:r0 + sub, :] = (b_scr[r0:r0 + sub, :] * jax.nn.gelu(yr_ref[r0:r0 + sub, :])).astype(BF16)


def _rglru(xr, yr, row_block0, n_chunks, nb, tcn, pos0, conv_prev, h_prev, wts):
    rows = tcn * nb
    tail = (CONV_W - 1) * nb
    blk = pl.BlockSpec((rows, REC_W), lambda i: (row_block0 + i, 0))
    rec, h_last = pl.pallas_call(
        functools.partial(_rglru_kernel, nb, tcn, pos0),
        grid=(n_chunks,),
        in_specs=[blk, blk, _resident(conv_prev.shape), _resident(h_prev.shape)] + [_resident(w.shape) for w in wts],
        out_specs=[pl.BlockSpec((rows, REC_W), lambda i: (i, 0)), pl.BlockSpec((nb, REC_W), lambda i: (0, 0))],
        out_shape=[jax.ShapeDtypeStruct((n_chunks * rows, REC_W), BF16), jax.ShapeDtypeStruct((nb, REC_W), F32)],
        scratch_shapes=[pltpu.VMEM((rows + tail, REC_W), F32), pltpu.VMEM((rows, REC_W), F32),
                        pltpu.VMEM((rows, REC_W), F32), pltpu.VMEM((nb, REC_W), F32)],
        compiler_params=_cparams("arbitrary"),
        name="rglru",
    )(xr, yr, conv_prev, h_prev, *wts)
    return rec, h_last


def _merge_kernel(n_p, attp_ref, atts_ref, recp_ref, recs_ref, g_ref, x_ref, wa_ref, wr_ref, wo_ref, o_ref,
                  att_scr, att_rows):
    is_p = pl.program_id(0) < n_p

    @pl.when(is_p)
    def _():
        for b in range(NBV):
            for l in range(ATT_W // LANES):
                att_scr[l, pl.ds(b, TM // NBV, stride=NBV), :] = (
                    attp_ref[:, b * ATT_W + l * LANES:b * ATT_W + (l + 1) * LANES].astype(F32))
        for l in range(ATT_W // LANES):
            att_rows[:, l * LANES:(l + 1) * LANES] = att_scr[l].astype(BF16)

    @pl.when(jnp.logical_not(is_p))
    def _():
        att_rows[...] = atts_ref[...]

    att = att_rows[...]
    rec = jnp.where(is_p, recp_ref[...], recs_ref[...])
    ya = jnp.dot(att, wa_ref[...], preferred_element_type=F32)
    yr = jnp.dot(rec, wr_ref[...], preferred_element_type=F32)
    merged = _sigmoid(g_ref[:, :D_MODEL]) * ya + _sigmoid(g_ref[:, D_MODEL:]) * yr
    o_ref[...] = x_ref[...] + jnp.dot(merged.astype(BF16), wo_ref[...], preferred_element_type=F32)


def _merge(att_p, att_s, rec_p, rec_s, g, x, wa, wr, wo):
    t = x.shape[0]
    n_p = rec_p.shape[0] // TM
    n_s = att_s.shape[0] // TM
    p_spec = lambda w: pl.BlockSpec((TM, w), lambda i: (jnp.minimum(i, n_p - 1), 0))
    s_spec = lambda w: pl.BlockSpec((TM, w), lambda i: (jnp.clip(i - n_p, 0, n_s - 1), 0))
    row = lambda w: pl.BlockSpec((TM, w), lambda i: (i, 0))
    view_spec = pl.BlockSpec((TM // NBV, NBV * ATT_W), lambda i: (jnp.minimum(i, n_p - 1), 0))
    return pl.pallas_call(
        functools.partial(_merge_kernel, n_p),
        grid=(t // TM,),
        in_specs=[view_spec, s_spec(ATT_W), p_spec(REC_W), s_spec(REC_W), row(2 * D_MODEL), row(D_MODEL),
                  _resident(wa.shape), _resident(wr.shape), _resident(wo.shape)],
        out_specs=row(D_MODEL),
        out_shape=jax.ShapeDtypeStruct((t, D_MODEL), F32),
        scratch_shapes=[pltpu.VMEM((ATT_W // LANES, TM, LANES), F32), pltpu.VMEM((TM, ATT_W), BF16)],
        compiler_params=_cparams("parallel"),
        name="merge_out",
    )(att_p, att_s, rec_p, rec_s, g, x, wa, wr, wo)


FF_CHUNK = 512


def _swiglu_mid(hn, wg_ref, wu_ref, mid_ref, width):
    for c in range(0, width, FF_CHUNK):
        gate = jnp.dot(hn, wg_ref[:, c:c + FF_CHUNK].astype(BF16), preferred_element_type=F32)
        up = jnp.dot(hn, wu_ref[:, c:c + FF_CHUNK].astype(BF16), preferred_element_type=F32)
        mid_ref[:, c:c + FF_CHUNK] = (jax.nn.silu(gate) * up).astype(BF16)


def _ffn_kernel(x_ref, ln_ref, wg_ref, wu_ref, wd_ref, o_ref, mid_ref):
    x = x_ref[...]
    hn = _rms(x, ln_ref[...]).astype(BF16)
    _swiglu_mid(hn, wg_ref, wu_ref, mid_ref, D_FF)
    o_ref[...] = x + jnp.dot(mid_ref[...], wd_ref[...], preferred_element_type=F32)


def _ffn(x, ln, wg, wu, wd):
    t = x.shape[0]
    row = pl.BlockSpec((TM, D_MODEL), lambda i: (i, 0))
    return pl.pallas_call(
        _ffn_kernel,
        grid=(t // TM,),
        in_specs=[row, _resident((1, D_MODEL)), _resident(wg.shape), _resident(wu.shape), _resident(wd.shape)],
        out_specs=row,
        out_shape=jax.ShapeDtypeStruct((t, D_MODEL), F32),
        scratch_shapes=[pltpu.VMEM((TM, D_FF), BF16)],
        compiler_params=_cparams("parallel"),
        name="ffn_dense",
    )(x, ln, wg, wu, wd)


TMG = 512
TOP_K = 2
DISPATCH_ROWS = 512
ZERO_TILES = 2 * N_EXPERTS
INFO_E, INFO_RANK, INFO_W = 0, 2, 4


def _route_kernel(x_ref, ln_ref, wr_ref, br_ref, info_ref, cnt_ref, base_scr):
    @pl.when(pl.program_id(0) == 0)
    def _():
        base_scr[...] = jnp.zeros_like(base_scr)

    hn = _rms(x_ref[...], ln_ref[...])
    lane = lax.broadcasted_iota(jnp.int32, (TM, LANES), 1).astype(F32)
    logits = jnp.dot(hn.astype(BF16), wr_ref[...], preferred_element_type=F32) + br_ref[...]
    m1 = jnp.max(logits, axis=-1, keepdims=True)
    i1 = jnp.min(jnp.where(logits == m1, lane, float(LANES)), axis=-1, keepdims=True)
    rest = jnp.where(lane == i1, -jnp.inf, logits)
    m2 = jnp.max(rest, axis=-1, keepdims=True)
    i2 = jnp.min(jnp.where(rest == m2, lane, float(LANES)), axis=-1, keepdims=True)
    e2 = jnp.exp(m2 - m1)
    den = 1.0 + e2
    hit1 = lane == i1
    hit2 = lane == i2
    onehot = jnp.where(hit1 | hit2, 1.0, 0.0)
    r = lax.broadcasted_iota(jnp.int32, (TM, TM), 0)
    c = lax.broadcasted_iota(jnp.int32, (TM, TM), 1)
    before = jnp.dot(jnp.where(c < r, 1.0, 0.0).astype(BF16), onehot.astype(BF16), preferred_element_type=F32)
    before = before + base_scr[0:1, :]
    rank1 = jnp.sum(jnp.where(hit1, before, 0.0), axis=-1, keepdims=True)
    rank2 = jnp.sum(jnp.where(hit2, before, 0.0), axis=-1, keepdims=True)
    base_scr[0:1, :] = base_scr[0:1, :] + jnp.sum(onehot, axis=0, keepdims=True)
    cnt_ref[...] = base_scr[...]
    info = jnp.zeros((TM, LANES), F32)
    for k, v in ((INFO_E, i1), (INFO_E + 1, i2), (INFO_RANK, rank1), (INFO_RANK + 1, rank2),
                 (INFO_W, 1.0 / den), (INFO_W + 1, e2 / den)):
        info = jnp.where(lane == float(k), v, info)
    info_ref[...] = info


def _route(x, ln, w_router, b_router):
    t = x.shape[0]
    row = lambda w: pl.BlockSpec((TM, w), lambda i: (i, 0))
    return pl.pallas_call(
        _route_kernel,
        grid=(t // TM,),
        in_specs=[row(D_MODEL), _resident((1, D_MODEL)), _resident(w_router.shape), _resident(b_router.shape)],
        out_specs=[row(LANES), pl.BlockSpec((8, LANES), lambda i: (0, 0))],
        out_shape=[jax.ShapeDtypeStruct((t, LANES), F32), jax.ShapeDtypeStruct((8, LANES), F32)],
        scratch_shapes=[pltpu.VMEM((8, LANES), F32)],
        compiler_params=_cparams("arbitrary"),
        name="moe_route",
    )(x, ln, w_router, b_router)


def _dispatch_kernel(zt_ref, pos_hbm, x_ref, ln_ref, xs_out, hn_ref, pos_smem, idx_sem, row_sem, zero_sem):
    i = pl.program_id(0)

    @pl.when(i == 0)
    def _():
        hn_ref[...] = jnp.zeros_like(hn_ref)
        zero = lambda m: pltpu.make_async_copy(
            hn_ref, xs_out.at[pl.ds(pl.multiple_of(zt_ref[m] * TMG, TMG), TMG)], zero_sem)
        for m in range(ZERO_TILES):
            @pl.when(zt_ref[m] >= 0)
            def _():
                zero(m).start()
        for m in range(ZERO_TILES):
            @pl.when(zt_ref[m] >= 0)
            def _():
                zero(m).wait()

    hn_ref[...] = _rms(x_ref[...], ln_ref[...])
    idx_copy = pltpu.make_async_copy(pos_hbm.at[i], pos_smem, idx_sem)
    idx_copy.start()
    idx_copy.wait()

    def issue(j, carry):
        base = pl.multiple_of(j * 8, 8)
        for u in range(8):
            for k in range(TOP_K):
                pltpu.make_async_copy(hn_ref.at[pl.ds(base + u, 1)],
                                      xs_out.at[pl.ds(pos_smem[TOP_K * (base + u) + k], 1)],
                                      row_sem).start(priority=k)
        return carry

    lax.fori_loop(0, DISPATCH_ROWS // 8, issue, 0)
    for k in range(TOP_K):
        pltpu.make_async_copy(hn_ref, xs_out.at[pl.ds(0, DISPATCH_ROWS)], row_sem).wait()


def _dispatch(zero_tiles, pos, x, ln, n_rows):
    t = x.shape[0]
    steps = t // DISPATCH_ROWS
    assert DISPATCH_ROWS == TMG
    return pl.pallas_call(
        _dispatch_kernel,
        grid_spec=pltpu.PrefetchScalarGridSpec(
            num_scalar_prefetch=1,
            grid=(steps,),
            in_specs=[pl.BlockSpec(memory_space=pl.ANY),
                      pl.BlockSpec((DISPATCH_ROWS, D_MODEL), lambda i, zt: (i, 0)),
                      pl.BlockSpec((1, D_MODEL), lambda i, zt: (0, 0))],
            out_specs=pl.BlockSpec(memory_space=pl.ANY),
            scratch_shapes=[pltpu.VMEM((DISPATCH_ROWS, D_MODEL), F32),
                            pltpu.SMEM((TOP_K * DISPATCH_ROWS,), jnp.int32), pltpu.SemaphoreType.DMA,
                            pltpu.SemaphoreType.DMA, pltpu.SemaphoreType.DMA]),
        out_shape=jax.ShapeDtypeStruct((n_rows, D_MODEL), F32),
        compiler_params=_cparams("arbitrary"),
        name="moe_dispatch",
    )(zero_tiles, pos.reshape(steps, TOP_K * DISPATCH_ROWS), x, ln)


def _expert_kernel(te_ref, nu_ref, xs_ref, wg_ref, wu_ref, wd_ref, y_ref, mid_ref):
    del te_ref

    @pl.when(pl.program_id(0) < nu_ref[0])
    def _():
        _swiglu_mid(xs_ref[...].astype(BF16), wg_ref, wu_ref, mid_ref, EXPERT_FF)
        y_ref[...] = jnp.dot(mid_ref[...], wd_ref[...].astype(BF16), preferred_element_type=F32)

    @pl.when(pl.program_id(0) >= nu_ref[0])
    def _():
        y_ref[...] = jnp.zeros_like(y_ref)


def _experts(tile_expert, n_used, xs, wg, wu, wd):
    n_rows = xs.shape[0]
    row = pl.BlockSpec((TMG, D_MODEL), lambda i, te, nu: (i, 0))
    ex = lambda a, b: pl.BlockSpec((None, a, b), lambda i, te, nu: (te[i], 0, 0))
    return pl.pallas_call(
        _expert_kernel,
        grid_spec=pltpu.PrefetchScalarGridSpec(
            num_scalar_prefetch=2,
            grid=(n_rows // TMG,),
            in_specs=[row, ex(D_MODEL, EXPERT_FF), ex(D_MODEL, EXPERT_FF), ex(EXPERT_FF, D_MODEL)],
            out_specs=row,
            scratch_shapes=[pltpu.VMEM((TMG, EXPERT_FF), BF16)]),
        out_shape=jax.ShapeDtypeStruct((n_rows, D_MODEL), F32),
        compiler_params=_cparams("arbitrary"),
        name="moe_experts",
    )(tile_expert, n_used, xs, wg, wu, wd)


def _combine_kernel(final, pos_hbm, x_ref, info_ref, ln_ref, y_hbm, *rest):
    if final:
        yp_ref, ys_ref, pos_smem, buf, idx_sem, row_sem, out_scr = rest
    else:
        o_ref, pos_smem, buf, idx_sem, row_sem = rest
    i = pl.program_id(0)
    n = pl.num_programs(0)

    def gather(tile, slot):
        idx_copy = pltpu.make_async_copy(pos_hbm.at[tile], pos_smem.at[slot], idx_sem)
        idx_copy.start()
        idx_copy.wait()

        def issue(j, carry):
            base = pl.multiple_of(j * 8, 8)
            for u in range(8):
                for k in range(TOP_K):
                    pltpu.make_async_copy(y_hbm.at[pl.ds(pos_smem[slot, TOP_K * (base + u) + k], 1)],
                                          buf.at[slot, k, pl.ds(base + u, 1)], row_sem.at[slot]).start(priority=k)
            return carry

        lax.fori_loop(0, TM // 8, issue, 0)

    @pl.when(i == 0)
    def _():
        gather(0, 0)

    for par in range(2):
        @pl.when((i + 1 < n) & ((i + 1) % 2 == par))
        def _():
            gather(i + 1, par)

    slot = i % 2
    for k in range(TOP_K):
        pltpu.make_async_copy(y_hbm.at[pl.ds(0, TM)], buf.at[slot, k], row_sem.at[slot]).wait()
    lane = lax.broadcasted_iota(jnp.int32, (TM, LANES), 1)
    info = info_ref[...]
    out = x_ref[...]
    for k in range(TOP_K):
        w_k = jnp.sum(jnp.where(lane == INFO_W + k, info, 0.0), axis=-1, keepdims=True)
        out = out + w_k * buf[slot, k]
    if not final:
        o_ref[...] = out
        return
    skip, n_p = final
    res = _rms(out, ln_ref[...])

    @pl.when((i >= skip) & (i < n_p))
    def _():
        for l in range(D_MODEL // LANES):
            out_scr[l] = res[:, l * LANES:(l + 1) * LANES]
        for l in range(D_MODEL // LANES):
            for b in range(NBV):
                yp_ref[b, :, l * LANES:(l + 1) * LANES] = out_scr[l, pl.ds(b, TM // NBV, stride=NBV), :]

    @pl.when(i >= n_p)
    def _():
        ys_ref[...] = res


def _combine(pos, x, info, y, final):
    t = x.shape[0]
    steps = t // TM
    row = lambda w: pl.BlockSpec((TM, w), lambda i: (i, 0))
    scratch = [pltpu.SMEM((2, TOP_K * TM), jnp.int32), pltpu.VMEM((2, TOP_K, TM, D_MODEL), F32),
               pltpu.SemaphoreType.DMA, pltpu.SemaphoreType.DMA((2,))]
    if final is None:
        ln, mode = jnp.ones((1, D_MODEL), F32), None
        out_specs = row(D_MODEL)
        out_shape = jax.ShapeDtypeStruct((t, D_MODEL), F32)
    else:
        ln, skip, n_p, n_s, bp, seq = final
        mode = (skip, n_p)
        out_specs = [pl.BlockSpec((NBV, TM // NBV, D_MODEL), lambda i: (0, jnp.clip(i - skip, 0, n_p - skip - 1), 0)),
                     pl.BlockSpec((TM, D_MODEL), lambda i: (jnp.clip(i - n_p, 0, n_s - 1), 0))]
        out_shape = [jax.ShapeDtypeStruct((bp, seq, D_MODEL), F32), jax.ShapeDtypeStruct((n_s * TM, D_MODEL), F32)]
        scratch = scratch + [pltpu.VMEM((D_MODEL // LANES, TM, LANES), F32)]
    return pl.pallas_call(
        functools.partial(_combine_kernel, mode),
        grid=(steps,),
        in_specs=[pl.BlockSpec(memory_space=pl.ANY), row(D_MODEL), row(LANES), _resident((1, D_MODEL)),
                  pl.BlockSpec(memory_space=pl.ANY)],
        out_specs=out_specs,
        out_shape=out_shape,
        scratch_shapes=scratch,
        compiler_params=_cparams("arbitrary"),
        name="moe_combine",
    )(pos.reshape(steps, TOP_K * TM), x, info, ln, y)


def _moe(x, ln, w_router, b_router, wg, wu, wd, final):
    t = x.shape[0]
    info, cnt = _route(x, ln, w_router, b_router)
    counts = cnt[0, :N_EXPERTS].astype(jnp.int32)
    tiles = (counts + TMG - 1) // TMG
    tile_end = jnp.cumsum(tiles)
    offsets = (tile_end - tiles) * TMG
    n_tiles = TOP_K * t // TMG + N_EXPERTS
    tile_expert = jnp.minimum(jnp.sum(jnp.arange(n_tiles)[:, None] >= tile_end[None, :], axis=1),
                              N_EXPERTS - 1).astype(jnp.int32)
    n_used = tile_end[N_EXPERTS - 1:].astype(jnp.int32)
    e = info[:, INFO_E:INFO_E + TOP_K].astype(jnp.int32)
    pos = (jnp.sum(jnp.where(e[..., None] == jnp.arange(N_EXPERTS), offsets, 0), axis=-1)
           + info[:, INFO_RANK:INFO_RANK + TOP_K].astype(jnp.int32)).reshape(-1)
    last_tile = jnp.where(tiles > 0, tile_end - 1, -1)
    spare = n_used[0] + jnp.arange(N_EXPERTS)
    zero_tiles = jnp.concatenate([last_tile, jnp.where(spare < n_tiles, spare, -1)]).astype(jnp.int32)
    xs = _dispatch(zero_tiles, pos, x, ln, n_tiles * TMG)
    y = _experts(tile_expert, n_used, xs, wg, wu, wd)
    return _combine(pos, x, info, y, final)


def _pack_kernel(skip, n_p, head_ref, xp_ref, xs_ref, o_ref, scr):
    i = pl.program_id(0)

    @pl.when(i < skip)
    def _():
        o_ref[...] = head_ref[...]

    @pl.when((i >= skip) & (i < n_p))
    def _():
        for l in range(D_MODEL // LANES):
            for b in range(NBV):
                scr[l, pl.ds(b, TM // NBV, stride=NBV), :] = xp_ref[b, :, l * LANES:(l + 1) * LANES]
        for l in range(D_MODEL // LANES):
            o_ref[:, l * LANES:(l + 1) * LANES] = scr[l]

    @pl.when(i >= n_p)
    def _():
        o_ref[...] = xs_ref[...]


def _pack_tokens(head, x_prompt, xs_tm):
    bp, seq, _ = x_prompt.shape
    skip = head.shape[0] // TM
    n_p = skip + bp * seq // TM
    n_s = xs_tm.shape[0] // TM
    return pl.pallas_call(
        functools.partial(_pack_kernel, skip, n_p),
        grid=(n_p + n_s,),
        in_specs=[pl.BlockSpec((TM, D_MODEL), lambda i: (jnp.minimum(i, skip - 1), 0)),
                  pl.BlockSpec((NBV, TM // NBV, D_MODEL), lambda i: (0, jnp.clip(i - skip, 0, n_p - skip - 1), 0)),
                  pl.BlockSpec((TM, D_MODEL), lambda i: (jnp.clip(i - n_p, 0, n_s - 1), 0))],
        out_specs=pl.BlockSpec((TM, D_MODEL), lambda i: (i, 0)),
        out_shape=jax.ShapeDtypeStruct(((n_p + n_s) * TM, D_MODEL), F32),
        scratch_shapes=[pltpu.VMEM((D_MODEL // LANES, TM, LANES), F32)],
        compiler_params=_cparams("parallel"),
        name="pack_tokens",
    )(head, x_prompt, xs_tm)


def _final_norm_kernel(x_ref, ln_ref, o_ref):
    o_ref[...] = _rms(x_ref[...], ln_ref[...])


def _final_norm(x, ln):
    t = x.shape[0]
    row = pl.BlockSpec((TM, D_MODEL), lambda i: (i, 0))
    return pl.pallas_call(
        _final_norm_kernel,
        grid=(t // TM,),
        in_specs=[row, _resident((1, D_MODEL))],
        out_specs=row,
        out_shape=jax.ShapeDtypeStruct((t, D_MODEL), F32),
        compiler_params=_cparams("parallel"),
        name="final_norm",
    )(x, ln)


def _gate_weights(w_a, w_i):
    half_blocks = REC_BLOCKS // 2
    eye = jnp.eye(half_blocks, dtype=F32)

    def dense(w):
        return (eye[:, None, :, None] * w[:, :, None, :]).reshape(REC_HALF, REC_HALF)

    halves = [jnp.concatenate([dense(w_a[c * half_blocks:(c + 1) * half_blocks]),
                               dense(w_i[c * half_blocks:(c + 1) * half_blocks])], axis=1) for c in range(2)]
    return jnp.stack(halves).astype(BF16)


def kernel(x_prompt, x_sample, cache_k, cache_v, state_conv, state_rglru, meta_tokens, ln_mix, w_in, attn_sinks,
           w_conv, b_conv, w_rg_a, b_rg_a, w_rg_i, b_rg_i, rg_lambda, w_att_proj, w_rec_proj, w_out, ln_ffn,
           w_ffn_gate, w_ffn_up, w_ffn_down, w_router, b_router, w_exp_gate, w_exp_up, w_exp_down, ln_final):
    bp, seq, _ = x_prompt.shape
    db, ds, _ = x_sample.shape
    depth = w_in.shape[0]
    pad = (-(N_META + seq)) % BLOCK
    lp = pad + N_META + seq
    tp, ts = lp * bp, ds * db
    assert tp % TM == 0 and ts % TM == 0 and db % SAMPLE_BB == 0 and ds >= CONV_W - 1
    assert seq >= WINDOW and bp & (bp - 1) == 0 and db & (db - 1) == 0 and ds & (ds - 1) == 0
    assert lp % RG_TCN == 0 and tp % ts == 0 and bp == NBV

    head_rows = (pad + N_META) * bp
    assert head_rows % TM == 0 and (seq * bp) % TM == 0
    head = jnp.concatenate([jnp.zeros((pad * bp, D_MODEL), F32), jnp.repeat(meta_tokens, bp, axis=0)], axis=0)
    x = _pack_tokens(head, x_prompt, jnp.transpose(x_sample, (1, 0, 2)).reshape(ts, D_MODEL))

    zeros_conv = jnp.zeros(((CONV_W - 1) * bp, REC_W), F32)
    zeros_h = jnp.zeros((bp, REC_W), F32)
    outs = {k: [] for k in ("kp", "vp", "cp", "hp", "ks", "vs", "cs", "hs")}

    for l in range(depth):
        q2, kv2, xr, yr, g = _in_proj(x, ln_mix[l].reshape(1, D_MODEL), w_in[l].astype(BF16))

        att_p = _attn_prompt(q2, kv2, attn_sinks[l], bp, lp, pad)
        att_s, nk, nv = _attn_sample(q2[lp:].reshape(ts, ATT_W), kv2[lp:].reshape(ts, 2 * KV_W),
                                     cache_k[l].reshape(db, WINDOW, KV_W),
                                     cache_v[l].reshape(db, WINDOW, KV_W), attn_sinks[l], db, ds)

        rg_w = (w_conv[l], b_conv[l].reshape(1, REC_W), _gate_weights(w_rg_a[l], w_rg_i[l]),
                b_rg_a[l].reshape(1, REC_W), b_rg_i[l].reshape(1, REC_W), rg_lambda[l].reshape(1, REC_W))
        rec_p, h_p = _rglru(xr, yr, 0, lp // RG_TCN, bp, RG_TCN, -pad, zeros_conv, zeros_h, rg_w)
        conv_s = jnp.transpose(state_conv[l], (1, 0, 2)).reshape((CONV_W - 1) * db, REC_W)
        rec_s, h_s = _rglru(xr, yr, tp // ts, 1, db, ds, PAST_LEN, conv_s, state_rglru[l], rg_w)

        x = _merge(att_p, att_s, rec_p, rec_s, g, x, w_att_proj[l].astype(BF16), w_rec_proj[l].astype(BF16),
                   w_out[l].astype(BF16))

        j = l // 2
        ln_f = ln_ffn[l].reshape(1, D_MODEL)
        if l % 2 == 0:
            x = _ffn(x, ln_f, w_ffn_gate[j].astype(BF16), w_ffn_up[j].astype(BF16), w_ffn_down[j].astype(BF16))
        else:
            w_r = jnp.zeros((D_MODEL, LANES), F32).at[:, :N_EXPERTS].set(w_router[j]).astype(BF16)
            b_r = jnp.full((1, LANES), ROUTER_PAD_BIAS, F32).at[0, :N_EXPERTS].set(b_router[j])
            x = _moe(x, ln_f, w_r, b_r, w_exp_gate[j], w_exp_up[j], w_exp_down[j],
                     (ln_final.reshape(1, D_MODEL), head_rows // TM, tp // TM, ts // TM, bp, seq)
                     if l == depth - 1 else None)

        kv_p = kv2[lp - WINDOW:lp].reshape(WINDOW, bp, 2 * KV_W)
        outs["kp"].append(jnp.transpose(kv_p[..., :KV_W], (1, 0, 2)).reshape(bp, WINDOW, N_KV_HEADS, HEAD_DIM))
        outs["vp"].append(jnp.transpose(kv_p[..., KV_W:], (1, 0, 2)).reshape(bp, WINDOW, N_KV_HEADS, HEAD_DIM))
        outs["cp"].append(jnp.transpose(xr[tp - (CONV_W - 1) * bp:tp].reshape(CONV_W - 1, bp, REC_W), (1, 0, 2)))
        outs["hp"].append(h_p)
        outs["ks"].append(nk.reshape(db, WINDOW, N_KV_HEADS, HEAD_DIM))
        outs["vs"].append(nv.reshape(db, WINDOW, N_KV_HEADS, HEAD_DIM))
        outs["cs"].append(jnp.transpose(xr[tp + ts - (CONV_W - 1) * db:].reshape(CONV_W - 1, db, REC_W), (1, 0, 2)))
        outs["hs"].append(h_s)

    if (depth - 1) % 2 == 1:
        y_prompt, ys_tm = x
    else:
        y = _final_norm(x, ln_final.reshape(1, D_MODEL))
        y_prompt = jnp.transpose(y[head_rows:tp].reshape(seq, bp, D_MODEL), (1, 0, 2))
        ys_tm = y[tp:]
    y_sample = jnp.transpose(ys_tm.reshape(ds, db, D_MODEL), (1, 0, 2))
    st = {k: jnp.stack(v) for k, v in outs.items()}
    return (y_prompt, y_sample, st["kp"], st["vp"], st["cp"], st["hp"], st["ks"], st["vs"], st["cs"], st["hs"])
```

```python
import functools

import jax
import jax.numpy as jnp
from jax import lax
from jax.experimental import pallas as pl
from jax.experimental.pallas import tpu as pltpu

F32 = jnp.float32
BF16 = jnp.bfloat16

D_MODEL = 1024
N_META = 16
N_HEADS = 16
N_KV_HEADS = 2
HEAD_DIM = D_MODEL // N_HEADS
GROUP = N_HEADS // N_KV_HEADS
ATT_W = N_HEADS * HEAD_DIM
KV_W = N_KV_HEADS * HEAD_DIM
WINDOW = 128
BLOCK = 128
ATT_SCALE = HEAD_DIM ** -0.5
REC_W = 5 * D_MODEL // 4
REC_BLOCKS = 16
REC_BW = REC_W // REC_BLOCKS
REC_HALF = REC_W // 2
CONV_W = 4
LRU_C = 8.0
D_FF = 3 * D_MODEL
N_EXPERTS = 8
EXPERT_FF = 3 * D_MODEL // 2
PAST_LEN = 8192
RMS_EPS = 1e-6
IN_OFFS = (0, ATT_W, ATT_W + 2 * KV_W, ATT_W + 2 * KV_W + REC_W, ATT_W + 2 * KV_W + 2 * REC_W,
           ATT_W + 2 * KV_W + 2 * REC_W + 2 * D_MODEL)
LANES = 128
VMEM_LIMIT = 52 * 1024 * 1024

TM = 512
ROUTER_PAD_BIAS = -1e30


def _cparams(*sem):
    return pltpu.CompilerParams(dimension_semantics=sem, vmem_limit_bytes=VMEM_LIMIT)


def _resident(shape):
    nd = len(shape)
    return pl.BlockSpec(shape, lambda *_: (0,) * nd, pipeline_mode=pl.Buffered(1))


def _rms(x, g):
    return x * lax.rsqrt(jnp.mean(x * x, axis=-1, keepdims=True) + RMS_EPS) * g


SQRT_GUARD = 1e-30


def _sigmoid(x):
    return 0.5 * jnp.tanh(0.5 * x) + 0.5


NBV = 8


def _to_col_view(val, scr, out_ref, dtype):
    rows, width = val.shape
    for l in range(width // LANES):
        scr[l] = val[:, l * LANES:(l + 1) * LANES]
    for b in range(NBV):
        for l in range(width // LANES):
            out_ref[:, b * width + l * LANES:b * width + (l + 1) * LANES] = (
                scr[l, pl.ds(b, rows // NBV, stride=NBV), :].astype(dtype))


def _in_proj_kernel(x_ref, ln_ref, w_ref, q_ref, kv_ref, xr_ref, yr_ref, g_ref, q_scr, kv_scr):
    hn = _rms(x_ref[...], ln_ref[...]).astype(BF16)

    def proj(a, b):
        return jnp.dot(hn, w_ref[:, a:b], preferred_element_type=F32)

    _to_col_view(proj(IN_OFFS[0], IN_OFFS[1]) * ATT_SCALE, q_scr, q_ref, BF16)
    _to_col_view(proj(IN_OFFS[1], IN_OFFS[2]), kv_scr, kv_ref, F32)
    xr_ref[...] = proj(IN_OFFS[2], IN_OFFS[3])
    yr_ref[...] = proj(IN_OFFS[3], IN_OFFS[4])
    g_ref[...] = proj(IN_OFFS[4], IN_OFFS[5])


def _in_proj(x, ln, w_in):
    t = x.shape[0]
    row = lambda w: pl.BlockSpec((TM, w), lambda i: (i, 0))
    view = lambda w: pl.BlockSpec((TM // NBV, NBV * w), lambda i: (i, 0))
    return pl.pallas_call(
        _in_proj_kernel,
        grid=(t // TM,),
        in_specs=[row(D_MODEL), _resident((1, D_MODEL)), _resident(w_in.shape)],
        out_specs=[view(ATT_W), view(2 * KV_W), row(REC_W), row(REC_W), row(2 * D_MODEL)],
        out_shape=[jax.ShapeDtypeStruct((t // NBV, NBV * ATT_W), BF16),
                   jax.ShapeDtypeStruct((t // NBV, NBV * 2 * KV_W), F32),
                   jax.ShapeDtypeStruct((t, REC_W), F32), jax.ShapeDtypeStruct((t, REC_W), F32),
                   jax.ShapeDtypeStruct((t, 2 * D_MODEL), F32)],
        scratch_shapes=[pltpu.VMEM((ATT_W // LANES, TM, LANES), F32), pltpu.VMEM((2 * KV_W // LANES, TM, LANES), F32)],
        compiler_params=_cparams("parallel"),
        name="in_proj",
    )(x, ln, w_in)


def _sink_softmax(s, mask, sk):
    s = jnp.where(mask, s, -jnp.inf)
    m = jnp.maximum(jnp.max(s, axis=-1, keepdims=True), sk)
    p = jnp.exp(s - m)
    return p / (jnp.sum(p, axis=-1, keepdims=True) + jnp.exp(sk - m))


def _attn_prompt_kernel(pad, sink_ref, q_ref, kvp_ref, kvc_ref, o_ref):
    n = pl.program_id(1)
    q = q_ref[...]
    kv = jnp.concatenate([kvp_ref[...], kvc_ref[...]], axis=0).astype(BF16)
    i = lax.broadcasted_iota(jnp.int32, (BLOCK, 2 * BLOCK), 0)
    j = lax.broadcasted_iota(jnp.int32, (BLOCK, 2 * BLOCK), 1)
    mask = (j >= i) & (j <= i + WINDOW) & (j >= pad + BLOCK - n * BLOCK)
    outs = []
    for h in range(N_HEADS):
        kvh = h // GROUP
        qh = q[:, h * HEAD_DIM:(h + 1) * HEAD_DIM]
        k = kv[:, kvh * HEAD_DIM:(kvh + 1) * HEAD_DIM]
        v = kv[:, KV_W + kvh * HEAD_DIM:KV_W + (kvh + 1) * HEAD_DIM]
        s = lax.dot_general(qh, k, (((1,), (1,)), ((), ())), preferred_element_type=F32)
        s = jnp.where(mask, s, -jnp.inf)
        sk = sink_ref[h]
        m = jnp.maximum(jnp.max(s, axis=-1, keepdims=True), sk)
        p = jnp.exp(s - m)
        inv = 1.0 / (jnp.sum(p, axis=-1, keepdims=True) + jnp.exp(sk - m))
        outs.append(jnp.dot(p.astype(BF16), v, preferred_element_type=F32) * inv)
    o_ref[...] = jnp.concatenate(outs, axis=1).astype(BF16)


def _attn_prompt(q2, kv2, sinks, nb, lp, pad):
    return pl.pallas_call(
        functools.partial(_attn_prompt_kernel, pad),
        grid=(nb, lp // BLOCK),
        in_specs=[pl.BlockSpec(memory_space=pltpu.SMEM),
                  pl.BlockSpec((BLOCK, ATT_W), lambda b, n: (n, b)),
                  pl.BlockSpec((BLOCK, 2 * KV_W), lambda b, n: (jnp.maximum(n - 1, 0), b)),
                  pl.BlockSpec((BLOCK, 2 * KV_W), lambda b, n: (n, b))],
        out_specs=pl.BlockSpec((BLOCK, ATT_W), lambda b, n: (n, b)),
        out_shape=jax.ShapeDtypeStruct((lp, nb * ATT_W), BF16),
        compiler_params=_cparams("parallel", "parallel"),
        name="attn_prompt",
    )(sinks, q2, kv2, kv2)


SAMPLE_BB = 8


def _attn_sample_kernel(ds, sink_ref, q_ref, kvn_ref, ck_ref, cv_ref, o_ref, nk_ref, nv_ref):
    rows = GROUP * ds
    r = lax.broadcasted_iota(jnp.int32, (rows, WINDOW + ds), 0)
    j = lax.broadcasted_iota(jnp.int32, (rows, WINDOW + ds), 1)
    tq = r & (ds - 1)
    mask = (j >= tq) & (j <= tq + WINDOW)
    for bl in range(SAMPLE_BB):
        qb = q_ref[:, bl * ATT_W:(bl + 1) * ATT_W].astype(F32)
        kvb = kvn_ref[:, bl * 2 * KV_W:(bl + 1) * 2 * KV_W]
        ck = ck_ref[bl]
        cv = cv_ref[bl]
        nk_ref[bl] = jnp.concatenate([ck[ds:], kvb[:, :KV_W]], axis=0)
        nv_ref[bl] = jnp.concatenate([cv[ds:], kvb[:, KV_W:]], axis=0)
        pieces = []
        for kvh in range(N_KV_HEADS):
            lo, hi = kvh * HEAD_DIM, (kvh + 1) * HEAD_DIM
            k = jnp.concatenate([ck[:, lo:hi], kvb[:, lo:hi]], axis=0).astype(BF16)
            v = jnp.concatenate([cv[:, lo:hi], kvb[:, KV_W + lo:KV_W + hi]], axis=0).astype(BF16)
            qg = jnp.concatenate(
                [qb[:, (kvh * GROUP + g) * HEAD_DIM:(kvh * GROUP + g + 1) * HEAD_DIM] for g in range(GROUP)],
                axis=0).astype(BF16)
            s = lax.dot_general(qg, k, (((1,), (1,)), ((), ())), preferred_element_type=F32)
            p = _sink_softmax(s, mask, sink_ref[kvh][:, 0:1])
            o = jnp.dot(p.astype(BF16), v, preferred_element_type=F32)
            pieces += [o[g * ds:(g + 1) * ds] for g in range(GROUP)]
        o_ref[:, bl * ATT_W:(bl + 1) * ATT_W] = jnp.concatenate(pieces, axis=1).astype(BF16)


def _attn_sample(q, kv, cache_k, cache_v, sinks, nb, ds):
    q2 = q.reshape(ds, nb * ATT_W)
    kv2 = kv.reshape(ds, nb * 2 * KV_W)
    sink_rows = jnp.broadcast_to(
        jnp.repeat(sinks.reshape(N_KV_HEADS, GROUP), ds, axis=1)[:, :, None], (N_KV_HEADS, GROUP * ds, LANES))
    cache_spec = pl.BlockSpec((SAMPLE_BB, WINDOW, KV_W), lambda i: (i, 0, 0))
    att, nk, nv = pl.pallas_call(
        functools.partial(_attn_sample_kernel, ds),
        grid=(nb // SAMPLE_BB,),
        in_specs=[_resident(sink_rows.shape),
                  pl.BlockSpec((ds, SAMPLE_BB * ATT_W), lambda i: (0, i)),
                  pl.BlockSpec((ds, SAMPLE_BB * 2 * KV_W), lambda i: (0, i)),
                  cache_spec, cache_spec],
        out_specs=[pl.BlockSpec((ds, SAMPLE_BB * ATT_W), lambda i: (0, i)), cache_spec, cache_spec],
        out_shape=[jax.ShapeDtypeStruct((ds, nb * ATT_W), BF16),
                   jax.ShapeDtypeStruct(cache_k.shape, F32), jax.ShapeDtypeStruct(cache_v.shape, F32)],
        compiler_params=_cparams("parallel"),
        name="attn_sample",
    )(sink_rows, q2, kv2, cache_k, cache_v)
    return att.reshape(ds * nb, ATT_W), nk, nv


RG_ROWS = 256
RG_TCN = 64


def _rglru_kernel(nb, tcn, pos0, xr_ref, yr_ref, cprev_ref, hprev_ref, wc_ref, bc_ref, wg_ref, ba_ref, bi_ref,
                  lam_ref, rec_ref, hlast_ref, xbuf, a_scr, b_scr, hc):
    step = pl.program_id(0)
    rows = tcn * nb
    tail = (CONV_W - 1) * nb

    @pl.when(step == 0)
    def _():
        xbuf[0:tail, :] = cprev_ref[...]
        hc[...] = hprev_ref[...]

    @pl.when(step > 0)
    def _():
        xbuf[0:tail, :] = xbuf[rows:rows + tail, :]

    def row_pos(r0, n):
        t_row = lax.shift_right_logical(r0 + lax.broadcasted_iota(jnp.int32, (n, 1), 0), nb.bit_length() - 1)
        return pos0 + step * tcn + t_row

    sub = min(RG_ROWS, rows)
    for r0 in range(0, rows, sub):
        x = xr_ref[r0:r0 + sub, :]
        if pos0 < 0:
            x = jnp.where(row_pos(r0, sub) >= 0, x, 0.0)
        xbuf[tail + r0:tail + r0 + sub, :] = x

    lam = lam_ref[...]
    neg_c_softplus = -LRU_C * (jnp.maximum(-lam, 0.0) + jnp.log1p(jnp.exp(-jnp.abs(lam))))

    for r0 in range(0, rows, sub):
        xc = bc_ref[...]
        for jj in range(CONV_W):
            xc = xc + xbuf[r0 + jj * nb:r0 + jj * nb + sub, :] * wc_ref[jj:jj + 1, :]
        xcb = xc.astype(BF16)
        pos = row_pos(r0, sub)
        for c in range(2):
            lo, hi = c * REC_HALF, (c + 1) * REC_HALF
            gates = jnp.dot(xcb[:, lo:hi], wg_ref[c], preferred_element_type=F32)
            r = _sigmoid(gates[:, :REC_HALF] + ba_ref[:, lo:hi])
            ig = _sigmoid(gates[:, REC_HALF:] + bi_ref[:, lo:hi])
            a = jnp.exp(r * neg_c_softplus[:, lo:hi])
            u = 1.0 - a * a
            mult = u * lax.rsqrt(jnp.maximum(u, SQRT_GUARD))
            if pos0 <= 0:
                mult = jnp.where(pos > 0, mult, jnp.where(pos == 0, 1.0, 0.0))
            bt = mult * ig * xc[:, lo:hi]
            a_scr[r0:r0 + sub, lo:hi] = a
            b_scr[r0:r0 + sub, lo:hi] = bt

    if nb * REC_W <= 16 * 1024:
        def body(t, h):
            rr = pl.multiple_of(t * nb, nb)
            h = a_scr[pl.ds(rr, nb), :] * h + b_scr[pl.ds(rr, nb), :]
            b_scr[pl.ds(rr, nb), :] = h
            return h
        hc[...] = lax.fori_loop(0, tcn, body, hc[...])
    else:
        for t in range(tcn):
            h = a_scr[t * nb:(t + 1) * nb, :] * hc[...] + b_scr[t * nb:(t + 1) * nb, :]
            b_scr[t * nb:(t + 1) * nb, :] = h
            hc[...] = h

    hlast_ref[...] = hc[...]
    for r0 in range(0, rows, sub):
        rec_ref[r0:r0 + sub, :] = (b_scr[r0:r0 + sub, :] * jax.nn.gelu(yr_ref[r0:r0 + sub, :])).astype(BF16)


def _rglru(xr, yr, row_block0, n_chunks, nb, tcn, pos0, conv_prev, h_prev, wts):
    rows = tcn * nb
    tail = (CONV_W - 1) * nb
    blk = pl.BlockSpec((rows, REC_W), lambda i: (row_block0 + i, 0))
    rec, h_last = pl.pallas_call(
        functools.partial(_rglru_kernel, nb, tcn, pos0),
        grid=(n_chunks,),
        in_specs=[blk, blk, _resident(conv_prev.shape), _resident(h_prev.shape)] + [_resident(w.shape) for w in wts],
        out_specs=[pl.BlockSpec((rows, REC_W), lambda i: (i, 0)), pl.BlockSpec((nb, REC_W), lambda i: (0, 0))],
        out_shape=[jax.ShapeDtypeStruct((n_chunks * rows, REC_W), BF16), jax.ShapeDtypeStruct((nb, REC_W), F32)],
        scratch_shapes=[pltpu.VMEM((rows + tail, REC_W), F32), pltpu.VMEM((rows, REC_W), F32),
                        pltpu.VMEM((rows, REC_W), F32), pltpu.VMEM((nb, REC_W), F32)],
        compiler_params=_cparams("arbitrary"),
        name="rglru",
    )(xr, yr, conv_prev, h_prev, *wts)
    return rec, h_last


def _merge_kernel(n_p, attp_ref, atts_ref, recp_ref, recs_ref, g_ref, x_ref, wa_ref, wr_ref, wo_ref, o_ref,
                  att_scr, att_rows):
    is_p = pl.program_id(0) < n_p

    @pl.when(is_p)
    def _():
        for b in range(NBV):
            for l in range(ATT_W // LANES):
                att_scr[l, pl.ds(b, TM // NBV, stride=NBV), :] = (
                    attp_ref[:, b * ATT_W + l * LANES:b * ATT_W + (l + 1) * LANES].astype(F32))
        for l in range(ATT_W // LANES):
            att_rows[:, l * LANES:(l + 1) * LANES] = att_scr[l].astype(BF16)

    @pl.when(jnp.logical_not(is_p))
    def _():
        att_rows[...] = atts_ref[...]

    att = att_rows[...]
    rec = jnp.where(is_p, recp_ref[...], recs_ref[...])
    ya = jnp.dot(att, wa_ref[...], preferred_element_type=F32)
    yr = jnp.dot(rec, wr_ref[...], preferred_element_type=F32)
    merged = _sigmoid(g_ref[:, :D_MODEL]) * ya + _sigmoid(g_ref[:, D_MODEL:]) * yr
    o_ref[...] = x_ref[...] + jnp.dot(merged.astype(BF16), wo_ref[...], preferred_element_type=F32)


def _merge(att_p, att_s, rec_p, rec_s, g, x, wa, wr, wo):
    t = x.shape[0]
    n_p = rec_p.shape[0] // TM
    n_s = att_s.shape[0] // TM
    p_spec = lambda w: pl.BlockSpec((TM, w), lambda i: (jnp.minimum(i, n_p - 1), 0))
    s_spec = lambda w: pl.BlockSpec((TM, w), lambda i: (jnp.clip(i - n_p, 0, n_s - 1), 0))
    row = lambda w: pl.BlockSpec((TM, w), lambda i: (i, 0))
    view_spec = pl.BlockSpec((TM // NBV, NBV * ATT_W), lambda i: (jnp.minimum(i, n_p - 1), 0))
    return pl.pallas_call(
        functools.partial(_merge_kernel, n_p),
        grid=(t // TM,),
        in_specs=[view_spec, s_spec(ATT_W), p_spec(REC_W), s_spec(REC_W), row(2 * D_MODEL), row(D_MODEL),
                  _resident(wa.shape), _resident(wr.shape), _resident(wo.shape)],
        out_specs=row(D_MODEL),
        out_shape=jax.ShapeDtypeStruct((t, D_MODEL), F32),
        scratch_shapes=[pltpu.VMEM((ATT_W // LANES, TM, LANES), F32), pltpu.VMEM((TM, ATT_W), BF16)],
        compiler_params=_cparams("parallel"),
        name="merge_out",
    )(att_p, att_s, rec_p, rec_s, g, x, wa, wr, wo)


FF_CHUNK = 512


def _swiglu_mid(hn, wg_ref, wu_ref, mid_ref, width):
    for c in range(0, width, FF_CHUNK):
        gate = jnp.dot(hn, wg_ref[:, c:c + FF_CHUNK].astype(BF16), preferred_element_type=F32)
        up = jnp.dot(hn, wu_ref[:, c:c + FF_CHUNK].astype(BF16), preferred_element_type=F32)
        mid_ref[:, c:c + FF_CHUNK] = (jax.nn.silu(gate) * up).astype(BF16)


def _ffn_kernel(x_ref, ln_ref, wg_ref, wu_ref, wd_ref, o_ref, mid_ref):
    x = x_ref[...]
    hn = _rms(x, ln_ref[...]).astype(BF16)
    _swiglu_mid(hn, wg_ref, wu_ref, mid_ref, D_FF)
    o_ref[...] = x + jnp.dot(mid_ref[...], wd_ref[...], preferred_element_type=F32)


def _ffn(x, ln, wg, wu, wd):
    t = x.shape[0]
    row = pl.BlockSpec((TM, D_MODEL), lambda i: (i, 0))
    return pl.pallas_call(
        _ffn_kernel,
        grid=(t // TM,),
        in_specs=[row, _resident((1, D_MODEL)), _resident(wg.shape), _resident(wu.shape), _resident(wd.shape)],
        out_specs=row,
        out_shape=jax.ShapeDtypeStruct((t, D_MODEL), F32),
        scratch_shapes=[pltpu.VMEM((TM, D_FF), BF16)],
        compiler_params=_cparams("parallel"),
        name="ffn_dense",
    )(x, ln, wg, wu, wd)


TMG = 512
TOP_K = 2
DISPATCH_ROWS = 512
ZERO_TILES = 2 * N_EXPERTS
INFO_E, INFO_RANK, INFO_W = 0, 2, 4


def _route_kernel(x_ref, ln_ref, wr_ref, br_ref, info_ref, cnt_ref, base_scr):
    @pl.when(pl.program_id(0) == 0)
    def _():
        base_scr[...] = jnp.zeros_like(base_scr)

    hn = _rms(x_ref[...], ln_ref[...])
    lane = lax.broadcasted_iota(jnp.int32, (TM, LANES), 1).astype(F32)
    logits = jnp.dot(hn.astype(BF16), wr_ref[...], preferred_element_type=F32) + br_ref[...]
    m1 = jnp.max(logits, axis=-1, keepdims=True)
    i1 = jnp.min(jnp.where(logits == m1, lane, float(LANES)), axis=-1, keepdims=True)
    rest = jnp.where(lane == i1, -jnp.inf, logits)
    m2 = jnp.max(rest, axis=-1, keepdims=True)
    i2 = jnp.min(jnp.where(rest == m2, lane, float(LANES)), axis=-1, keepdims=True)
    e2 = jnp.exp(m2 - m1)
    den = 1.0 + e2
    hit1 = lane == i1
    hit2 = lane == i2
    onehot = jnp.where(hit1 | hit2, 1.0, 0.0)
    r = lax.broadcasted_iota(jnp.int32, (TM, TM), 0)
    c = lax.broadcasted_iota(jnp.int32, (TM, TM), 1)
    before = jnp.dot(jnp.where(c < r, 1.0, 0.0).astype(BF16), onehot.astype(BF16), preferred_element_type=F32)
    before = before + base_scr[0:1, :]
    rank1 = jnp.sum(jnp.where(hit1, before, 0.0), axis=-1, keepdims=True)
    rank2 = jnp.sum(jnp.where(hit2, before, 0.0), axis=-1, keepdims=True)
    base_scr[0:1, :] = base_scr[0:1, :] + jnp.sum(onehot, axis=0, keepdims=True)
    cnt_ref[...] = base_scr[...]
    info = jnp.zeros((TM, LANES), F32)
    for k, v in ((INFO_E, i1), (INFO_E + 1, i2), (INFO_RANK, rank1), (INFO_RANK + 1, rank2),
                 (INFO_W, 1.0 / den), (INFO_W + 1, e2 / den)):
        info = jnp.where(lane == float(k), v, info)
    info_ref[...] = info


def _route(x, ln, w_router, b_router):
    t = x.shape[0]
    row = lambda w: pl.BlockSpec((TM, w), lambda i: (i, 0))
    return pl.pallas_call(
        _route_kernel,
        grid=(t // TM,),
        in_specs=[row(D_MODEL), _resident((1, D_MODEL)), _resident(w_router.shape), _resident(b_router.shape)],
        out_specs=[row(LANES), pl.BlockSpec((8, LANES), lambda i: (0, 0))],
        out_shape=[jax.ShapeDtypeStruct((t, LANES), F32), jax.ShapeDtypeStruct((8, LANES), F32)],
        scratch_shapes=[pltpu.VMEM((8, LANES), F32)],
        compiler_params=_cparams("arbitrary"),
        name="moe_route",
    )(x, ln, w_router, b_router)


def _dispatch_kernel(zt_ref, pos_hbm, x_ref, ln_ref, xs_out, hn_ref, pos_smem, idx_sem, row_sem, zero_sem):
    i = pl.program_id(0)

    @pl.when(i == 0)
    def _():
        hn_ref[...] = jnp.zeros_like(hn_ref)
        zero = lambda m: pltpu.make_async_copy(
            hn_ref, xs_out.at[pl.ds(pl.multiple_of(zt_ref[m] * TMG, TMG), TMG)], zero_sem)
        for m in range(ZERO_TILES):
            @pl.when(zt_ref[m] >= 0)
            def _():
                zero(m).start()
        for m in range(ZERO_TILES):
            @pl.when(zt_ref[m] >= 0)
            def _():
                zero(m).wait()

    hn_ref[...] = _rms(x_ref[...], ln_ref[...])
    idx_copy = pltpu.make_async_copy(pos_hbm.at[i], pos_smem, idx_sem)
    idx_copy.start()
    idx_copy.wait()

    def issue(j, carry):
        base = pl.multiple_of(j * 8, 8)
        for u in range(8):
            for k in range(TOP_K):
                pltpu.make_async_copy(hn_ref.at[pl.ds(base + u, 1)],
                                      xs_out.at[pl.ds(pos_smem[TOP_K * (base + u) + k], 1)],
                                      row_sem).start(priority=k)
        return carry

    lax.fori_loop(0, DISPATCH_ROWS // 8, issue, 0)
    for k in range(TOP_K):
        pltpu.make_async_copy(hn_ref, xs_out.at[pl.ds(0, DISPATCH_ROWS)], row_sem).wait()


def _dispatch(zero_tiles, pos, x, ln, n_rows):
    t = x.shape[0]
    steps = t // DISPATCH_ROWS
    assert DISPATCH_ROWS == TMG
    return pl.pallas_call(
        _dispatch_kernel,
        grid_spec=pltpu.PrefetchScalarGridSpec(
            num_scalar_prefetch=1,
            grid=(steps,),
            in_specs=[pl.BlockSpec(memory_space=pl.ANY),
                      pl.BlockSpec((DISPATCH_ROWS, D_MODEL), lambda i, zt: (i, 0)),
                      pl.BlockSpec((1, D_MODEL), lambda i, zt: (0, 0))],
            out_specs=pl.BlockSpec(memory_space=pl.ANY),
            scratch_shapes=[pltpu.VMEM((DISPATCH_ROWS, D_MODEL), F32),
                            pltpu.SMEM((TOP_K * DISPATCH_ROWS,), jnp.int32), pltpu.SemaphoreType.DMA,
                            pltpu.SemaphoreType.DMA, pltpu.SemaphoreType.DMA]),
        out_shape=jax.ShapeDtypeStruct((n_rows, D_MODEL), F32),
        compiler_params=_cparams("arbitrary"),
        name="moe_dispatch",
    )(zero_tiles, pos.reshape(steps, TOP_K * DISPATCH_ROWS), x, ln)


def _expert_kernel(te_ref, nu_ref, xs_ref, wg_ref, wu_ref, wd_ref, y_ref, mid_ref):
    del te_ref

    @pl.when(pl.program_id(0) < nu_ref[0])
    def _():
        _swiglu_mid(xs_ref[...].astype(BF16), wg_ref, wu_ref, mid_ref, EXPERT_FF)
        y_ref[...] = jnp.dot(mid_ref[...], wd_ref[...].astype(BF16), preferred_element_type=F32)

    @pl.when(pl.program_id(0) >= nu_ref[0])
    def _():
        y_ref[...] = jnp.zeros_like(y_ref)


def _experts(tile_expert, n_used, xs, wg, wu, wd):
    n_rows = xs.shape[0]
    row = pl.BlockSpec((TMG, D_MODEL), lambda i, te, nu: (i, 0))
    ex = lambda a, b: pl.BlockSpec((None, a, b), lambda i, te, nu: (te[i], 0, 0))
    return pl.pallas_call(
        _expert_kernel,
        grid_spec=pltpu.PrefetchScalarGridSpec(
            num_scalar_prefetch=2,
            grid=(n_rows // TMG,),
            in_specs=[row, ex(D_MODEL, EXPERT_FF), ex(D_MODEL, EXPERT_FF), ex(EXPERT_FF, D_MODEL)],
            out_specs=row,
            scratch_shapes=[pltpu.VMEM((TMG, EXPERT_FF), BF16)]),
        out_shape=jax.ShapeDtypeStruct((n_rows, D_MODEL), F32),
        compiler_params=_cparams("arbitrary"),
        name="moe_experts",
    )(tile_expert, n_used, xs, wg, wu, wd)


def _combine_kernel(final, pos_hbm, x_ref, info_ref, ln_ref, y_hbm, *rest):
    if final:
        yp_ref, ys_ref, pos_smem, buf, idx_sem, row_sem, out_scr = rest
    else:
        o_ref, pos_smem, buf, idx_sem, row_sem = rest
    i = pl.program_id(0)
    n = pl.num_programs(0)

    def gather(tile, slot):
        idx_copy = pltpu.make_async_copy(pos_hbm.at[tile], pos_smem.at[slot], idx_sem)
        idx_copy.start()
        idx_copy.wait()

        def issue(j, carry):
            base = pl.multiple_of(j * 8, 8)
            for u in range(8):
                for k in range(TOP_K):
                    pltpu.make_async_copy(y_hbm.at[pl.ds(pos_smem[slot, TOP_K * (base + u) + k], 1)],
                                          buf.at[slot, k, pl.ds(base + u, 1)], row_sem.at[slot]).start(priority=k)
            return carry

        lax.fori_loop(0, TM // 8, issue, 0)

    @pl.when(i == 0)
    def _():
        gather(0, 0)

    for par in range(2):
        @pl.when((i + 1 < n) & ((i + 1) % 2 == par))
        def _():
            gather(i + 1, par)

    slot = i % 2
    for k in range(TOP_K):
        pltpu.make_async_copy(y_hbm.at[pl.ds(0, TM)], buf.at[slot, k], row_sem.at[slot]).wait()
    lane = lax.broadcasted_iota(jnp.int32, (TM, LANES), 1)
    info = info_ref[...]
    out = x_ref[...]
    for k in range(TOP_K):
        w_k = jnp.sum(jnp.where(lane == INFO_W + k, info, 0.0), axis=-1, keepdims=True)
        out = out + w_k * buf[slot, k]
    if not final:
        o_ref[...] = out
        return
    skip, n_p = final
    res = _rms(out, ln_ref[...])

    @pl.when((i >= skip) & (i < n_p))
    def _():
        for l in range(D_MODEL // LANES):
            out_scr[l] = res[:, l * LANES:(l + 1) * LANES]
        for l in range(D_MODEL // LANES):
            for b in range(NBV):
                yp_ref[b, :, l * LANES:(l + 1) * LANES] = out_scr[l, pl.ds(b, TM // NBV, stride=NBV), :]

    @pl.when(i >= n_p)
    def _():
        ys_ref[...] = res


def _combine(pos, x, info, y, final):
    t = x.shape[0]
    steps = t // TM
    row = lambda w: pl.BlockSpec((TM, w), lambda i: (i, 0))
    scratch = [pltpu.SMEM((2, TOP_K * TM), jnp.int32), pltpu.VMEM((2, TOP_K, TM, D_MODEL), F32),
               pltpu.SemaphoreType.DMA, pltpu.SemaphoreType.DMA((2,))]
    if final is None:
        ln, mode = jnp.ones((1, D_MODEL), F32), None
        out_specs = row(D_MODEL)
        out_shape = jax.ShapeDtypeStruct((t, D_MODEL), F32)
    else:
        ln, skip, n_p, n_s, bp, seq = final
        mode = (skip, n_p)
        out_specs = [pl.BlockSpec((NBV, TM // NBV, D_MODEL), lambda i: (0, jnp.clip(i - skip, 0, n_p - skip - 1), 0)),
                     pl.BlockSpec((TM, D_MODEL), lambda i: (jnp.clip(i - n_p, 0, n_s - 1), 0))]
        out_shape = [jax.ShapeDtypeStruct((bp, seq, D_MODEL), F32), jax.ShapeDtypeStruct((n_s * TM, D_MODEL), F32)]
        scratch = scratch + [pltpu.VMEM((D_MODEL // LANES, TM, LANES), F32)]
    return pl.pallas_call(
        functools.partial(_combine_kernel, mode),
        grid=(steps,),
        in_specs=[pl.BlockSpec(memory_space=pl.ANY), row(D_MODEL), row(LANES), _resident((1, D_MODEL)),
                  pl.BlockSpec(memory_space=pl.ANY)],
        out_specs=out_specs,
        out_shape=out_shape,
        scratch_shapes=scratch,
        compiler_params=_cparams("arbitrary"),
        name="moe_combine",
    )(pos.reshape(steps, TOP_K * TM), x, info, ln, y)


def _moe(x, ln, w_router, b_router, wg, wu, wd, final):
    t = x.shape[0]
    info, cnt = _route(x, ln, w_router, b_router)
    counts = cnt[0, :N_EXPERTS].astype(jnp.int32)
    tiles = (counts + TMG - 1) // TMG
    tile_end = jnp.cumsum(tiles)
    offsets = (tile_end - tiles) * TMG
    n_tiles = TOP_K * t // TMG + N_EXPERTS
    tile_expert = jnp.minimum(jnp.sum(jnp.arange(n_tiles)[:, None] >= tile_end[None, :], axis=1),
                              N_EXPERTS - 1).astype(jnp.int32)
    n_used = tile_end[N_EXPERTS - 1:].astype(jnp.int32)
    e = info[:, INFO_E:INFO_E + TOP_K].astype(jnp.int32)
    pos = (jnp.sum(jnp.where(e[..., None] == jnp.arange(N_EXPERTS), offsets, 0), axis=-1)
           + info[:, INFO_RANK:INFO_RANK + TOP_K].astype(jnp.int32)).reshape(-1)
    last_tile = jnp.where(tiles > 0, tile_end - 1, -1)
    spare = n_used[0] + jnp.arange(N_EXPERTS)
    zero_tiles = jnp.concatenate([last_tile, jnp.where(spare < n_tiles, spare, -1)]).astype(jnp.int32)
    xs = _dispatch(zero_tiles, pos, x, ln, n_tiles * TMG)
    y = _experts(tile_expert, n_used, xs, wg, wu, wd)
    return _combine(pos, x, info, y, final)


def _pack_kernel(skip, n_p, head_ref, xp_ref, xs_ref, o_ref, scr):
    i = pl.program_id(0)

    @pl.when(i < skip)
    def _():
        o_ref[...] = head_ref[...]

    @pl.when((i >= skip) & (i < n_p))
    def _():
        for l in range(D_MODEL // LANES):
            for b in range(NBV):
                scr[l, pl.ds(b, TM // NBV, stride=NBV), :] = xp_ref[b, :, l * LANES:(l + 1) * LANES]
        for l in range(D_MODEL // LANES):
            o_ref[:, l * LANES:(l + 1) * LANES] = scr[l]

    @pl.when(i >= n_p)
    def _():
        o_ref[...] = xs_ref[...]


def _pack_tokens(head, x_prompt, xs_tm):
    bp, seq, _ = x_prompt.shape
    skip = head.shape[0] // TM
    n_p = skip + bp * seq // TM
    n_s = xs_tm.shape[0] // TM
    return pl.pallas_call(
        functools.partial(_pack_kernel, skip, n_p),
        grid=(n_p + n_s,),
        in_specs=[pl.BlockSpec((TM, D_MODEL), lambda i: (jnp.minimum(i, skip - 1), 0)),
                  pl.BlockSpec((NBV, TM // NBV, D_MODEL), lambda i: (0, jnp.clip(i - skip, 0, n_p - skip - 1), 0)),
                  pl.BlockSpec((TM, D_MODEL), lambda i: (jnp.clip(i - n_p, 0, n_s - 1), 0))],
        out_specs=pl.BlockSpec((TM, D_MODEL), lambda i: (i, 0)),
        out_shape=jax.ShapeDtypeStruct(((n_p + n_s) * TM, D_MODEL), F32),
        scratch_shapes=[pltpu.VMEM((D_MODEL // LANES, TM, LANES), F32)],
        compiler_params=_cparams("parallel"),
        name="pack_tokens",
    )(head, x_prompt, xs_tm)


def _final_norm_kernel(x_ref, ln_ref, o_ref):
    o_ref[...] = _rms(x_ref[...], ln_ref[...])


def _final_norm(x, ln):
    t = x.shape[0]
    row = pl.BlockSpec((TM, D_MODEL), lambda i: (i, 0))
    return pl.pallas_call(
        _final_norm_kernel,
        grid=(t // TM,),
        in_specs=[row, _resident((1, D_MODEL))],
        out_specs=row,
        out_shape=jax.ShapeDtypeStruct((t, D_MODEL), F32),
        compiler_params=_cparams("parallel"),
        name="final_norm",
    )(x, ln)


def _gate_weights(w_a, w_i):
    half_blocks = REC_BLOCKS // 2
    eye = jnp.eye(half_blocks, dtype=F32)

    def dense(w):
        return (eye[:, None, :, None] * w[:, :, None, :]).reshape(REC_HALF, REC_HALF)

    halves = [jnp.concatenate([dense(w_a[c * half_blocks:(c + 1) * half_blocks]),
                               dense(w_i[c * half_blocks:(c + 1) * half_blocks])], axis=1) for c in range(2)]
    return jnp.stack(halves).astype(BF16)


def kernel(x_prompt, x_sample, cache_k, cache_v, state_conv, state_rglru, meta_tokens, ln_mix, w_in, attn_sinks,
           w_conv, b_conv, w_rg_a, b_rg_a, w_rg_i, b_rg_i, rg_lambda, w_att_proj, w_rec_proj, w_out, ln_ffn,
           w_ffn_gate, w_ffn_up, w_ffn_down, w_router, b_router, w_exp_gate, w_exp_up, w_exp_down, ln_final):
    bp, seq, _ = x_prompt.shape
    db, ds, _ = x_sample.shape
    depth = w_in.shape[0]
    pad = (-(N_META + seq)) % BLOCK
    lp = pad + N_META + seq
    tp, ts = lp * bp, ds * db
    assert tp % TM == 0 and ts % TM == 0 and db % SAMPLE_BB == 0 and ds >= CONV_W - 1
    assert seq >= WINDOW and bp & (bp - 1) == 0 and db & (db - 1) == 0 and ds & (ds - 1) == 0
    assert lp % RG_TCN == 0 and tp % ts == 0 and bp == NBV

    head_rows = (pad + N_META) * bp
    assert head_rows % TM == 0 and (seq * bp) % TM == 0
    head = jnp.concatenate([jnp.zeros((pad * bp, D_MODEL), F32), jnp.repeat(meta_tokens, bp, axis=0)], axis=0)
    x = _pack_tokens(head, x_prompt, jnp.transpose(x_sample, (1, 0, 2)).reshape(ts, D_MODEL))

    zeros_conv = jnp.zeros(((CONV_W - 1) * bp, REC_W), F32)
    zeros_h = jnp.zeros((bp, REC_W), F32)
    outs = {k: [] for k in ("kp", "vp", "cp", "hp", "ks", "vs", "cs", "hs")}

    for l in range(depth):
        q2, kv2, xr, yr, g = _in_proj(x, ln_mix[l].reshape(1, D_MODEL), w_in[l].astype(BF16))

        att_p = _attn_prompt(q2, kv2, attn_sinks[l], bp, lp, pad)
        att_s, nk, nv = _attn_sample(q2[lp:].reshape(ts, ATT_W), kv2[lp:].reshape(ts, 2 * KV_W),
                                     cache_k[l].reshape(db, WINDOW, KV_W),
                                     cache_v[l].reshape(db, WINDOW, KV_W), attn_sinks[l], db, ds)

        rg_w = (w_conv[l], b_conv[l].reshape(1, REC_W), _gate_weights(w_rg_a[l], w_rg_i[l]),
                b_rg_a[l].reshape(1, REC_W), b_rg_i[l].reshape(1, REC_W), rg_lambda[l].reshape(1, REC_W))
        rec_p, h_p = _rglru(xr, yr, 0, lp // RG_TCN, bp, RG_TCN, -pad, zeros_conv, zeros_h, rg_w)
        conv_s = jnp.transpose(state_conv[l], (1, 0, 2)).reshape((CONV_W - 1) * db, REC_W)
        rec_s, h_s = _rglru(xr, yr, tp // ts, 1, db, ds, PAST_LEN, conv_s, state_rglru[l], rg_w)

        x = _merge(att_p, att_s, rec_p, rec_s, g, x, w_att_proj[l].astype(BF16), w_rec_proj[l].astype(BF16),
                   w_out[l].astype(BF16))

        j = l // 2
        ln_f = ln_ffn[l].reshape(1, D_MODEL)
        if l % 2 == 0:
            x = _ffn(x, ln_f, w_ffn_gate[j].astype(BF16), w_ffn_up[j].astype(BF16), w_ffn_down[j].astype(BF16))
        else:
            w_r = jnp.zeros((D_MODEL, LANES), F32).at[:, :N_EXPERTS].set(w_router[j]).astype(BF16)
            b_r = jnp.full((1, LANES), ROUTER_PAD_BIAS, F32).at[0, :N_EXPERTS].set(b_router[j])
            x = _moe(x, ln_f, w_r, b_r, w_exp_gate[j], w_exp_up[j], w_exp_down[j],
                     (ln_final.reshape(1, D_MODEL), head_rows // TM, tp // TM, ts // TM, bp, seq)
                     if l == depth - 1 else None)

        kv_p = kv2[lp - WINDOW:lp].reshape(WINDOW, bp, 2 * KV_W)
        outs["kp"].append(jnp.transpose(kv_p[..., :KV_W], (1, 0, 2)).reshape(bp, WINDOW, N_KV_HEADS, HEAD_DIM))
        outs["vp"].append(jnp.transpose(kv_p[..., KV_W:], (1, 0, 2)).reshape(bp, WINDOW, N_KV_HEADS, HEAD_DIM))
        outs["cp"].append(jnp.transpose(xr[tp - (CONV_W - 1) * bp:tp].reshape(CONV_W - 1, bp, REC_W), (1, 0, 2)))
        outs["hp"].append(h_p)
        outs["ks"].append(nk.reshape(db, WINDOW, N_KV_HEADS, HEAD_DIM))
        outs["vs"].append(nv.reshape(db, WINDOW, N_KV_HEADS, HEAD_DIM))
        outs["cs"].append(jnp.transpose(xr[tp + ts - (CONV_W - 1) * db:].reshape(CONV_W - 1, db, REC_W), (1, 0, 2)))
        outs["hs"].append(h_s)

    if (depth - 1) % 2 == 1:
        y_prompt, ys_tm = x
    else:
        y = _final_norm(x, ln_final.reshape(1, D_MODEL))
        y_prompt = jnp.transpose(y[head_rows:tp].reshape(seq, bp, D_MODEL), (1, 0, 2))
        ys_tm = y[tp:]
    y_sample = jnp.transpose(ys_tm.reshape(ds, db, D_MODEL), (1, 0, 2))
    st = {k: jnp.stack(v) for k, v in outs.items()}
    return (y_prompt, y_sample, st["kp"], st["vp"], st["cp"], st["hp"], st["ks"], st["vs"], st["cs"], st["hs"])
```

```python
import functools

import jax
import jax.numpy as jnp
from jax import lax
from jax.experimental import pallas as pl
from jax.experimental.pallas import tpu as pltpu

F32 = jnp.float32
BF16 = jnp.bfloat16

D_MODEL = 1024
N_META = 16
N_HEADS = 16
N_KV_HEADS = 2
HEAD_DIM = D_MODEL // N_HEADS
GROUP = N_HEADS // N_KV_HEADS
ATT_W = N_HEADS * HEAD_DIM
KV_W = N_KV_HEADS * HEAD_DIM
WINDOW = 128
BLOCK = 128
ATT_SCALE = HEAD_DIM ** -0.5
REC_W = 5 * D_MODEL // 4
REC_BLOCKS = 16
REC_BW = REC_W // REC_BLOCKS
REC_HALF = REC_W // 2
CONV_W = 4
LRU_C = 8.0
D_FF = 3 * D_MODEL
N_EXPERTS = 8
EXPERT_FF = 3 * D_MODEL // 2
PAST_LEN = 8192
RMS_EPS = 1e-6
IN_OFFS = (0, ATT_W, ATT_W + 2 * KV_W, ATT_W + 2 * KV_W + REC_W, ATT_W + 2 * KV_W + 2 * REC_W,
           ATT_W + 2 * KV_W + 2 * REC_W + 2 * D_MODEL)
LANES = 128
VMEM_LIMIT = 52 * 1024 * 1024

TM = 512
ROUTER_PAD_BIAS = -1e30


def _cparams(*sem):
    return pltpu.CompilerParams(dimension_semantics=sem, vmem_limit_bytes=VMEM_LIMIT)


def _resident(shape):
    nd = len(shape)
    return pl.BlockSpec(shape, lambda *_: (0,) * nd, pipeline_mode=pl.Buffered(1))


def _rms(x, g):
    return x * lax.rsqrt(jnp.mean(x * x, axis=-1, keepdims=True) + RMS_EPS) * g


SQRT_GUARD = 1e-30


def _sigmoid(x):
    return 0.5 * jnp.tanh(0.5 * x) + 0.5


NBV = 8


def _to_col_view(val, scr, out_ref, dtype):
    rows, width = val.shape
    for l in range(width // LANES):
        scr[l] = val[:, l * LANES:(l + 1) * LANES]
    for b in range(NBV):
        for l in range(width // LANES):
            out_ref[:, b * width + l * LANES:b * width + (l + 1) * LANES] = (
                scr[l, pl.ds(b, rows // NBV, stride=NBV), :].astype(dtype))


def _in_proj_kernel(x_ref, ln_ref, w_ref, q_ref, kv_ref, xr_ref, yr_ref, g_ref, q_scr, kv_scr):
    hn = _rms(x_ref[...], ln_ref[...]).astype(BF16)

    def proj(a, b):
        return jnp.dot(hn, w_ref[:, a:b], preferred_element_type=F32)

    _to_col_view(proj(IN_OFFS[0], IN_OFFS[1]) * ATT_SCALE, q_scr, q_ref, BF16)
    _to_col_view(proj(IN_OFFS[1], IN_OFFS[2]), kv_scr, kv_ref, F32)
    xr_ref[...] = proj(IN_OFFS[2], IN_OFFS[3])
    yr_ref[...] = proj(IN_OFFS[3], IN_OFFS[4])
    g_ref[...] = proj(IN_OFFS[4], IN_OFFS[5])


def _in_proj(x, ln, w_in):
    t = x.shape[0]
    row = lambda w: pl.BlockSpec((TM, w), lambda i: (i, 0))
    view = lambda w: pl.BlockSpec((TM // NBV, NBV * w), lambda i: (i, 0))
    return pl.pallas_call(
        _in_proj_kernel,
        grid=(t // TM,),
        in_specs=[row(D_MODEL), _resident((1, D_MODEL)), _resident(w_in.shape)],
        out_specs=[view(ATT_W), view(2 * KV_W), row(REC_W), row(REC_W), row(2 * D_MODEL)],
        out_shape=[jax.ShapeDtypeStruct((t // NBV, NBV * ATT_W), BF16),
                   jax.ShapeDtypeStruct((t // NBV, NBV * 2 * KV_W), F32),
                   jax.ShapeDtypeStruct((t, REC_W), F32), jax.ShapeDtypeStruct((t, REC_W), F32),
                   jax.ShapeDtypeStruct((t, 2 * D_MODEL), F32)],
        scratch_shapes=[pltpu.VMEM((ATT_W // LANES, TM, LANES), F32), pltpu.VMEM((2 * KV_W // LANES, TM, LANES), F32)],
        compiler_params=_cparams("parallel"),
        name="in_proj",
    )(x, ln, w_in)


def _sink_softmax(s, mask, sk):
    s = jnp.where(mask, s, -jnp.inf)
    m = jnp.maximum(jnp.max(s, axis=-1, keepdims=True), sk)
    p = jnp.exp(s - m)
    return p / (jnp.sum(p, axis=-1, keepdims=True) + jnp.exp(sk - m))


def _attn_prompt_kernel(pad, sink_ref, q_ref, kvp_ref, kvc_ref, o_ref):
    n = pl.program_id(1)
    q = q_ref[...]
    kv = jnp.concatenate([kvp_ref[...], kvc_ref[...]], axis=0).astype(BF16)
    i = lax.broadcasted_iota(jnp.int32, (BLOCK, 2 * BLOCK), 0)
    j = lax.broadcasted_iota(jnp.int32, (BLOCK, 2 * BLOCK), 1)
    mask = (j >= i) & (j <= i + WINDOW) & (j >= pad + BLOCK - n * BLOCK)
    outs = []
    for h in range(N_HEADS):
        kvh = h // GROUP
        qh = q[:, h * HEAD_DIM:(h + 1) * HEAD_DIM]
        k = kv[:, kvh * HEAD_DIM:(kvh + 1) * HEAD_DIM]
        v = kv[:, KV_W + kvh * HEAD_DIM:KV_W + (kvh + 1) * HEAD_DIM]
        s = lax.dot_general(qh, k, (((1,), (1,)), ((), ())), preferred_element_type=F32)
        s = jnp.where(mask, s, -jnp.inf)
        sk = sink_ref[h]
        m = jnp.maximum(jnp.max(s, axis=-1, keepdims=True), sk)
        p = jnp.exp(s - m)
        inv = 1.0 / (jnp.sum(p, axis=-1, keepdims=True) + jnp.exp(sk - m))
        outs.append(jnp.dot(p.astype(BF16), v, preferred_element_type=F32) * inv)
    o_ref[...] = jnp.concatenate(outs, axis=1).astype(BF16)


def _attn_prompt(q2, kv2, sinks, nb, lp, pad):
    return pl.pallas_call(
        functools.partial(_attn_prompt_kernel, pad),
        grid=(nb, lp // BLOCK),
        in_specs=[pl.BlockSpec(memory_space=pltpu.SMEM),
                  pl.BlockSpec((BLOCK, ATT_W), lambda b, n: (n, b)),
                  pl.BlockSpec((BLOCK, 2 * KV_W), lambda b, n: (jnp.maximum(n - 1, 0), b)),
                  pl.BlockSpec((BLOCK, 2 * KV_W), lambda b, n: (n, b))],
        out_specs=pl.BlockSpec((BLOCK, ATT_W), lambda b, n: (n, b)),
        out_shape=jax.ShapeDtypeStruct((lp, nb * ATT_W), BF16),
        compiler_params=_cparams("parallel", "parallel"),
        name="attn_prompt",
    )(sinks, q2, kv2, kv2)


SAMPLE_BB = 8


def _attn_sample_kernel(ds, sink_ref, q_ref, kvn_ref, ck_ref, cv_ref, o_ref, nk_ref, nv_ref):
    rows = GROUP * ds
    r = lax.broadcasted_iota(jnp.int32, (rows, WINDOW + ds), 0)
    j = lax.broadcasted_iota(jnp.int32, (rows, WINDOW + ds), 1)
    tq = r & (ds - 1)
    mask = (j >= tq) & (j <= tq + WINDOW)
    for bl in range(SAMPLE_BB):
        qb = q_ref[:, bl * ATT_W:(bl + 1) * ATT_W].astype(F32)
        kvb = kvn_ref[:, bl * 2 * KV_W:(bl + 1) * 2 * KV_W]
        ck = ck_ref[bl]
        cv = cv_ref[bl]
        nk_ref[bl] = jnp.concatenate([ck[ds:], kvb[:, :KV_W]], axis=0)
        nv_ref[bl] = jnp.concatenate([cv[ds:], kvb[:, KV_W:]], axis=0)
        pieces = []
        for kvh in range(N_KV_HEADS):
            lo, hi = kvh * HEAD_DIM, (kvh + 1) * HEAD_DIM
            k = jnp.concatenate([ck[:, lo:hi], kvb[:, lo:hi]], axis=0).astype(BF16)
            v = jnp.concatenate([cv[:, lo:hi], kvb[:, KV_W + lo:KV_W + hi]], axis=0).astype(BF16)
            qg = jnp.concatenate(
                [qb[:, (kvh * GROUP + g) * HEAD_DIM:(kvh * GROUP + g + 1) * HEAD_DIM] for g in range(GROUP)],
                axis=0).astype(BF16)
            s = lax.dot_general(qg, k, (((1,), (1,)), ((), ())), preferred_element_type=F32)
            s = jnp.where(mask, s, -jnp.inf)
            sk = sink_ref[kvh][:, 0:1]
            m = jnp.maximum(jnp.max(s, axis=-1, keepdims=True), sk)
            p = jnp.exp(s - m)
            inv = 1.0 / (jnp.sum(p, axis=-1, keepdims=True) + jnp.exp(sk - m))
            o = jnp.dot(p.astype(BF16), v, preferred_element_type=F32) * inv
            pieces += [o[g * ds:(g + 1) * ds] for g in range(GROUP)]
        o_ref[:, bl * ATT_W:(bl + 1) * ATT_W] = jnp.concatenate(pieces, axis=1).astype(BF16)


def _attn_sample(q, kv, cache_k, cache_v, sinks, nb, ds):
    q2 = q.reshape(ds, nb * ATT_W)
    kv2 = kv.reshape(ds, nb * 2 * KV_W)
    sink_rows = jnp.broadcast_to(
        jnp.repeat(sinks.reshape(N_KV_HEADS, GROUP), ds, axis=1)[:, :, None], (N_KV_HEADS, GROUP * ds, LANES))
    cache_spec = pl.BlockSpec((SAMPLE_BB, WINDOW, KV_W), lambda i: (i, 0, 0))
    att, nk, nv = pl.pallas_call(
        functools.partial(_attn_sample_kernel, ds),
        grid=(nb // SAMPLE_BB,),
        in_specs=[_resident(sink_rows.shape),
                  pl.BlockSpec((ds, SAMPLE_BB * ATT_W), lambda i: (0, i)),
                  pl.BlockSpec((ds, SAMPLE_BB * 2 * KV_W), lambda i: (0, i)),
                  cache_spec, cache_spec],
        out_specs=[pl.BlockSpec((ds, SAMPLE_BB * ATT_W), lambda i: (0, i)), cache_spec, cache_spec],
        out_shape=[jax.ShapeDtypeStruct((ds, nb * ATT_W), BF16),
                   jax.ShapeDtypeStruct(cache_k.shape, F32), jax.ShapeDtypeStruct(cache_v.shape, F32)],
        compiler_params=_cparams("parallel"),
        name="attn_sample",
    )(sink_rows, q2, kv2, cache_k, cache_v)
    return att.reshape(ds * nb, ATT_W), nk, nv


RG_ROWS = 256
RG_TCN = 64


def _rglru_kernel(nb, tcn, pos0, xr_ref, yr_ref, cprev_ref, hprev_ref, wc_ref, bc_ref, wg_ref, ba_ref, bi_ref,
                  lam_ref, rec_ref, hlast_ref, xbuf, a_scr, b_scr, hc):
    step = pl.program_id(0)
    rows = tcn * nb
    tail = (CONV_W - 1) * nb

    @pl.when(step == 0)
    def _():
        xbuf[0:tail, :] = cprev_ref[...]
        hc[...] = hprev_ref[...]

    @pl.when(step > 0)
    def _():
        xbuf[0:tail, :] = xbuf[rows:rows + tail, :]

    def row_pos(r0, n):
        t_row = lax.shift_right_logical(r0 + lax.broadcasted_iota(jnp.int32, (n, 1), 0), nb.bit_length() - 1)
        return pos0 + step * tcn + t_row

    sub = min(RG_ROWS, rows)
    for r0 in range(0, rows, sub):
        x = xr_ref[r0:r0 + sub, :]
        if pos0 < 0:
            x = jnp.where(row_pos(r0, sub) >= 0, x, 0.0)
        xbuf[tail + r0:tail + r0 + sub, :] = x

    lam = lam_ref[...]
    neg_c_softplus = -LRU_C * (jnp.maximum(-lam, 0.0) + jnp.log1p(jnp.exp(-jnp.abs(lam))))

    for r0 in range(0, rows, sub):
        xc = bc_ref[...]
        for jj in range(CONV_W):
            xc = xc + xbuf[r0 + jj * nb:r0 + jj * nb + sub, :] * wc_ref[jj:jj + 1, :]
        xcb = xc.astype(BF16)
        pos = row_pos(r0, sub)
        for c in range(2):
            lo, hi = c * REC_HALF, (c + 1) * REC_HALF
            gates = jnp.dot(xcb[:, lo:hi], wg_ref[c], preferred_element_type=F32)
            r = _sigmoid(gates[:, :REC_HALF] + ba_ref[:, lo:hi])
            ig = _sigmoid(gates[:, REC_HALF:] + bi_ref[:, lo:hi])
            a = jnp.exp(r * neg_c_softplus[:, lo:hi])
            u = 1.0 - a * a
            mult = u * lax.rsqrt(jnp.maximum(u, SQRT_GUARD))
            if pos0 <= 0:
                mult = jnp.where(pos > 0, mult, jnp.where(pos == 0, 1.0, 0.0))
            bt = mult * ig * xc[:, lo:hi]
            a_scr[r0:r0 + sub, lo:hi] = a
            b_scr[r0:r0 + sub, lo:hi] = bt

    if nb * REC_W <= 16 * 1024:
        def body(t, h):
            rr = pl.multiple_of(t * nb, nb)
            h = a_scr[pl.ds(rr, nb), :] * h + b_scr[pl.ds(rr, nb), :]
            b_scr[pl.ds(rr, nb), :] = h
            return h
        hc[...] = lax.fori_loop(0, tcn, body, hc[...])
    else:
        for t in range(tcn):
            h = a_scr[t * nb:(t + 1) * nb, :] * hc[...] + b_scr[t * nb:(t + 1) * nb, :]
            b_scr[t * nb:(t + 1) * nb, :] = h
            hc[...] = h

    hlast_ref[...] = hc[...]
    for r0 in range(0, rows, sub):
        rec_ref[r0:r0 + sub, :] = (b_scr[r0:r0 + sub, :] * jax.nn.gelu(yr_ref[r0:r0 + sub, :])).astype(BF16)


def _rglru(xr, yr, row_block0, n_chunks, nb, tcn, pos0, conv_prev, h_prev, wts):
    rows = tcn * nb
    tail = (CONV_W - 1) * nb
    blk = pl.BlockSpec((rows, REC_W), lambda i: (row_block0 + i, 0))
    rec, h_last = pl.pallas_call(
        functools.partial(_rglru_kernel, nb, tcn, pos0),
        grid=(n_chunks,),
        in_specs=[blk, blk, _resident(conv_prev.shape), _resident(h_prev.shape)] + [_resident(w.shape) for w in wts],
        out_specs=[pl.BlockSpec((rows, REC_W), lambda i: (i, 0)), pl.BlockSpec((nb, REC_W), lambda i: (0, 0))],
        out_shape=[jax.ShapeDtypeStruct((n_chunks * rows, REC_W), BF16), jax.ShapeDtypeStruct((nb, REC_W), F32)],
        scratch_shapes=[pltpu.VMEM((rows + tail, REC_W), F32), pltpu.VMEM((rows, REC_W), F32),
                        pltpu.VMEM((rows, REC_W), F32), pltpu.VMEM((nb, REC_W), F32)],
        compiler_params=_cparams("arbitrary"),
        name="rglru",
    )(xr, yr, conv_prev, h_prev, *wts)
    return rec, h_last


def _merge_kernel(n_p, attp_ref, atts_ref, recp_ref, recs_ref, g_ref, x_ref, wa_ref, wr_ref, wo_ref, o_ref,
                  att_scr, att_rows):
    is_p = pl.program_id(0) < n_p

    @pl.when(is_p)
    def _():
        for b in range(NBV):
            for l in range(ATT_W // LANES):
                att_scr[l, pl.ds(b, TM // NBV, stride=NBV), :] = (
                    attp_ref[:, b * ATT_W + l * LANES:b * ATT_W + (l + 1) * LANES].astype(F32))
        for l in range(ATT_W // LANES):
            att_rows[:, l * LANES:(l + 1) * LANES] = att_scr[l].astype(BF16)

    @pl.when(jnp.logical_not(is_p))
    def _():
        att_rows[...] = atts_ref[...]

    att = att_rows[...]
    rec = jnp.where(is_p, recp_ref[...], recs_ref[...])
    ya = jnp.dot(att, wa_ref[...], preferred_element_type=F32)
    yr = jnp.dot(rec, wr_ref[...], preferred_element_type=F32)
    merged = _sigmoid(g_ref[:, :D_MODEL]) * ya + _sigmoid(g_ref[:, D_MODEL:]) * yr
    o_ref[...] = x_ref[...] + jnp.dot(merged.astype(BF16), wo_ref[...], preferred_element_type=F32)


def _merge(att_p, att_s, rec_p, rec_s, g, x, wa, wr, wo):
    t = x.shape[0]
    n_p = rec_p.shape[0] // TM
    n_s = att_s.shape[0] // TM
    p_spec = lambda w: pl.BlockSpec((TM, w), lambda i: (jnp.minimum(i, n_p - 1), 0))
    s_spec = lambda w: pl.BlockSpec((TM, w), lambda i: (jnp.clip(i - n_p, 0, n_s - 1), 0))
    row = lambda w: pl.BlockSpec((TM, w), lambda i: (i, 0))
    view_spec = pl.BlockSpec((TM // NBV, NBV * ATT_W), lambda i: (jnp.minimum(i, n_p - 1), 0))
    return pl.pallas_call(
        functools.partial(_merge_kernel, n_p),
        grid=(t // TM,),
        in_specs=[view_spec, s_spec(ATT_W), p_spec(REC_W), s_spec(REC_W), row(2 * D_MODEL), row(D_MODEL),
                  _resident(wa.shape), _resident(wr.shape), _resident(wo.shape)],
        out_specs=row(D_MODEL),
        out_shape=jax.ShapeDtypeStruct((t, D_MODEL), F32),
        scratch_shapes=[pltpu.VMEM((ATT_W // LANES, TM, LANES), F32), pltpu.VMEM((TM, ATT_W), BF16)],
        compiler_params=_cparams("parallel"),
        name="merge_out",
    )(att_p, att_s, rec_p, rec_s, g, x, wa, wr, wo)


FF_CHUNK = 512


def _swiglu_mid(hn, wg_ref, wu_ref, mid_ref, width):
    for c in range(0, width, FF_CHUNK):
        gate = jnp.dot(hn, wg_ref[:, c:c + FF_CHUNK].astype(BF16), preferred_element_type=F32)
        up = jnp.dot(hn, wu_ref[:, c:c + FF_CHUNK].astype(BF16), preferred_element_type=F32)
        mid_ref[:, c:c + FF_CHUNK] = (jax.nn.silu(gate) * up).astype(BF16)


def _ffn_kernel(x_ref, ln_ref, wg_ref, wu_ref, wd_ref, o_ref, mid_ref):
    x = x_ref[...]
    hn = _rms(x, ln_ref[...]).astype(BF16)
    _swiglu_mid(hn, wg_ref, wu_ref, mid_ref, D_FF)
    o_ref[...] = x + jnp.dot(mid_ref[...], wd_ref[...], preferred_element_type=F32)


def _ffn(x, ln, wg, wu, wd):
    t = x.shape[0]
    row = pl.BlockSpec((TM, D_MODEL), lambda i: (i, 0))
    return pl.pallas_call(
        _ffn_kernel,
        grid=(t // TM,),
        in_specs=[row, _resident((1, D_MODEL)), _resident(wg.shape), _resident(wu.shape), _resident(wd.shape)],
        out_specs=row,
        out_shape=jax.ShapeDtypeStruct((t, D_MODEL), F32),
        scratch_shapes=[pltpu.VMEM((TM, D_FF), BF16)],
        compiler_params=_cparams("parallel"),
        name="ffn_dense",
    )(x, ln, wg, wu, wd)


TMG = 512
TOP_K = 2
DISPATCH_ROWS = 512
ZERO_TILES = 2 * N_EXPERTS
INFO_E, INFO_RANK, INFO_W = 0, 2, 4


def _route_kernel(x_ref, ln_ref, wr_ref, br_ref, info_ref, cnt_ref, base_scr):
    @pl.when(pl.program_id(0) == 0)
    def _():
        base_scr[...] = jnp.zeros_like(base_scr)

    hn = _rms(x_ref[...], ln_ref[...])
    lane = lax.broadcasted_iota(jnp.int32, (TM, LANES), 1).astype(F32)
    logits = jnp.dot(hn.astype(BF16), wr_ref[...], preferred_element_type=F32) + br_ref[...]
    m1 = jnp.max(logits, axis=-1, keepdims=True)
    i1 = jnp.min(jnp.where(logits == m1, lane, float(LANES)), axis=-1, keepdims=True)
    rest = jnp.where(lane == i1, -jnp.inf, logits)
    m2 = jnp.max(rest, axis=-1, keepdims=True)
    i2 = jnp.min(jnp.where(rest == m2, lane, float(LANES)), axis=-1, keepdims=True)
    e2 = jnp.exp(m2 - m1)
    den = 1.0 + e2
    hit1 = lane == i1
    hit2 = lane == i2
    onehot = jnp.where(hit1 | hit2, 1.0, 0.0)
    r = lax.broadcasted_iota(jnp.int32, (TM, TM), 0)
    c = lax.broadcasted_iota(jnp.int32, (TM, TM), 1)
    before = jnp.dot(jnp.where(c < r, 1.0, 0.0).astype(BF16), onehot.astype(BF16), preferred_element_type=F32)
    before = before + base_scr[0:1, :]
    rank1 = jnp.sum(jnp.where(hit1, before, 0.0), axis=-1, keepdims=True)
    rank2 = jnp.sum(jnp.where(hit2, before, 0.0), axis=-1, keepdims=True)
    base_scr[0:1, :] = base_scr[0:1, :] + jnp.sum(onehot, axis=0, keepdims=True)
    cnt_ref[...] = base_scr[...]
    info = jnp.zeros((TM, LANES), F32)
    for k, v in ((INFO_E, i1), (INFO_E + 1, i2), (INFO_RANK, rank1), (INFO_RANK + 1, rank2),
                 (INFO_W, 1.0 / den), (INFO_W + 1, e2 / den)):
        info = jnp.where(lane == float(k), v, info)
    info_ref[...] = info


def _route(x, ln, w_router, b_router):
    t = x.shape[0]
    row = lambda w: pl.BlockSpec((TM, w), lambda i: (i, 0))
    return pl.pallas_call(
        _route_kernel,
        grid=(t // TM,),
        in_specs=[row(D_MODEL), _resident((1, D_MODEL)), _resident(w_router.shape), _resident(b_router.shape)],
        out_specs=[row(LANES), pl.BlockSpec((8, LANES), lambda i: (0, 0))],
        out_shape=[jax.ShapeDtypeStruct((t, LANES), F32), jax.ShapeDtypeStruct((8, LANES), F32)],
        scratch_shapes=[pltpu.VMEM((8, LANES), F32)],
        compiler_params=_cparams("arbitrary"),
        name="moe_route",
    )(x, ln, w_router, b_router)


def _dispatch_kernel(zt_ref, pos_hbm, x_ref, ln_ref, xs_out, hn_ref, pos_smem, idx_sem, row_sem, zero_sem):
    i = pl.program_id(0)

    @pl.when(i == 0)
    def _():
        hn_ref[...] = jnp.zeros_like(hn_ref)
        zero = lambda m: pltpu.make_async_copy(
            hn_ref, xs_out.at[pl.ds(pl.multiple_of(zt_ref[m] * TMG, TMG), TMG)], zero_sem)
        for m in range(ZERO_TILES):
            @pl.when(zt_ref[m] >= 0)
            def _():
                zero(m).start()
        for m in range(ZERO_TILES):
            @pl.when(zt_ref[m] >= 0)
            def _():
                zero(m).wait()

    hn_ref[...] = _rms(x_ref[...], ln_ref[...])
    idx_copy = pltpu.make_async_copy(pos_hbm.at[i], pos_smem, idx_sem)
    idx_copy.start()
    idx_copy.wait()

    def issue(j, carry):
        base = pl.multiple_of(j * 8, 8)
        for u in range(8):
            for k in range(TOP_K):
                pltpu.make_async_copy(hn_ref.at[pl.ds(base + u, 1)],
                                      xs_out.at[pl.ds(pos_smem[TOP_K * (base + u) + k], 1)],
                                      row_sem).start(priority=k)
        return carry

    lax.fori_loop(0, DISPATCH_ROWS // 8, issue, 0)
    for k in range(TOP_K):
        pltpu.make_async_copy(hn_ref, xs_out.at[pl.ds(0, DISPATCH_ROWS)], row_sem).wait()


def _dispatch(zero_tiles, pos, x, ln, n_rows):
    t = x.shape[0]
    steps = t // DISPATCH_ROWS
    assert DISPATCH_ROWS == TMG
    return pl.pallas_call(
        _dispatch_kernel,
        grid_spec=pltpu.PrefetchScalarGridSpec(
            num_scalar_prefetch=1,
            grid=(steps,),
            in_specs=[pl.BlockSpec(memory_space=pl.ANY),
                      pl.BlockSpec((DISPATCH_ROWS, D_MODEL), lambda i, zt: (i, 0)),
                      pl.BlockSpec((1, D_MODEL), lambda i, zt: (0, 0))],
            out_specs=pl.BlockSpec(memory_space=pl.ANY),
            scratch_shapes=[pltpu.VMEM((DISPATCH_ROWS, D_MODEL), F32),
                            pltpu.SMEM((TOP_K * DISPATCH_ROWS,), jnp.int32), pltpu.SemaphoreType.DMA,
                            pltpu.SemaphoreType.DMA, pltpu.SemaphoreType.DMA]),
        out_shape=jax.ShapeDtypeStruct((n_rows, D_MODEL), F32),
        compiler_params=_cparams("arbitrary"),
        name="moe_dispatch",
    )(zero_tiles, pos.reshape(steps, TOP_K * DISPATCH_ROWS), x, ln)


def _expert_kernel(te_ref, nu_ref, xs_ref, wg_ref, wu_ref, wd_ref, y_ref, mid_ref):
    del te_ref

    @pl.when(pl.program_id(0) < nu_ref[0])
    def _():
        _swiglu_mid(xs_ref[...].astype(BF16), wg_ref, wu_ref, mid_ref, EXPERT_FF)
        y_ref[...] = jnp.dot(mid_ref[...], wd_ref[...].astype(BF16), preferred_element_type=F32)

    @pl.when(pl.program_id(0) >= nu_ref[0])
    def _():
        y_ref[...] = jnp.zeros_like(y_ref)


def _experts(tile_expert, n_used, xs, wg, wu, wd):
    n_rows = xs.shape[0]
    row = pl.BlockSpec((TMG, D_MODEL), lambda i, te, nu: (i, 0))
    ex = lambda a, b: pl.BlockSpec((None, a, b), lambda i, te, nu: (te[i], 0, 0))
    return pl.pallas_call(
        _expert_kernel,
        grid_spec=pltpu.PrefetchScalarGridSpec(
            num_scalar_prefetch=2,
            grid=(n_rows // TMG,),
            in_specs=[row, ex(D_MODEL, EXPERT_FF), ex(D_MODEL, EXPERT_FF), ex(EXPERT_FF, D_MODEL)],
            out_specs=row,
            scratch_shapes=[pltpu.VMEM((TMG, EXPERT_FF), BF16)]),
        out_shape=jax.ShapeDtypeStruct((n_rows, D_MODEL), F32),
        compiler_params=_cparams("arbitrary"),
        name="moe_experts",
    )(tile_expert, n_used, xs, wg, wu, wd)


def _combine_kernel(final, pos_hbm, x_ref, info_ref, ln_ref, y_hbm, *rest):
    if final:
        yp_ref, ys_ref, pos_smem, buf, idx_sem, row_sem, out_scr = rest
    else:
        o_ref, pos_smem, buf, idx_sem, row_sem = rest
    i = pl.program_id(0)
    n = pl.num_programs(0)

    def gather(tile, slot):
        idx_copy = pltpu.make_async_copy(pos_hbm.at[tile], pos_smem.at[slot], idx_sem)
        idx_copy.start()
        idx_copy.wait()

        def issue(j, carry):
            base = pl.multiple_of(j * 8, 8)
            for u in range(8):
                for k in range(TOP_K):
                    pltpu.make_async_copy(y_hbm.at[pl.ds(pos_smem[slot, TOP_K * (base + u) + k], 1)],
                                          buf.at[slot, k, pl.ds(base + u, 1)], row_sem.at[slot]).start(priority=k)
            return carry

        lax.fori_loop(0, TM // 8, issue, 0)

    @pl.when(i == 0)
    def _():
        gather(0, 0)

    for par in range(2):
        @pl.when((i + 1 < n) & ((i + 1) % 2 == par))
        def _():
            gather(i + 1, par)

    slot = i % 2
    for k in range(TOP_K):
        pltpu.make_async_copy(y_hbm.at[pl.ds(0, TM)], buf.at[slot, k], row_sem.at[slot]).wait()
    lane = lax.broadcasted_iota(jnp.int32, (TM, LANES), 1)
    info = info_ref[...]
    out = x_ref[...]
    for k in range(TOP_K):
        w_k = jnp.sum(jnp.where(lane == INFO_W + k, info, 0.0), axis=-1, keepdims=True)
        out = out + w_k * buf[slot, k]
    if not final:
        o_ref[...] = out
        return
    skip, n_p = final
    res = _rms(out, ln_ref[...])

    @pl.when((i >= skip) & (i < n_p))
    def _():
        for l in range(D_MODEL // LANES):
            out_scr[l] = res[:, l * LANES:(l + 1) * LANES]
        for l in range(D_MODEL // LANES):
            for b in range(NBV):
                yp_ref[b, :, l * LANES:(l + 1) * LANES] = out_scr[l, pl.ds(b, TM // NBV, stride=NBV), :]

    @pl.when(i >= n_p)
    def _():
        ys_ref[...] = res


def _combine(pos, x, info, y, final):
    t = x.shape[0]
    steps = t // TM
    row = lambda w: pl.BlockSpec((TM, w), lambda i: (i, 0))
    scratch = [pltpu.SMEM((2, TOP_K * TM), jnp.int32), pltpu.VMEM((2, TOP_K, TM, D_MODEL), F32),
               pltpu.SemaphoreType.DMA, pltpu.SemaphoreType.DMA((2,))]
    if final is None:
        ln, mode = jnp.ones((1, D_MODEL), F32), None
        out_specs = row(D_MODEL)
        out_shape = jax.ShapeDtypeStruct((t, D_MODEL), F32)
    else:
        ln, skip, n_p, n_s, bp, seq = final
        mode = (skip, n_p)
        out_specs = [pl.BlockSpec((NBV, TM // NBV, D_MODEL), lambda i: (0, jnp.clip(i - skip, 0, n_p - skip - 1), 0)),
                     pl.BlockSpec((TM, D_MODEL), lambda i: (jnp.clip(i - n_p, 0, n_s - 1), 0))]
        out_shape = [jax.ShapeDtypeStruct((bp, seq, D_MODEL), F32), jax.ShapeDtypeStruct((n_s * TM, D_MODEL), F32)]
        scratch = scratch + [pltpu.VMEM((D_MODEL // LANES, TM, LANES), F32)]
    return pl.pallas_call(
        functools.partial(_combine_kernel, mode),
        grid=(steps,),
        in_specs=[pl.BlockSpec(memory_space=pl.ANY), row(D_MODEL), row(LANES), _resident((1, D_MODEL)),
                  pl.BlockSpec(memory_space=pl.ANY)],
        out_specs=out_specs,
        out_shape=out_shape,
        scratch_shapes=scratch,
        compiler_params=_cparams("arbitrary"),
        name="moe_combine",
    )(pos.reshape(steps, TOP_K * TM), x, info, ln, y)


def _moe(x, ln, w_router, b_router, wg, wu, wd, final):
    t = x.shape[0]
    info, cnt = _route(x, ln, w_router, b_router)
    counts = cnt[0, :N_EXPERTS].astype(jnp.int32)
    tiles = (counts + TMG - 1) // TMG
    tile_end = jnp.cumsum(tiles)
    offsets = (tile_end - tiles) * TMG
    n_tiles = TOP_K * t // TMG + N_EXPERTS
    tile_expert = jnp.minimum(jnp.sum(jnp.arange(n_tiles)[:, None] >= tile_end[None, :], axis=1),
                              N_EXPERTS - 1).astype(jnp.int32)
    n_used = tile_end[N_EXPERTS - 1:].astype(jnp.int32)
    e = info[:, INFO_E:INFO_E + TOP_K].astype(jnp.int32)
    pos = (jnp.sum(jnp.where(e[..., None] == jnp.arange(N_EXPERTS), offsets, 0), axis=-1)
           + info[:, INFO_RANK:INFO_RANK + TOP_K].astype(jnp.int32)).reshape(-1)
    last_tile = jnp.where(tiles > 0, tile_end - 1, -1)
    spare = n_used[0] + jnp.arange(N_EXPERTS)
    zero_tiles = jnp.concatenate([last_tile, jnp.where(spare < n_tiles, spare, -1)]).astype(jnp.int32)
    xs = _dispatch(zero_tiles, pos, x, ln, n_tiles * TMG)
    y = _experts(tile_expert, n_used, xs, wg, wu, wd)
    return _combine(pos, x, info, y, final)


def _pack_kernel(skip, n_p, head_ref, xp_ref, xs_ref, o_ref, scr):
    i = pl.program_id(0)

    @pl.when(i < skip)
    def _():
        o_ref[...] = head_ref[...]

    @pl.when((i >= skip) & (i < n_p))
    def _():
        for l in range(D_MODEL // LANES):
            for b in range(NBV):
                scr[l, pl.ds(b, TM // NBV, stride=NBV), :] = xp_ref[b, :, l * LANES:(l + 1) * LANES]
        for l in range(D_MODEL // LANES):
            o_ref[:, l * LANES:(l + 1) * LANES] = scr[l]

    @pl.when(i >= n_p)
    def _():
        o_ref[...] = xs_ref[...]


def _pack_tokens(head, x_prompt, xs_tm):
    bp, seq, _ = x_prompt.shape
    skip = head.shape[0] // TM
    n_p = skip + bp * seq // TM
    n_s = xs_tm.shape[0] // TM
    return pl.pallas_call(
        functools.partial(_pack_kernel, skip, n_p),
        grid=(n_p + n_s,),
        in_specs=[pl.BlockSpec((TM, D_MODEL), lambda i: (jnp.minimum(i, skip - 1), 0)),
                  pl.BlockSpec((NBV, TM // NBV, D_MODEL), lambda i: (0, jnp.clip(i - skip, 0, n_p - skip - 1), 0)),
                  pl.BlockSpec((TM, D_MODEL), lambda i: (jnp.clip(i - n_p, 0, n_s - 1), 0))],
        out_specs=pl.BlockSpec((TM, D_MODEL), lambda i: (i, 0)),
        out_shape=jax.ShapeDtypeStruct(((n_p + n_s) * TM, D_MODEL), F32),
        scratch_shapes=[pltpu.VMEM((D_MODEL // LANES, TM, LANES), F32)],
        compiler_params=_cparams("parallel"),
        name="pack_tokens",
    )(head, x_prompt, xs_tm)


def _final_norm_kernel(x_ref, ln_ref, o_ref):
    o_ref[...] = _rms(x_ref[...], ln_ref[...])


def _final_norm(x, ln):
    t = x.shape[0]
    row = pl.BlockSpec((TM, D_MODEL), lambda i: (i, 0))
    return pl.pallas_call(
        _final_norm_kernel,
        grid=(t // TM,),
        in_specs=[row, _resident((1, D_MODEL))],
        out_specs=row,
        out_shape=jax.ShapeDtypeStruct((t, D_MODEL), F32),
        compiler_params=_cparams("parallel"),
        name="final_norm",
    )(x, ln)


def _gate_weights(w_a, w_i):
    half_blocks = REC_BLOCKS // 2
    eye = jnp.eye(half_blocks, dtype=F32)

    def dense(w):
        return (eye[:, None, :, None] * w[:, :, None, :]).reshape(REC_HALF, REC_HALF)

    halves = [jnp.concatenate([dense(w_a[c * half_blocks:(c + 1) * half_blocks]),
                               dense(w_i[c * half_blocks:(c + 1) * half_blocks])], axis=1) for c in range(2)]
    return jnp.stack(halves).astype(BF16)


def kernel(x_prompt, x_sample, cache_k, cache_v, state_conv, state_rglru, meta_tokens, ln_mix, w_in, attn_sinks,
           w_conv, b_conv, w_rg_a, b_rg_a, w_rg_i, b_rg_i, rg_lambda, w_att_proj, w_rec_proj, w_out, ln_ffn,
           w_ffn_gate, w_ffn_up, w_ffn_down, w_router, b_router, w_exp_gate, w_exp_up, w_exp_down, ln_final):
    bp, seq, _ = x_prompt.shape
    db, ds, _ = x_sample.shape
    depth = w_in.shape[0]
    pad = (-(N_META + seq)) % BLOCK
    lp = pad + N_META + seq
    tp, ts = lp * bp, ds * db
    assert tp % TM == 0 and ts % TM == 0 and db % SAMPLE_BB == 0 and ds >= CONV_W - 1
    assert seq >= WINDOW and bp & (bp - 1) == 0 and db & (db - 1) == 0 and ds & (ds - 1) == 0
    assert lp % RG_TCN == 0 and tp % ts == 0 and bp == NBV

    head_rows = (pad + N_META) * bp
    assert head_rows % TM == 0 and (seq * bp) % TM == 0
    head = jnp.concatenate([jnp.zeros((pad * bp, D_MODEL), F32), jnp.repeat(meta_tokens, bp, axis=0)], axis=0)
    x = _pack_tokens(head, x_prompt, jnp.transpose(x_sample, (1, 0, 2)).reshape(ts, D_MODEL))

    zeros_conv = jnp.zeros(((CONV_W - 1) * bp, REC_W), F32)
    zeros_h = jnp.zeros((bp, REC_W), F32)
    outs = {k: [] for k in ("kp", "vp", "cp", "hp", "ks", "vs", "cs", "hs")}

    for l in range(depth):
        q2, kv2, xr, yr, g = _in_proj(x, ln_mix[l].reshape(1, D_MODEL), w_in[l].astype(BF16))

        att_p = _attn_prompt(q2, kv2, attn_sinks[l], bp, lp, pad)
        att_s, nk, nv = _attn_sample(q2[lp:].reshape(ts, ATT_W), kv2[lp:].reshape(ts, 2 * KV_W),
                                     cache_k[l].reshape(db, WINDOW, KV_W),
                                     cache_v[l].reshape(db, WINDOW, KV_W), attn_sinks[l], db, ds)

        rg_w = (w_conv[l], b_conv[l].reshape(1, REC_W), _gate_weights(w_rg_a[l], w_rg_i[l]),
                b_rg_a[l].reshape(1, REC_W), b_rg_i[l].reshape(1, REC_W), rg_lambda[l].reshape(1, REC_W))
        rec_p, h_p = _rglru(xr, yr, 0, lp // RG_TCN, bp, RG_TCN, -pad, zeros_conv, zeros_h, rg_w)
        conv_s = jnp.transpose(state_conv[l], (1, 0, 2)).reshape((CONV_W - 1) * db, REC_W)
        rec_s, h_s = _rglru(xr, yr, tp // ts, 1, db, ds, PAST_LEN, conv_s, state_rglru[l], rg_w)

        x = _merge(att_p, att_s, rec_p, rec_s, g, x, w_att_proj[l].astype(BF16), w_rec_proj[l].astype(BF16),
                   w_out[l].astype(BF16))

        j = l // 2
        ln_f = ln_ffn[l].reshape(1, D_MODEL)
        if l % 2 == 0:
            x = _ffn(x, ln_f, w_ffn_gate[j].astype(BF16), w_ffn_up[j].astype(BF16), w_ffn_down[j].astype(BF16))
        else:
            w_r = jnp.zeros((D_MODEL, LANES), F32).at[:, :N_EXPERTS].set(w_router[j]).astype(BF16)
            b_r = jnp.full((1, LANES), ROUTER_PAD_BIAS, F32).at[0, :N_EXPERTS].set(b_router[j])
            x = _moe(x, ln_f, w_r, b_r, w_exp_gate[j], w_exp_up[j], w_exp_down[j],
                     (ln_final.reshape(1, D_MODEL), head_rows // TM, tp // TM, ts // TM, bp, seq)
                     if l == depth - 1 else None)

        kv_p = kv2[lp - WINDOW:lp].reshape(WINDOW, bp, 2 * KV_W)
        outs["kp"].append(jnp.transpose(kv_p[..., :KV_W], (1, 0, 2)).reshape(bp, WINDOW, N_KV_HEADS, HEAD_DIM))
        outs["vp"].append(jnp.transpose(kv_p[..., KV_W:], (1, 0, 2)).reshape(bp, WINDOW, N_KV_HEADS, HEAD_DIM))
        outs["cp"].append(jnp.transpose(xr[tp - (CONV_W - 1) * bp:tp].reshape(CONV_W - 1, bp, REC_W), (1, 0, 2)))
        outs["hp"].append(h_p)
        outs["ks"].append(nk.reshape(db, WINDOW, N_KV_HEADS, HEAD_DIM))
        outs["vs"].append(nv.reshape(db, WINDOW, N_KV_HEADS, HEAD_DIM))
        outs["cs"].append(jnp.transpose(xr[tp + ts - (CONV_W - 1) * db:].reshape(CONV_W - 1, db, REC_W), (1, 0, 2)))
        outs["hs"].append(h_s)

    if (depth - 1) % 2 == 1:
        y_prompt, ys_tm = x
    else:
        y = _final_norm(x, ln_final.reshape(1, D_MODEL))
        y_prompt = jnp.transpose(y[head_rows:tp].reshape(seq, bp, D_MODEL), (1, 0, 2))
        ys_tm = y[tp:]
    y_sample = jnp.transpose(ys_tm.reshape(ds, db, D_MODEL), (1, 0, 2))
    st = {k: jnp.stack(v) for k, v in outs.items()}
    return (y_prompt, y_sample, st["kp"], st["vp"], st["cp"], st["hp"], st["ks"], st["vs"], st["cs"], st["hs"])
```
